```python
import math
import jax, jax.numpy as jnp
from jax import lax
import numpy as np

D_MODEL = 1024
BATCH = 2
SEQ = 8192
DEPTH = 4

N_MIXERS = 3
N_META = 16
RMS_EPS = 1e-6
N_HEADS = 16
N_KV_HEADS = 4
HEAD_DIM = 64
GROUP = N_HEADS // N_KV_HEADS
WINDOW = 128
BLOCK = 128
N_BUCKETS = 32
MAX_DISTANCE = 128
CONV_WIDTH = 3
POOL_WINDOWS = (2, 4, 8, 16)
N_POOL_GROUPS = len(POOL_WINDOWS)
POOL_GROUP_DIM = D_MODEL // N_POOL_GROUPS
D_FF = ((8 * D_MODEL + 3 * 256 - 1) // (3 * 256)) * 256
N_ATTN = len(range(0, DEPTH, N_MIXERS))
N_CONV = len(range(1, DEPTH, N_MIXERS))
N_POOL = len(range(2, DEPTH, N_MIXERS))

kernel_name = "hybrid_swa_sink_shortconv_pool_decoder"


def rms_norm(x, g):
    xf = x.astype(jnp.float32)
    y = xf * lax.rsqrt(jnp.mean(xf * xf, axis=-1, keepdims=True) + RMS_EPS)
    return (y * g.astype(jnp.float32)).astype(x.dtype)


def rel_bucket(dist):
    max_exact = N_BUCKETS // 2
    d = jnp.maximum(dist, 0)
    df = jnp.maximum(d, 1).astype(jnp.float32)
    large = max_exact + (jnp.log(df / max_exact) / math.log(MAX_DISTANCE / max_exact)
                         * (N_BUCKETS - max_exact)).astype(jnp.int32)
    large = jnp.minimum(large, N_BUCKETS - 1)
    return jnp.where(d < max_exact, d, large)


def rel_bias(rel_table, dist):
    b = rel_table.astype(jnp.float32)[rel_bucket(dist)]
    return jnp.moveaxis(b, -1, 0)


def sliding_window_attention(h, w_qkv, b_qkv, w_o, b_o, sinks, rel_table):
    bsz, L, _ = h.shape
    S = L - N_META
    nb = S // BLOCK
    qkv = h @ w_qkv + b_qkv
    q, k, v = jnp.split(qkv, [N_HEADS * HEAD_DIM, (N_HEADS + N_KV_HEADS) * HEAD_DIM], axis=-1)
    q = q.reshape(bsz, L, N_KV_HEADS, GROUP, HEAD_DIM) * (HEAD_DIM ** -0.5)
    k = k.reshape(bsz, L, N_KV_HEADS, HEAD_DIM)
    v = v.reshape(bsz, L, N_KV_HEADS, HEAD_DIM)
    qm, qr = q[:, :N_META], q[:, N_META:]
    km, kr = k[:, :N_META], k[:, N_META:]
    vm, vr = v[:, :N_META], v[:, N_META:]
    sink = sinks.astype(jnp.float32).reshape(N_KV_HEADS, GROUP)

    im = jnp.arange(N_META)
    dist_mm = im[:, None] - im[None, :]
    s_mm = jnp.einsum('bqkgd,bmkd->bkgqm', qm, km).astype(jnp.float32)
    s_mm = s_mm + rel_bias(rel_table, dist_mm).reshape(N_KV_HEADS, GROUP, N_META, N_META)
    s_mm = jnp.where(dist_mm >= 0, s_mm, -jnp.inf)
    sink_mm = jnp.broadcast_to(sink[None, :, :, None, None], s_mm.shape[:-1] + (1,))
    p_mm = jax.nn.softmax(jnp.concatenate([s_mm, sink_mm], axis=-1), axis=-1)[..., :N_META]
    o_m = jnp.einsum('bkgqm,bmkd->bqkgd', p_mm.astype(v.dtype), vm)
    o_m = o_m.reshape(bsz, N_META, N_HEADS * HEAD_DIM)

    qb = qr.reshape(bsz, nb, BLOCK, N_KV_HEADS, GROUP, HEAD_DIM)
    kb = jnp.pad(kr, ((0, 0), (BLOCK, 0), (0, 0), (0, 0))).reshape(bsz, nb + 1, BLOCK, N_KV_HEADS, HEAD_DIM)
    vb = jnp.pad(vr, ((0, 0), (BLOCK, 0), (0, 0), (0, 0))).reshape(bsz, nb + 1, BLOCK, N_KV_HEADS, HEAD_DIM)
    k_band = jnp.concatenate([kb[:, :-1], kb[:, 1:]], axis=2)
    v_band = jnp.concatenate([vb[:, :-1], vb[:, 1:]], axis=2)

    iq = jnp.arange(BLOCK)[:, None]
    jk = jnp.arange(2 * BLOCK)[None, :]
    dist_band = BLOCK + iq - jk
    blk = jnp.arange(nb)[:, None, None]
    valid = (dist_band >= 0) & (dist_band < WINDOW) & ((blk > 0) | (jk >= BLOCK))
    bias_band = rel_bias(rel_table, dist_band).reshape(N_KV_HEADS, GROUP, BLOCK, 2 * BLOCK)

    qpos = N_META + jnp.arange(nb)[:, None] * BLOCK + jnp.arange(BLOCK)[None, :]
    dist_meta = qpos[:, :, None] - im[None, None, :]
    bias_meta = jnp.moveaxis(rel_bias(rel_table, dist_meta), 0, 1)
    bias_meta = bias_meta.reshape(nb, N_KV_HEADS, GROUP, BLOCK, N_META)

    s_band = jnp.einsum('bnqkgd,bnskd->bnkgqs', qb, k_band).astype(jnp.float32) + bias_band
    s_band = jnp.where(valid[None, :, None, None], s_band, -jnp.inf)
    s_meta = jnp.einsum('bnqkgd,bmkd->bnkgqm', qb, km).astype(jnp.float32) + bias_meta[None]
    sink_b = jnp.broadcast_to(sink[None, None, :, :, None, None], s_band.shape[:-1] + (1,))
    p = jax.nn.softmax(jnp.concatenate([s_meta, s_band, sink_b], axis=-1), axis=-1)
    p_meta = p[..., :N_META].astype(v.dtype)
    p_band = p[..., N_META:N_META + 2 * BLOCK].astype(v.dtype)
    o_r = (jnp.einsum('bnkgqm,bmkd->bnqkgd', p_meta, vm)
           + jnp.einsum('bnkgqs,bnskd->bnqkgd', p_band, v_band))
    o_r = o_r.reshape(bsz, S, N_HEADS * HEAD_DIM)

    o = jnp.concatenate([o_m, o_r], axis=1)
    return o @ w_o + b_o


def short_conv_mixer(h, w_in, conv_w, w_out):
    L = h.shape[1]
    gate_b, gate_c, u = jnp.split(h @ w_in, 3, axis=-1)
    z = gate_c * u
    zp = jnp.pad(z, ((0, 0), (CONV_WIDTH - 1, 0), (0, 0)))
    conv = sum(conv_w[t] * zp[:, t:t + L] for t in range(CONV_WIDTH))
    return (gate_b * conv) @ w_out


def pooling_mixer(h, w_pool, scale):
    bsz, L, D = h.shape
    hf = h.astype(jnp.float32).reshape(bsz, L, N_POOL_GROUPS, POOL_GROUP_DIM)
    cs = jnp.pad(lax.cumsum(hf, axis=1), ((0, 0), (1, 0), (0, 0), (0, 0)))
    t = jnp.arange(L)[:, None]
    win = jnp.array(POOL_WINDOWS, dtype=jnp.int32)[None, :]
    lo = jnp.maximum(t + 1 - win, 0)
    count = jnp.minimum(win, t + 1).astype(jnp.float32)
    lower = cs[:, lo, jnp.arange(N_POOL_GROUPS)[None, :], :]
    mix = (cs[:, 1:] - lower) / count[None, :, :, None] - hf
    out = jnp.einsum('blgc,gcd->blgd', mix.astype(h.dtype), w_pool).reshape(bsz, L, D)
    return out * scale


def swiglu(h, w_gate, w_up, w_down):
    return (jax.nn.silu(h @ w_gate) * (h @ w_up)) @ w_down


def setup_inputs(seed: int = 0) -> dict:
    key = jax.random.key(seed)
    ks = jax.random.split(key, 20)
    f32 = jnp.float32
    qkv_out = (N_HEADS + 2 * N_KV_HEADS) * HEAD_DIM
    nrm = lambda k, shape, s: jax.random.normal(k, shape, f32) * s
    return {
        "x": nrm(ks[0], (BATCH, SEQ, D_MODEL), 1.0),
        "meta_tokens": nrm(ks[1], (N_META, D_MODEL), 1.0),
        "rel_bias_table": nrm(ks[2], (N_BUCKETS, N_HEADS), 0.5),
        "norm_mix": 1.0 + nrm(ks[3], (DEPTH, D_MODEL), 0.02),
        "norm_ffn": 1.0 + nrm(ks[4], (DEPTH, D_MODEL), 0.02),
        "norm_final": 1.0 + nrm(ks[5], (D_MODEL,), 0.02),
        "attn_w_qkv": nrm(ks[6], (N_ATTN, D_MODEL, qkv_out), D_MODEL ** -0.5),
        "attn_b_qkv": nrm(ks[7], (N_ATTN, qkv_out), 0.02),
        "attn_w_o": nrm(ks[8], (N_ATTN, N_HEADS * HEAD_DIM, D_MODEL), (N_HEADS * HEAD_DIM) ** -0.5),
        "attn_b_o": nrm(ks[9], (N_ATTN, D_MODEL), 0.02),
        "attn_sinks": nrm(ks[10], (N_ATTN, N_HEADS), 1.0),
        "conv_w_in": nrm(ks[11], (N_CONV, D_MODEL, 3 * D_MODEL), D_MODEL ** -0.5),
        "conv_w": nrm(ks[12], (N_CONV, CONV_WIDTH, D_MODEL), CONV_WIDTH ** -0.5),
        "conv_w_out": nrm(ks[13], (N_CONV, D_MODEL, D_MODEL), D_MODEL ** -0.5),
        "pool_w": nrm(ks[14], (N_POOL, N_POOL_GROUPS, POOL_GROUP_DIM, POOL_GROUP_DIM), POOL_GROUP_DIM ** -0.5),
        "pool_scale": 1.0 + nrm(ks[15], (N_POOL, D_MODEL), 0.1),
        "ffn_w_gate": nrm(ks[16], (DEPTH, D_MODEL, D_FF), D_MODEL ** -0.5),
        "ffn_w_up": nrm(ks[17], (DEPTH, D_MODEL, D_FF), D_MODEL ** -0.5),
        "ffn_w_down": nrm(ks[18], (DEPTH, D_FF, D_MODEL), D_FF ** -0.5),
    }


def reference(x, meta_tokens, rel_bias_table, norm_mix, norm_ffn, norm_final,
              attn_w_qkv, attn_b_qkv, attn_w_o, attn_b_o, attn_sinks,
              conv_w_in, conv_w, conv_w_out,
              pool_w, pool_scale,
              ffn_w_gate, ffn_w_up, ffn_w_down):
    bsz = x.shape[0]
    meta = jnp.broadcast_to(meta_tokens.astype(x.dtype)[None], (bsz, N_META, D_MODEL))
    h = jnp.concatenate([meta, x], axis=1)
    for i in range(DEPTH):
        kind, j = i % N_MIXERS, i // N_MIXERS
        a = rms_norm(h, norm_mix[i])
        if kind == 0:
            m = sliding_window_attention(a, attn_w_qkv[j], attn_b_qkv[j], attn_w_o[j], attn_b_o[j],
                                         attn_sinks[j], rel_bias_table)
        elif kind == 1:
            m = short_conv_mixer(a, conv_w_in[j], conv_w[j], conv_w_out[j])
        else:
            m = pooling_mixer(a, pool_w[j], pool_scale[j])
        h = h + m.astype(h.dtype)
        h = h + swiglu(rms_norm(h, norm_ffn[i]), ffn_w_gate[i], ffn_w_up[i], ffn_w_down[i])
    h = rms_norm(h, norm_final)
    return h[:, N_META:]
```

```python
import functools
import math

import numpy as np
import jax
import jax.numpy as jnp
from jax import lax
from jax.experimental import pallas as pl
from jax.experimental.pallas import tpu as pltpu

F32 = jnp.float32
BF16 = jnp.bfloat16

D_MODEL = 1024
N_META = 16
RMS_EPS = 1e-6
N_HEADS = 16
N_KV = 4
HEAD_DIM = 64
GROUP = N_HEADS // N_KV
KV_DIM = N_KV * HEAD_DIM
WINDOW = 128
N_BUCKETS = 32
MAX_DISTANCE = 128
POOL_WINDOWS = (2, 4, 8, 16)
POOL_GROUP_DIM = D_MODEL // len(POOL_WINDOWS)
POOL_HIST = 16
CONV_TAIL = 8

Q_SUB = 64
BAND = Q_SUB + WINDOW
KEY_COLS = 256
SINK_COL = N_META + BAND
PAD_ROWS = KEY_COLS - SINK_COL
META_KEY_COLS = 128
CODE_SINK = N_BUCKETS
CODE_MASKED = N_BUCKETS + 1

VMEM_LIMIT = 56 * 1024 * 1024


def _rel_bucket_np(dist):
    max_exact = N_BUCKETS // 2
    d = np.maximum(dist, 0)
    df = np.maximum(d, 1).astype(np.float64)
    large = max_exact + (np.log(df / max_exact) / math.log(MAX_DISTANCE / max_exact)
                         * (N_BUCKETS - max_exact)).astype(np.int64)
    large = np.minimum(large, N_BUCKETS - 1)
    return np.where(d < max_exact, d, large).astype(np.int32)


def _real_codes():
    i = np.arange(Q_SUB)[:, None]
    out = []
    for q0 in (None, 0, Q_SUB):
        code = np.full((Q_SUB, KEY_COLS), CODE_MASKED, np.int32)
        m = np.arange(N_META)[None, :]
        if q0 is None:
            code[:, :N_META] = N_BUCKETS - 1
        else:
            code[:, :N_META] = _rel_bucket_np(N_META + q0 + i - m)
        t = np.arange(BAND)[None, :]
        dist = WINDOW + i - t
        valid = (dist >= 0) & (dist < WINDOW)
        if q0 is not None:
            valid &= (q0 - WINDOW + t) >= 0
        code[:, N_META:SINK_COL] = np.where(valid, _rel_bucket_np(dist), CODE_MASKED)
        code[:, SINK_COL] = CODE_SINK
        out.append(code)
    return np.stack(out)


def _meta_codes():
    i = np.arange(N_META)[:, None]
    m = np.arange(N_META)[None, :]
    code = np.full((N_META, META_KEY_COLS), CODE_MASKED, np.int32)
    code[:, :N_META] = np.where(i >= m, _rel_bucket_np(i - m), CODE_MASKED)
    code[:, N_META] = CODE_SINK
    return code[None]


def _const_spec(shape):
    zeros = (0,) * len(shape)
    return pl.BlockSpec(shape, lambda *_: zeros, pipeline_mode=pl.Buffered(1))


def _smem_spec():
    return pl.BlockSpec(memory_space=pltpu.SMEM)


def _rms(x, g):
    return x * lax.rsqrt(jnp.mean(x * x, axis=-1, keepdims=True) + RMS_EPS) * g


def _dot(a, b):
    return jnp.dot(a, b, preferred_element_type=F32)


def _bias_table_kernel(code_ref, tbl_ref, sink_ref, out_ref):
    n_var, rows, _ = code_ref.shape
    for v in range(n_var):
        code = code_ref[v]
        for h in range(N_HEADS):
            kh, g = divmod(h, GROUP)
            val = jnp.where(code == CODE_SINK, sink_ref[h], -jnp.inf)
            for b in range(N_BUCKETS):
                val = jnp.where(code == b, tbl_ref[b, h], val)
            out_ref[v, kh, g * rows:(g + 1) * rows, :] = val


def _bias_table(codes, rel_table, sinks):
    n_var, rows, cols = codes.shape
    return pl.pallas_call(
        _bias_table_kernel,
        out_shape=jax.ShapeDtypeStruct((n_var, N_KV, GROUP * rows, cols), F32),
        in_specs=[pl.BlockSpec(memory_space=pltpu.VMEM), _smem_spec(), _smem_spec()],
        out_specs=pl.BlockSpec(memory_space=pltpu.VMEM),
        name="bias_table",
    )(jnp.asarray(codes), rel_table, sinks)


def _attn_unit(qs, kc, vc, bias, kh):
    s = lax.dot_general(qs, kc, (((1,), (1,)), ((), ())), preferred_element_type=F32) + bias
    m = jnp.max(s, axis=-1, keepdims=True)
    p = jnp.exp(s - m)
    l = jnp.sum(p, axis=-1, keepdims=True)
    o = _dot(p.astype(BF16), vc)[:, kh * HEAD_DIM:(kh + 1) * HEAD_DIM]
    return o / l


def _split_qkv(qkv):
    nq = N_HEADS * HEAD_DIM
    q = (qkv[:, :nq] * (HEAD_DIM ** -0.5)).astype(BF16)
    k = qkv[:, nq:nq + KV_DIM].astype(BF16)
    v = qkv[:, nq + KV_DIM:].astype(BF16)
    return q, k, v


def _qkv_kernel(h_ref, g_ref, w_ref, b_ref, q_ref, k_ref, v_ref):
    a = _rms(h_ref[...], g_ref[...]).astype(BF16)
    q, k, v = _split_qkv(_dot(a, w_ref[...]) + b_ref[...])
    q_ref[...] = q
    k_ref[...] = k
    v_ref[...] = v


def _qkv_call(h, g, w, b, tm):
    n = h.shape[0]
    nq = N_HEADS * HEAD_DIM
    row = lambda c: pl.BlockSpec((tm, c), lambda i: (i, 0))
    return pl.pallas_call(
        _qkv_kernel,
        grid=(n // tm,),
        out_shape=(jax.ShapeDtypeStruct((n, nq), BF16),
                   jax.ShapeDtypeStruct((n, KV_DIM), BF16),
                   jax.ShapeDtypeStruct((n, KV_DIM), BF16)),
        in_specs=[row(D_MODEL), _const_spec((1, D_MODEL)), _const_spec(w.shape), _const_spec((1, w.shape[1]))],
        out_specs=(row(nq), row(KV_DIM), row(KV_DIM)),
        compiler_params=pltpu.CompilerParams(dimension_semantics=("parallel",), vmem_limit_bytes=VMEM_LIMIT),
        name="attn_qkv",
    )(h, g, w, b)


def _attn_core_kernel(q_ref, kp_ref, kc_ref, vp_ref, vc_ref, km_ref, vm_ref, tbl_ref, h_ref, wo_ref, bo_ref,
                      out_ref, o_scr, *, tm):
    first_tile = pl.program_id(1) == 0
    kext = jnp.concatenate([kp_ref[...], kc_ref[...]], axis=0)
    vext = jnp.concatenate([vp_ref[...], vc_ref[...]], axis=0)
    km = km_ref[...]
    vm = vm_ref[...]
    zk = jnp.zeros((PAD_ROWS, HEAD_DIM), BF16)
    zv = jnp.zeros((PAD_ROWS, KV_DIM), BF16)
    for j in range(tm // Q_SUB):
        r0 = j * Q_SUB
        var = jnp.where(first_tile, 1 + j, 0) if j < 2 else 0
        vcat = jnp.concatenate([vm, vext[r0:r0 + BAND], zv], axis=0)
        pieces = []
        for kh in range(N_KV):
            qs = jnp.concatenate(
                [q_ref[r0:r0 + Q_SUB, (kh * GROUP + g) * HEAD_DIM:(kh * GROUP + g + 1) * HEAD_DIM]
                 for g in range(GROUP)], axis=0)
            cols = slice(kh * HEAD_DIM, (kh + 1) * HEAD_DIM)
            kcat = jnp.concatenate([km[:, cols], kext[r0:r0 + BAND, cols], zk], axis=0)
            o = _attn_unit(qs, kcat, vcat, tbl_ref[var, kh], kh)
            pieces += [o[g * Q_SUB:(g + 1) * Q_SUB] for g in range(GROUP)]
        o_scr[r0:r0 + Q_SUB, :] = jnp.concatenate(pieces, axis=-1).astype(BF16)
    out_ref[...] = h_ref[...] + _dot(o_scr[...], wo_ref[...]) + bo_ref[...]


def _attn_core_call(q, k, v, km, vm, tbl, h, wo, bo, batch, tm):
    n = h.shape[0]
    nt = n // batch // tm
    nq = N_HEADS * HEAD_DIM
    per = tm // WINDOW
    cur = lambda c: pl.BlockSpec((tm, c), lambda b, i: (b * nt + i, 0))
    prev = pl.BlockSpec((WINDOW, KV_DIM), lambda b, i: (jnp.maximum((b * nt + i) * per - 1, 0), 0))
    return pl.pallas_call(
        functools.partial(_attn_core_kernel, tm=tm),
        grid=(batch, nt),
        out_shape=jax.ShapeDtypeStruct((n, D_MODEL), F32),
        in_specs=[cur(nq), prev, cur(KV_DIM), prev, cur(KV_DIM),
                  _const_spec(km.shape), _const_spec(vm.shape), _const_spec(tbl.shape),
                  cur(D_MODEL), _const_spec(wo.shape), _const_spec((1, D_MODEL))],
        out_specs=cur(D_MODEL),
        scratch_shapes=[pltpu.VMEM((tm, nq), BF16)],
        compiler_params=pltpu.CompilerParams(dimension_semantics=("parallel", "parallel"),
                                             vmem_limit_bytes=VMEM_LIMIT),
        name="attn_core",
    )(q, k, k, v, v, km, vm, tbl, h, wo, bo)


def _attn_meta_kernel(h_ref, g_ref, w_ref, b_ref, tbl_ref, wo_ref, bo_ref, out_ref, km_ref, vm_ref):
    x = h_ref[...]
    a = _rms(x, g_ref[...]).astype(BF16)
    q, k, v = _split_qkv(_dot(a, w_ref[...]) + b_ref[...])
    km_ref[...] = k
    vm_ref[...] = v
    pad = META_KEY_COLS - N_META
    vcat = jnp.concatenate([v, jnp.zeros((pad, KV_DIM), BF16)], axis=0)
    pieces = []
    for kh in range(N_KV):
        qs = jnp.concatenate([q[:, (kh * GROUP + g) * HEAD_DIM:(kh * GROUP + g + 1) * HEAD_DIM]
                              for g in range(GROUP)], axis=0)
        kcat = jnp.concatenate([k[:, kh * HEAD_DIM:(kh + 1) * HEAD_DIM], jnp.zeros((pad, HEAD_DIM), BF16)], axis=0)
        o = _attn_unit(qs, kcat, vcat, tbl_ref[0, kh], kh)
        pieces += [o[g * N_META:(g + 1) * N_META] for g in range(GROUP)]
    o_all = jnp.concatenate(pieces, axis=-1).astype(BF16)
    out_ref[...] = x + _dot(o_all, wo_ref[...]) + bo_ref[...]


def _attn_meta_call(h, g, w, b, tbl, wo, bo):
    vmem = pl.BlockSpec(memory_space=pltpu.VMEM)
    return pl.pallas_call(
        _attn_meta_kernel,
        out_shape=(jax.ShapeDtypeStruct((N_META, D_MODEL), F32),
                   jax.ShapeDtypeStruct((N_META, KV_DIM), BF16),
                   jax.ShapeDtypeStruct((N_META, KV_DIM), BF16)),
        in_specs=[vmem] * 7,
        out_specs=(vmem, vmem, vmem),
        compiler_params=pltpu.CompilerParams(vmem_limit_bytes=VMEM_LIMIT),
        name="attn_meta",
    )(h, g, w, b, tbl, wo, bo)


def _conv_kernel(h_ref, g_ref, win_ref, cw_ref, wout_ref, zinit_ref, out_ref, ztail_ref, zprev_scr, *, tm):
    @pl.when(pl.program_id(1) == 0)
    def _():
        zprev_scr[...] = zinit_ref[...]

    x = h_ref[...]
    a = _rms(x, g_ref[...]).astype(BF16)
    bcu = _dot(a, win_ref[...])
    gate_b = bcu[:, :D_MODEL]
    z = bcu[:, D_MODEL:2 * D_MODEL] * bcu[:, 2 * D_MODEL:]
    zp = zprev_scr[...]
    row = lax.broadcasted_iota(jnp.int32, (tm, D_MODEL), 0)
    last, last2 = zp[CONV_TAIL - 1:CONV_TAIL], zp[CONV_TAIL - 2:CONV_TAIL - 1]
    z1 = jnp.where(row == 0, last, pltpu.roll(z, 1, 0))
    z2 = jnp.where(row == 0, last2, jnp.where(row == 1, last, pltpu.roll(z, 2, 0)))
    cw = cw_ref[...]
    conv = cw[0:1] * z2 + cw[1:2] * z1 + cw[2:3] * z
    out_ref[...] = x + _dot((gate_b * conv).astype(BF16), wout_ref[...])
    tail = z[tm - CONV_TAIL:]
    zprev_scr[...] = tail
    ztail_ref[...] = tail


def _conv_call(h, g, win, cw, wout, zinit, batch, tm):
    n = h.shape[0]
    nt = n // batch // tm
    row = pl.BlockSpec((tm, D_MODEL), lambda b, i: (b * nt + i, 0))
    tail_shape = (CONV_TAIL, D_MODEL)
    return pl.pallas_call(
        functools.partial(_conv_kernel, tm=tm),
        grid=(batch, nt),
        out_shape=(jax.ShapeDtypeStruct((n, D_MODEL), F32), jax.ShapeDtypeStruct(tail_shape, F32)),
        in_specs=[row, _const_spec((1, D_MODEL)), _const_spec(win.shape), _const_spec(cw.shape),
                  _const_spec(wout.shape), _const_spec(tail_shape)],
        out_specs=(row, pl.BlockSpec(tail_shape, lambda b, i: (0, 0))),
        scratch_shapes=[pltpu.VMEM(tail_shape, F32)],
        compiler_params=pltpu.CompilerParams(dimension_semantics=("arbitrary", "arbitrary"),
                                             vmem_limit_bytes=VMEM_LIMIT),
        name="conv_mixer",
    )(h, g, win, cw, wout, zinit)


def _pool_kernel(h_ref, g_ref, wp_ref, sc_ref, hinit_ref, out_ref, atail_ref, hist_scr, *, tm, from_start):
    @pl.when(pl.program_id(1) == 0)
    def _():
        hist_scr[...] = hinit_ref[...]

    x = h_ref[...]
    a = _rms(x, g_ref[...])
    ext = jnp.concatenate([hist_scr[...], a], axis=0)
    outs = []
    for gi, win in enumerate(POOL_WINDOWS):
        cols = slice(gi * POOL_GROUP_DIM, (gi + 1) * POOL_GROUP_DIM)
        s = ext[:, cols]
        k = 1
        while k < win:
            s = s + pltpu.roll(s, k, 0)
            k *= 2
        s = s[POOL_HIST:]
        if from_start:
            pos = lax.broadcasted_iota(jnp.int32, (tm, POOL_GROUP_DIM), 0)
            mean = s / jnp.minimum(win, pos + 1).astype(F32)
        else:
            mean = s * (1.0 / win)
        mix = (mean - a[:, cols]).astype(BF16)
        outs.append(_dot(mix, wp_ref[gi]))
    out_ref[...] = x + jnp.concatenate(outs, axis=-1) * sc_ref[...]
    tail = a[tm - POOL_HIST:]
    hist_scr[...] = tail
    atail_ref[...] = tail


def _pool_call(h, g, wp, sc, hinit, batch, tm, from_start):
    n = h.shape[0]
    nt = n // batch // tm
    row = pl.BlockSpec((tm, D_MODEL), lambda b, i: (b * nt + i, 0))
    hist_shape = (POOL_HIST, D_MODEL)
    return pl.pallas_call(
        functools.partial(_pool_kernel, tm=tm, from_start=from_start),
        grid=(batch, nt),
        out_shape=(jax.ShapeDtypeStruct((n, D_MODEL), F32), jax.ShapeDtypeStruct(hist_shape, F32)),
        in_specs=[row, _const_spec((1, D_MODEL)), _const_spec(wp.shape), _const_spec((1, D_MODEL)),
                  _const_spec(hist_shape)],
        out_specs=(row, pl.BlockSpec(hist_shape, lambda b, i: (0, 0))),
        scratch_shapes=[pltpu.VMEM(hist_shape, F32)],
        compiler_params=pltpu.CompilerParams(dimension_semantics=("arbitrary", "arbitrary"),
                                             vmem_limit_bytes=VMEM_LIMIT),
        name="pool_mixer",
    )(h, g, wp, sc, hinit)


def _ffn_kernel(h_ref, g_ref, wg_ref, wu_ref, wd_ref, gf_ref, out_ref, *, final_norm):
    x = h_ref[...]
    a = _rms(x, g_ref[...]).astype(BF16)
    gate = _dot(a, wg_ref[...])
    up = _dot(a, wu_ref[...])
    hmid = (gate * (1.0 / (1.0 + jnp.exp(-gate))) * up).astype(BF16)
    y = x + _dot(hmid, wd_ref[...])
    out_ref[...] = _rms(y, gf_ref[...]) if final_norm else y


def _ffn_call(h, g, wg, wu, wd, gf, tm, final_norm):
    n = h.shape[0]
    row = pl.BlockSpec((tm, D_MODEL), lambda i: (i, 0))
    return pl.pallas_call(
        functools.partial(_ffn_kernel, final_norm=final_norm),
        grid=(n // tm,),
        out_shape=jax.ShapeDtypeStruct((n, D_MODEL), F32),
        in_specs=[row, _const_spec((1, D_MODEL)), _const_spec(wg.shape), _const_spec(wu.shape),
                  _const_spec(wd.shape), _const_spec((1, D_MODEL))],
        out_specs=row,
        compiler_params=pltpu.CompilerParams(dimension_semantics=("parallel",), vmem_limit_bytes=VMEM_LIMIT),
        name="ffn",
    )(h, g, wg, wu, wd, gf)


ATTN_TM = 256
ROW_TM = 512


def kernel(x, meta_tokens, rel_bias_table, norm_mix, norm_ffn, norm_final, attn_w_qkv, attn_b_qkv, attn_w_o,
           attn_b_o, attn_sinks, conv_w_in, conv_w, conv_w_out, pool_w, pool_scale, ffn_w_gate, ffn_w_up,
           ffn_w_down):
    batch, seq, _ = x.shape
    depth = norm_mix.shape[0]
    n_mixers = 3
    hr = x.reshape(batch * seq, D_MODEL)
    hm = meta_tokens.astype(F32)
    row = lambda v: v.reshape(1, -1)
    real_codes, meta_codes = _real_codes(), _meta_codes()
    for i in range(depth):
        kind, j = i % n_mixers, i // n_mixers
        last = i == depth - 1
        gm = row(norm_mix[i])
        if kind == 0:
            w, b = attn_w_qkv[j].astype(BF16), row(attn_b_qkv[j])
            wo, bo = attn_w_o[j].astype(BF16), row(attn_b_o[j])
            tbl_meta = _bias_table(meta_codes, rel_bias_table, attn_sinks[j])
            tbl_real = _bias_table(real_codes, rel_bias_table, attn_sinks[j])
            hm, km, vm = _attn_meta_call(hm, gm, w, b, tbl_meta, wo, bo)
            q, k, v = _qkv_call(hr, gm, w, b, ROW_TM)
            hr = _attn_core_call(q, k, v, km, vm, tbl_real, hr, wo, bo, batch, ATTN_TM)
        elif kind == 1:
            win, wout = conv_w_in[j].astype(BF16), conv_w_out[j].astype(BF16)
            hm, ztail = _conv_call(hm, gm, win, conv_w[j], wout, jnp.zeros((CONV_TAIL, D_MODEL), F32), 1, N_META)
            hr, _ = _conv_call(hr, gm, win, conv_w[j], wout, ztail, batch, ROW_TM)
        else:
            wp, sc = pool_w[j].astype(BF16), row(pool_scale[j])
            hm, atail = _pool_call(hm, gm, wp, sc, jnp.zeros((POOL_HIST, D_MODEL), F32), 1, N_META, True)
            hr, _ = _pool_call(hr, gm, wp, sc, atail, batch, ROW_TM, False)
        gf = row(norm_ffn[i])
        wg, wu, wd = ffn_w_gate[i].astype(BF16), ffn_w_up[i].astype(BF16), ffn_w_down[i].astype(BF16)
        if not last:
            hm = _ffn_call(hm, gf, wg, wu, wd, row(norm_final), N_META, False)
        hr = _ffn_call(hr, gf, wg, wu, wd, row(norm_final), ROW_TM, last)
    return hr.reshape(batch, seq, D_MODEL)
```

```python
import functools
import math

import numpy as np
import jax
import jax.numpy as jnp
from jax import lax
from jax.experimental import pallas as pl
from jax.experimental.pallas import tpu as pltpu

F32 = jnp.float32
BF16 = jnp.bfloat16

D_MODEL = 1024
N_META = 16
RMS_EPS = 1e-6
N_HEADS = 16
N_KV = 4
HEAD_DIM = 64
GROUP = N_HEADS // N_KV
Q_DIM = N_HEADS * HEAD_DIM
KV_DIM = N_KV * HEAD_DIM
WINDOW = 128
N_BUCKETS = 32
MAX_DISTANCE = 128
POOL_WINDOWS = (2, 4, 8, 16)
POOL_GROUP_DIM = D_MODEL // len(POOL_WINDOWS)
POOL_HIST = 16
CONV_TAIL = 8
LANES = 128
LOG2E = math.log2(math.e)

KEY_ROWS = 2 * WINDOW + 2 * N_META
SINK_ROW = 2 * WINDOW + N_META
ONES_ROWS = 16

META_KEY_COLS = 128
CODE_SINK = N_BUCKETS
CODE_MASKED = N_BUCKETS + 1

VMEM_LIMIT = 56 * 1024 * 1024


def _rel_bucket_np(dist):
    max_exact = N_BUCKETS // 2
    d = np.maximum(dist, 0)
    df = np.maximum(d, 1).astype(np.float64)
    large = max_exact + (np.log(df / max_exact) / math.log(MAX_DISTANCE / max_exact)
                         * (N_BUCKETS - max_exact)).astype(np.int64)
    large = np.minimum(large, N_BUCKETS - 1)
    return np.where(d < max_exact, d, large).astype(np.int32)


def _meta_codes():
    i = np.arange(N_META)[:, None]
    m = np.arange(N_META)[None, :]
    code = np.full((N_META, META_KEY_COLS), CODE_MASKED, np.int32)
    code[:, :N_META] = np.where(i >= m, _rel_bucket_np(i - m), CODE_MASKED)
    code[:, N_META] = CODE_SINK
    return code[None]


def _band_codes():
    return np.stack([_rel_bucket_np(np.arange(WINDOW)), np.zeros(WINDOW, np.int32)])


def _first_meta_codes():
    m = np.arange(N_META)[:, None]
    i = np.arange(WINDOW)[None, :]
    return _rel_bucket_np(N_META + i - m)


def _const_spec(shape):
    zeros = (0,) * len(shape)
    return pl.BlockSpec(shape, lambda *_: zeros, pipeline_mode=pl.Buffered(1))


def _smem_spec():
    return pl.BlockSpec(memory_space=pltpu.SMEM)


def _rms(x, g):
    return x * lax.rsqrt(jnp.mean(x * x, axis=-1, keepdims=True) + RMS_EPS) * g


def _dot(a, b):
    return jnp.dot(a, b, preferred_element_type=F32)


def _dot_nt(a, b):
    return lax.dot_general(a, b, (((1,), (1,)), ((), ())), preferred_element_type=F32)


def _dot_tn(a, b):
    return lax.dot_general(a, b, (((0,), (0,)), ((), ())), preferred_element_type=F32)


def _select_by_code(code, tbl_ref, h, otherwise):
    val = otherwise
    for b in range(N_BUCKETS):
        val = jnp.where(code == b, tbl_ref[b, h], val)
    return val


def _meta_table_kernel(code_ref, tbl_ref, sink_ref, out_ref):
    n_var, rows, _ = code_ref.shape
    for v in range(n_var):
        code = code_ref[v]
        for h in range(N_HEADS):
            kh, g = divmod(h, GROUP)
            val = _select_by_code(code, tbl_ref, h, jnp.where(code == CODE_SINK, sink_ref[h], -jnp.inf))
            out_ref[v, kh, g * rows:(g + 1) * rows, :] = val


def _meta_table(codes, rel_table, sinks):
    n_var, rows, cols = codes.shape
    return pl.pallas_call(
        _meta_table_kernel,
        out_shape=jax.ShapeDtypeStruct((n_var, N_KV, GROUP * rows, cols), F32),
        in_specs=[pl.BlockSpec(memory_space=pltpu.VMEM), _smem_spec(), _smem_spec()],
        out_specs=pl.BlockSpec(memory_space=pltpu.VMEM),
        name="meta_bias_table",
    )(jnp.asarray(codes), rel_table, sinks)


def _real_table_kernel(band_code_ref, meta_code_ref, tbl_ref, sink_ref, out_ref):
    neg = -jnp.inf
    band_code = band_code_ref[0:1, :]
    meta_code = meta_code_ref[...]
    t = lax.broadcasted_iota(jnp.int32, (WINDOW, WINDOW), 0)
    i = lax.broadcasted_iota(jnp.int32, (WINDOW, WINDOW), 1)
    row16 = lax.broadcasted_iota(jnp.int32, (N_META, WINDOW), 0)
    for h in range(N_HEADS):
        kh, g = divmod(h, GROUP)
        lanes = slice(g * WINDOW, (g + 1) * WINDOW)
        by_dist = _select_by_code(band_code, tbl_ref, h, jnp.zeros((1, WINDOW), F32)) * LOG2E
        rolled = pltpu.roll(jnp.broadcast_to(by_dist, (WINDOW, WINDOW)), 0, 1, stride=1, stride_axis=0)
        prev_blk = jnp.where(i < t, rolled, neg)
        cur_blk = jnp.where(i >= t, rolled, neg)
        far = jnp.full((N_META, WINDOW), tbl_ref[N_BUCKETS - 1, h], F32)
        near = _select_by_code(meta_code, tbl_ref, h, far) * LOG2E
        far = far * LOG2E
        sink_pad = jnp.where(row16 == 0, sink_ref[h] * LOG2E, neg)
        for v in range(2):
            out_ref[v, kh, 0:WINDOW, lanes] = jnp.full((WINDOW, WINDOW), neg, F32) if v else prev_blk
            out_ref[v, kh, WINDOW:2 * WINDOW, lanes] = cur_blk
            out_ref[v, kh, 2 * WINDOW:SINK_ROW, lanes] = near if v else far
            out_ref[v, kh, SINK_ROW:KEY_ROWS, lanes] = sink_pad


def _real_table(rel_table, sinks):
    vmem = pl.BlockSpec(memory_space=pltpu.VMEM)
    return pl.pallas_call(
        _real_table_kernel,
        out_shape=jax.ShapeDtypeStruct((2, N_KV, KEY_ROWS, GROUP * WINDOW), F32),
        in_specs=[vmem, vmem, _smem_spec(), _smem_spec()],
        out_specs=vmem,
        name="real_bias_table",
    )(jnp.asarray(_band_codes()), jnp.asarray(_first_meta_codes()), rel_table, sinks)


def _qkv_kernel(h_ref, g_ref, wqv_ref, bqv_ref, wk_ref, bk_ref, qT_ref, k_ref, vT_ref, *, tm):
    a = _rms(h_ref[...], g_ref[...]).astype(BF16)
    bias = jnp.concatenate([bqv_ref[...]] * (tm // LANES), axis=1)
    qvT = _dot_nt(wqv_ref[...], a) + bias
    qT_ref[...] = (qvT[:Q_DIM] * (HEAD_DIM ** -0.5 * LOG2E)).astype(BF16)
    vT_ref[...] = qvT[Q_DIM:].astype(BF16)
    k_ref[...] = (_dot(a, wk_ref[...]) + bk_ref[...]).astype(BF16)


def _qkv_call(h, g, wqvT, bqvT, wk, bk, tm):
    n = h.shape[0]
    row = lambda c: pl.BlockSpec((tm, c), lambda i: (i, 0))
    col = lambda r: pl.BlockSpec((r, tm), lambda i: (0, i))
    return pl.pallas_call(
        functools.partial(_qkv_kernel, tm=tm),
        grid=(n // tm,),
        out_shape=(jax.ShapeDtypeStruct((Q_DIM, n), BF16),
                   jax.ShapeDtypeStruct((n, KV_DIM), BF16),
                   jax.ShapeDtypeStruct((KV_DIM, n), BF16)),
        in_specs=[row(D_MODEL), _const_spec((1, D_MODEL)), _const_spec(wqvT.shape), _const_spec(bqvT.shape),
                  _const_spec(wk.shape), _const_spec((1, KV_DIM))],
        out_specs=(col(Q_DIM), row(KV_DIM), col(KV_DIM)),
        compiler_params=pltpu.CompilerParams(dimension_semantics=("parallel",), vmem_limit_bytes=VMEM_LIMIT),
        name="attn_qkv",
    )(h, g, wqvT, bqvT, wk, bk)


def _attn_core_kernel(qT_ref, kp_ref, kc_ref, vTp_ref, vTc_ref, km_ref, vmT_ref, tbl_ref, h_ref, wo_ref, bo_ref,
                      out_ref, oT_scr, *, tm):
    first_tile = pl.program_id(1) == 0
    kext = jnp.concatenate([kp_ref[...], kc_ref[...]], axis=0)
    vText = jnp.concatenate([vTp_ref[...], vTc_ref[...]], axis=1)
    km = km_ref[...]
    vmT = vmT_ref[...]
    zk = jnp.zeros((KEY_ROWS - SINK_ROW, HEAD_DIM), BF16)
    ones = jnp.ones((ONES_ROWS, KEY_ROWS), BF16)

    def scores(j, kh):
        r0 = j * WINDOW
        cols = slice(kh * HEAD_DIM, (kh + 1) * HEAD_DIM)
        qsT = jnp.concatenate(
            [qT_ref[(kh * GROUP + g) * HEAD_DIM:(kh * GROUP + g + 1) * HEAD_DIM, r0:r0 + WINDOW]
             for g in range(GROUP)], axis=1)
        keys = jnp.concatenate([kext[r0:r0 + 2 * WINDOW, cols], km[:, cols], zk], axis=0)
        return _dot(keys, qsT)

    def finish(j, kh, sT, sT_next):
        r0 = j * WINDOW
        cols = slice(kh * HEAD_DIM, (kh + 1) * HEAD_DIM)
        var = jnp.where(first_tile, 1, 0) if j == 0 else 0
        sT = sT + tbl_ref[var, kh]
        m = jnp.max(sT, axis=0, keepdims=True)
        pT = jnp.exp2(sT - m).astype(BF16)
        valsT = jnp.concatenate([vText[cols, r0:r0 + 2 * WINDOW], vmT[cols, :]], axis=1)
        den = ones
        if sT_next is not None:
            bits = pltpu.bitcast(sT_next[KEY_ROWS - ONES_ROWS:, :2 * WINDOW], jnp.uint32)
            zero = pltpu.bitcast((bits >> 16) >> 16, F32).astype(BF16)
            den = jnp.concatenate([ones[:, :2 * WINDOW] + zero, ones[:, 2 * WINDOW:]], axis=1)
        oT = _dot(jnp.concatenate([valsT, den], axis=0), pT)
        oT = oT[:HEAD_DIM] / oT[HEAD_DIM:HEAD_DIM + 1]
        for g in range(GROUP):
            h0 = (kh * GROUP + g) * HEAD_DIM
            oT_scr[h0:h0 + HEAD_DIM, r0:r0 + WINDOW] = oT[:, g * WINDOW:(g + 1) * WINDOW].astype(BF16)

    units = [(j, kh) for j in range(tm // WINDOW) for kh in range(N_KV)]
    sT = scores(*units[0])
    for idx, unit in enumerate(units):
        sT_next = scores(*units[idx + 1]) if idx + 1 < len(units) else None
        finish(*unit, sT, sT_next)
        sT = sT_next
    out_ref[...] = h_ref[...] + _dot_tn(oT_scr[...], wo_ref[...]) + bo_ref[...]


def _attn_core_call(qT, k, vT, km, vmT, tbl, h, wo, bo, batch, tm):
    n = h.shape[0]
    nt = n // batch // tm
    per = tm // WINDOW
    prev_idx = lambda b, i: jnp.maximum((b * nt + i) * per - 1, 0)
    row = lambda c: pl.BlockSpec((tm, c), lambda b, i: (b * nt + i, 0))
    col = lambda r: pl.BlockSpec((r, tm), lambda b, i: (0, b * nt + i))
    return pl.pallas_call(
        functools.partial(_attn_core_kernel, tm=tm),
        grid=(batch, nt),
        out_shape=jax.ShapeDtypeStruct((n, D_MODEL), F32),
        in_specs=[col(Q_DIM),
                  pl.BlockSpec((WINDOW, KV_DIM), lambda b, i: (prev_idx(b, i), 0)), row(KV_DIM),
                  pl.BlockSpec((KV_DIM, WINDOW), lambda b, i: (0, prev_idx(b, i))), col(KV_DIM),
                  _const_spec(km.shape), _const_spec(vmT.shape), _const_spec(tbl.shape),
                  row(D_MODEL), _const_spec(wo.shape), _const_spec((1, D_MODEL))],
        out_specs=row(D_MODEL),
        scratch_shapes=[pltpu.VMEM((Q_DIM, tm), BF16)],
        compiler_params=pltpu.CompilerParams(dimension_semantics=("parallel", "parallel"),
                                             vmem_limit_bytes=VMEM_LIMIT),
        name="attn_core",
    )(qT, k, k, vT, vT, km, vmT, tbl, h, wo, bo)


def _attn_meta_kernel(h_ref, g_ref, wqv_ref, bqv_ref, wk_ref, b_ref, tbl_ref, wo_ref, bo_ref,
                      out_ref, km_ref, vmT_ref):
    x = h_ref[...]
    a = _rms(x, g_ref[...]).astype(BF16)
    a2 = jnp.concatenate([a, jnp.zeros_like(a)], axis=0)
    lane = lax.broadcasted_iota(jnp.int32, (KV_DIM, 2 * N_META), 1)
    vT = _dot_nt(wqv_ref[Q_DIM:, :], a2) + jnp.where(lane < N_META, bqv_ref[Q_DIM:, :2 * N_META], 0.0)
    vmT_ref[...] = vT.astype(BF16)
    b = b_ref[...]
    q = ((_dot_nt(a, wqv_ref[:Q_DIM, :]) + b[:, :Q_DIM]) * (HEAD_DIM ** -0.5)).astype(BF16)
    k = (_dot(a, wk_ref[...]) + b[:, Q_DIM:Q_DIM + KV_DIM]).astype(BF16)
    v = (_dot_nt(a, wqv_ref[Q_DIM:, :]) + b[:, Q_DIM + KV_DIM:]).astype(BF16)
    km_ref[...] = k
    pad = META_KEY_COLS - N_META
    vcat = jnp.concatenate([v, jnp.zeros((pad, KV_DIM), BF16)], axis=0)
    pieces = []
    for kh in range(N_KV):
        cols = slice(kh * HEAD_DIM, (kh + 1) * HEAD_DIM)
        qs = jnp.concatenate([q[:, (kh * GROUP + g) * HEAD_DIM:(kh * GROUP + g + 1) * HEAD_DIM]
                              for g in range(GROUP)], axis=0)
        kcat = jnp.concatenate([k[:, cols], jnp.zeros((pad, HEAD_DIM), BF16)], axis=0)
        s = _dot_nt(qs, kcat) + tbl_ref[0, kh]
        p = jnp.exp(s - jnp.max(s, axis=-1, keepdims=True))
        l = jnp.sum(p, axis=-1, keepdims=True)
        o = _dot(p.astype(BF16), vcat)[:, cols] / l
        pieces += [o[g * N_META:(g + 1) * N_META] for g in range(GROUP)]
    o_all = jnp.concatenate(pieces, axis=-1).astype(BF16)
    out_ref[...] = x + _dot(o_all, wo_ref[...]) + bo_ref[...]


def _attn_meta_call(h, g, wqvT, bqvT, wk, bk, tbl, wo, bo):
    vmem = pl.BlockSpec(memory_space=pltpu.VMEM)
    return pl.pallas_call(
        _attn_meta_kernel,
        out_shape=(jax.ShapeDtypeStruct((N_META, D_MODEL), F32),
                   jax.ShapeDtypeStruct((N_META, KV_DIM), BF16),
                   jax.ShapeDtypeStruct((KV_DIM, 2 * N_META), BF16)),
        in_specs=[vmem] * 9,
        out_specs=(vmem, vmem, vmem),
        compiler_params=pltpu.CompilerParams(vmem_limit_bytes=VMEM_LIMIT),
        name="attn_meta",
    )(h, g, wqvT, bqvT, wk, bk, tbl, wo, bo)


def _conv_kernel(h_ref, g_ref, win_ref, cw_ref, wout_ref, zinit_ref, out_ref, ztail_ref, zprev_scr, *, tm):
    @pl.when(pl.program_id(1) == 0)
    def _():
        zprev_scr[...] = zinit_ref[...]

    x = h_ref[...]
    a = _rms(x, g_ref[...]).astype(BF16)
    bcu = _dot(a, win_ref[...])
    gate_b = bcu[:, :D_MODEL]
    z = bcu[:, D_MODEL:2 * D_MODEL] * bcu[:, 2 * D_MODEL:]
    zp = zprev_scr[...]
    row = lax.broadcasted_iota(jnp.int32, (tm, D_MODEL), 0)
    last, last2 = zp[CONV_TAIL - 1:CONV_TAIL], zp[CONV_TAIL - 2:CONV_TAIL - 1]
    z1 = jnp.where(row == 0, last, pltpu.roll(z, 1, 0))
    z2 = jnp.where(row == 0, last2, jnp.where(row == 1, last, pltpu.roll(z, 2, 0)))
    cw = cw_ref[...]
    conv = cw[0:1] * z2 + cw[1:2] * z1 + cw[2:3] * z
    out_ref[...] = x + _dot((gate_b * conv).astype(BF16), wout_ref[...])
    tail = z[tm - CONV_TAIL:]
    zprev_scr[...] = tail
    ztail_ref[...] = tail


def _conv_call(h, g, win, cw, wout, zinit, batch, tm):
    n = h.shape[0]
    nt = n // batch // tm
    row = pl.BlockSpec((tm, D_MODEL), lambda b, i: (b * nt + i, 0))
    tail_shape = (CONV_TAIL, D_MODEL)
    return pl.pallas_call(
        functools.partial(_conv_kernel, tm=tm),
        grid=(batch, nt),
        out_shape=(jax.ShapeDtypeStruct((n, D_MODEL), F32), jax.ShapeDtypeStruct(tail_shape, F32)),
        in_specs=[row, _const_spec((1, D_MODEL)), _const_spec(win.shape), _const_spec(cw.shape),
                  _const_spec(wout.shape), _const_spec(tail_shape)],
        out_specs=(row, pl.BlockSpec(tail_shape, lambda b, i: (0, 0))),
        scratch_shapes=[pltpu.VMEM(tail_shape, F32)],
        compiler_params=pltpu.CompilerParams(dimension_semantics=("arbitrary", "arbitrary"),
                                             vmem_limit_bytes=VMEM_LIMIT),
        name="conv_mixer",
    )(h, g, win, cw, wout, zinit)


def _pool_kernel(h_ref, g_ref, wp_ref, sc_ref, hinit_ref, out_ref, atail_ref, hist_scr, *, tm, from_start):
    @pl.when(pl.program_id(1) == 0)
    def _():
        hist_scr[...] = hinit_ref[...]

    x = h_ref[...]
    a = _rms(x, g_ref[...])
    ext = jnp.concatenate([hist_scr[...], a], axis=0)
    outs = []
    for gi, win in enumerate(POOL_WINDOWS):
        cols = slice(gi * POOL_GROUP_DIM, (gi + 1) * POOL_GROUP_DIM)
        s = ext[:, cols]
        k = 1
        while k < win:
            s = s + pltpu.roll(s, k, 0)
            k *= 2
        s = s[POOL_HIST:]
        if from_start:
            pos = lax.broadcasted_iota(jnp.int32, (tm, POOL_GROUP_DIM), 0)
            mean = s / jnp.minimum(win, pos + 1).astype(F32)
        else:
            mean = s * (1.0 / win)
        mix = (mean - a[:, cols]).astype(BF16)
        outs.append(_dot(mix, wp_ref[gi]))
    out_ref[...] = x + jnp.concatenate(outs, axis=-1) * sc_ref[...]
    tail = a[tm - POOL_HIST:]
    hist_scr[...] = tail
    atail_ref[...] = tail


def _pool_call(h, g, wp, sc, hinit, batch, tm, from_start):
    n = h.shape[0]
    nt = n // batch // tm
    row = pl.BlockSpec((tm, D_MODEL), lambda b, i: (b * nt + i, 0))
    hist_shape = (POOL_HIST, D_MODEL)
    return pl.pallas_call(
        functools.partial(_pool_kernel, tm=tm, from_start=from_start),
        grid=(batch, nt),
        out_shape=(jax.ShapeDtypeStruct((n, D_MODEL), F32), jax.ShapeDtypeStruct(hist_shape, F32)),
        in_specs=[row, _const_spec((1, D_MODEL)), _const_spec(wp.shape), _const_spec((1, D_MODEL)),
                  _const_spec(hist_shape)],
        out_specs=(row, pl.BlockSpec(hist_shape, lambda b, i: (0, 0))),
        scratch_shapes=[pltpu.VMEM(hist_shape, F32)],
        compiler_params=pltpu.CompilerParams(dimension_semantics=("arbitrary", "arbitrary"),
                                             vmem_limit_bytes=VMEM_LIMIT),
        name="pool_mixer",
    )(h, g, wp, sc, hinit)


def _ffn_kernel(h_ref, g_ref, wg_ref, wu_ref, wd_ref, gf_ref, out_ref, *, final_norm):
    x = h_ref[...]
    a = _rms(x, g_ref[...]).astype(BF16)
    gate = _dot(a, wg_ref[...])
    up = _dot(a, wu_ref[...])
    hmid = (gate * (1.0 / (1.0 + jnp.exp(-gate))) * up).astype(BF16)
    y = x + _dot(hmid, wd_ref[...])
    out_ref[...] = _rms(y, gf_ref[...]) if final_norm else y


def _ffn_call(h, g, wg, wu, wd, gf, tm, final_norm):
    n = h.shape[0]
    row = pl.BlockSpec((tm, D_MODEL), lambda i: (i, 0))
    return pl.pallas_call(
        functools.partial(_ffn_kernel, final_norm=final_norm),
        grid=(n // tm,),
        out_shape=jax.ShapeDtypeStruct((n, D_MODEL), F32),
        in_specs=[row, _const_spec((1, D_MODEL)), _const_spec(wg.shape), _const_spec(wu.shape),
                  _const_spec(wd.shape), _const_spec((1, D_MODEL))],
        out_specs=row,
        compiler_params=pltpu.CompilerParams(dimension_semantics=("parallel",), vmem_limit_bytes=VMEM_LIMIT),
        name="ffn",
    )(h, g, wg, wu, wd, gf)


ATTN_TM = 256
ROW_TM = 512


def kernel(x, meta_tokens, rel_bias_table, norm_mix, norm_ffn, norm_final, attn_w_qkv, attn_b_qkv, attn_w_o,
           attn_b_o, attn_sinks, conv_w_in, conv_w, conv_w_out, pool_w, pool_scale, ffn_w_gate, ffn_w_up,
           ffn_w_down):
    batch, seq, _ = x.shape
    depth = norm_mix.shape[0]
    n_mixers = 3
    hr = x.reshape(batch * seq, D_MODEL)
    hm = meta_tokens.astype(F32)
    row = lambda v: v.reshape(1, -1)
    meta_codes = _meta_codes()
    for i in range(depth):
        kind, j = i % n_mixers, i // n_mixers
        last = i == depth - 1
        gm = row(norm_mix[i])
        if kind == 0:
            w, b = attn_w_qkv[j], attn_b_qkv[j]
            wqvT = jnp.concatenate([w[:, :Q_DIM], w[:, Q_DIM + KV_DIM:]], axis=1).T.astype(BF16)
            bqvT = jnp.broadcast_to(jnp.concatenate([b[:Q_DIM], b[Q_DIM + KV_DIM:]])[:, None], (Q_DIM + KV_DIM, LANES))
            wk, bk = w[:, Q_DIM:Q_DIM + KV_DIM].astype(BF16), row(b[Q_DIM:Q_DIM + KV_DIM])
            wo, bo = attn_w_o[j].astype(BF16), row(attn_b_o[j])
            tbl_meta = _meta_table(meta_codes, rel_bias_table, attn_sinks[j])
            tbl_real = _real_table(rel_bias_table, attn_sinks[j])
            hm, km, vmT = _attn_meta_call(hm, gm, wqvT, bqvT, wk, row(b), tbl_meta, wo, bo)
            qT, k, vT = _qkv_call(hr, gm, wqvT, bqvT, wk, bk, ROW_TM)
            hr = _attn_core_call(qT, k, vT, km, vmT, tbl_real, hr, wo, bo, batch, ATTN_TM)
        elif kind == 1:
            win, wout = conv_w_in[j].astype(BF16), conv_w_out[j].astype(BF16)
            hm, ztail = _conv_call(hm, gm, win, conv_w[j], wout, jnp.zeros((CONV_TAIL, D_MODEL), F32), 1, N_META)
            hr, _ = _conv_call(hr, gm, win, conv_w[j], wout, ztail, batch, ROW_TM)
        else:
            wp, sc = pool_w[j].astype(BF16), row(pool_scale[j])
            hm, atail = _pool_call(hm, gm, wp, sc, jnp.zeros((POOL_HIST, D_MODEL), F32), 1, N_META, True)
            hr, _ = _pool_call(hr, gm, wp, sc, atail, batch, ROW_TM, False)
        gf = row(norm_ffn[i])
        wg, wu, wd = ffn_w_gate[i].astype(BF16), ffn_w_up[i].astype(BF16), ffn_w_down[i].astype(BF16)
        if not last:
            hm = _ffn_call(hm, gf, wg, wu, wd, row(norm_final), N_META, False)
        hr = _ffn_call(hr, gf, wg, wu, wd, row(norm_final), ROW_TM, last)
    return hr.reshape(batch, seq, D_MODEL)
```

```python
import functools
import math

import numpy as np
import jax
import jax.numpy as jnp
from jax import lax
from jax.experimental import pallas as pl
from jax.experimental.pallas import tpu as pltpu

F32 = jnp.float32
BF16 = jnp.bfloat16

D_MODEL = 1024
N_META = 16
RMS_EPS = 1e-6
N_HEADS = 16
N_KV = 4
HEAD_DIM = 64
GROUP = N_HEADS // N_KV
Q_DIM = N_HEADS * HEAD_DIM
KV_DIM = N_KV * HEAD_DIM
WINDOW = 128
N_BUCKETS = 32
MAX_DISTANCE = 128
POOL_WINDOWS = (2, 4, 8, 16)
POOL_GROUP_DIM = D_MODEL // len(POOL_WINDOWS)
POOL_HIST = 16
CONV_TAIL = 8
LANES = 128
LOG2E = math.log2(math.e)

KEY_ROWS = 2 * WINDOW + 2 * N_META
SINK_ROW = 2 * WINDOW + N_META
ONES_ROWS = 16

META_KEY_COLS = 128
CODE_SINK = N_BUCKETS
CODE_MASKED = N_BUCKETS + 1

VMEM_LIMIT = 56 * 1024 * 1024


def _rel_bucket_np(dist):
    max_exact = N_BUCKETS // 2
    d = np.maximum(dist, 0)
    df = np.maximum(d, 1).astype(np.float64)
    large = max_exact + (np.log(df / max_exact) / math.log(MAX_DISTANCE / max_exact)
                         * (N_BUCKETS - max_exact)).astype(np.int64)
    large = np.minimum(large, N_BUCKETS - 1)
    return np.where(d < max_exact, d, large).astype(np.int32)


def _meta_codes():
    i = np.arange(N_META)[:, None]
    m = np.arange(N_META)[None, :]
    code = np.full((N_META, META_KEY_COLS), CODE_MASKED, np.int32)
    code[:, :N_META] = np.where(i >= m, _rel_bucket_np(i - m), CODE_MASKED)
    code[:, N_META] = CODE_SINK
    return code[None]


def _band_codes():
    return np.stack([_rel_bucket_np(np.arange(WINDOW)), np.zeros(WINDOW, np.int32)])


def _first_meta_codes():
    m = np.arange(N_META)[:, None]
    i = np.arange(WINDOW)[None, :]
    return _rel_bucket_np(N_META + i - m)


def _const_spec(shape):
    zeros = (0,) * len(shape)
    return pl.BlockSpec(shape, lambda *_: zeros, pipeline_mode=pl.Buffered(1))


def _layer_spec(stacked_shape, layer):
    index = (layer,) + (0,) * (len(stacked_shape) - 1)
    return pl.BlockSpec((None,) + tuple(stacked_shape[1:]), lambda *_: index, pipeline_mode=pl.Buffered(1))


def _smem_spec():
    return pl.BlockSpec(memory_space=pltpu.SMEM)


def _rms(x, g):
    return x * lax.rsqrt(jnp.mean(x * x, axis=-1, keepdims=True) + RMS_EPS) * g


def _dot(a, b):
    return jnp.dot(a, b, preferred_element_type=F32)


def _dot_nt(a, b):
    return lax.dot_general(a, b, (((1,), (1,)), ((), ())), preferred_element_type=F32)


def _dot_tn(a, b):
    return lax.dot_general(a, b, (((0,), (0,)), ((), ())), preferred_element_type=F32)


def _select_by_code(code, tbl_ref, h, otherwise):
    val = otherwise
    for b in range(N_BUCKETS):
        val = jnp.where(code == b, tbl_ref[b, h], val)
    return val


def _meta_table_kernel(code_ref, tbl_ref, sink_ref, out_ref):
    n_var, rows, _ = code_ref.shape
    for v in range(n_var):
        code = code_ref[v]
        for h in range(N_HEADS):
            kh, g = divmod(h, GROUP)
            val = _select_by_code(code, tbl_ref, h, jnp.where(code == CODE_SINK, sink_ref[h], -jnp.inf))
            out_ref[v, kh, g * rows:(g + 1) * rows, :] = val


def _meta_table(codes, rel_table, sinks):
    n_var, rows, cols = codes.shape
    return pl.pallas_call(
        _meta_table_kernel,
        out_shape=jax.ShapeDtypeStruct((n_var, N_KV, GROUP * rows, cols), F32),
        in_specs=[pl.BlockSpec(memory_space=pltpu.VMEM), _smem_spec(), _smem_spec()],
        out_specs=pl.BlockSpec(memory_space=pltpu.VMEM),
        name="meta_bias_table",
    )(jnp.asarray(codes), rel_table, sinks)


def _real_table_kernel(band_code_ref, meta_code_ref, tbl_ref, sink_ref, out_ref):
    neg = -jnp.inf
    band_code = band_code_ref[0:1, :]
    meta_code = meta_code_ref[...]
    t = lax.broadcasted_iota(jnp.int32, (WINDOW, WINDOW), 0)
    i = lax.broadcasted_iota(jnp.int32, (WINDOW, WINDOW), 1)
    row16 = lax.broadcasted_iota(jnp.int32, (N_META, WINDOW), 0)
    for h in range(N_HEADS):
        kh, g = divmod(h, GROUP)
        lanes = slice(g * WINDOW, (g + 1) * WINDOW)
        by_dist = _select_by_code(band_code, tbl_ref, h, jnp.zeros((1, WINDOW), F32)) * LOG2E
        rolled = pltpu.roll(jnp.broadcast_to(by_dist, (WINDOW, WINDOW)), 0, 1, stride=1, stride_axis=0)
        prev_blk = jnp.where(i < t, rolled, neg)
        cur_blk = jnp.where(i >= t, rolled, neg)
        far = jnp.full((N_META, WINDOW), tbl_ref[N_BUCKETS - 1, h], F32)
        near = _select_by_code(meta_code, tbl_ref, h, far) * LOG2E
        far = far * LOG2E
        sink_pad = jnp.where(row16 == 0, sink_ref[h] * LOG2E, neg)
        for v in range(2):
            out_ref[v, kh, 0:WINDOW, lanes] = jnp.full((WINDOW, WINDOW), neg, F32) if v else prev_blk
            out_ref[v, kh, WINDOW:2 * WINDOW, lanes] = cur_blk
            out_ref[v, kh, 2 * WINDOW:SINK_ROW, lanes] = near if v else far
            out_ref[v, kh, SINK_ROW:KEY_ROWS, lanes] = sink_pad


def _real_table(rel_table, sinks):
    vmem = pl.BlockSpec(memory_space=pltpu.VMEM)
    return pl.pallas_call(
        _real_table_kernel,
        out_shape=jax.ShapeDtypeStruct((2, N_KV, KEY_ROWS, GROUP * WINDOW), F32),
        in_specs=[vmem, vmem, _smem_spec(), _smem_spec()],
        out_specs=vmem,
        name="real_bias_table",
    )(jnp.asarray(_band_codes()), jnp.asarray(_first_meta_codes()), rel_table, sinks)


def _qkv_kernel(h_ref, g_ref, wqv_ref, bqv_ref, wk_ref, bk_ref, qT_ref, k_ref, vT_ref, *, tm):
    a = _rms(h_ref[...], g_ref[...]).astype(BF16)
    bias = jnp.concatenate([bqv_ref[...]] * (tm // LANES), axis=1)
    qvT = _dot_nt(wqv_ref[...], a) + bias
    qT_ref[...] = (qvT[:Q_DIM] * (HEAD_DIM ** -0.5 * LOG2E)).astype(BF16)
    vT_ref[...] = qvT[Q_DIM:].astype(BF16)
    k_ref[...] = (_dot(a, wk_ref[...]) + bk_ref[...]).astype(BF16)


def _qkv_call(h, g, wqvT, bqvT, wk, bk, tm):
    n = h.shape[0]
    row = lambda c: pl.BlockSpec((tm, c), lambda i: (i, 0))
    col = lambda r: pl.BlockSpec((r, tm), lambda i: (0, i))
    return pl.pallas_call(
        functools.partial(_qkv_kernel, tm=tm),
        grid=(n // tm,),
        out_shape=(jax.ShapeDtypeStruct((Q_DIM, n), BF16),
                   jax.ShapeDtypeStruct((n, KV_DIM), BF16),
                   jax.ShapeDtypeStruct((KV_DIM, n), BF16)),
        in_specs=[row(D_MODEL), _const_spec((1, D_MODEL)), _const_spec(wqvT.shape), _const_spec(bqvT.shape),
                  _const_spec(wk.shape), _const_spec((1, KV_DIM))],
        out_specs=(col(Q_DIM), row(KV_DIM), col(KV_DIM)),
        compiler_params=pltpu.CompilerParams(dimension_semantics=("parallel",), vmem_limit_bytes=VMEM_LIMIT),
        name="attn_qkv",
    )(h, g, wqvT, bqvT, wk, bk)


def _attn_core_kernel(qT_ref, kp_ref, kc_ref, vTp_ref, vTc_ref, km_ref, vmT_ref, tbl_ref, h_ref, wo_ref, bo_ref,
                      out_ref, oT_scr, *, tm):
    first_tile = pl.program_id(1) == 0
    kext = jnp.concatenate([kp_ref[...], kc_ref[...]], axis=0)
    vText = jnp.concatenate([vTp_ref[...], vTc_ref[...]], axis=1)
    km = km_ref[...]
    vmT = vmT_ref[...]
    zk = jnp.zeros((KEY_ROWS - SINK_ROW, HEAD_DIM), BF16)
    ones = jnp.ones((ONES_ROWS, KEY_ROWS), BF16)

    def scores(j, kh):
        r0 = j * WINDOW
        cols = slice(kh * HEAD_DIM, (kh + 1) * HEAD_DIM)
        qsT = jnp.concatenate(
            [qT_ref[(kh * GROUP + g) * HEAD_DIM:(kh * GROUP + g + 1) * HEAD_DIM, r0:r0 + WINDOW]
             for g in range(GROUP)], axis=1)
        keys = jnp.concatenate([kext[r0:r0 + 2 * WINDOW, cols], km[:, cols], zk], axis=0)
        return _dot(keys, qsT)

    def finish(j, kh, sT, sT_next):
        r0 = j * WINDOW
        cols = slice(kh * HEAD_DIM, (kh + 1) * HEAD_DIM)
        var = jnp.where(first_tile, 1, 0) if j == 0 else 0
        sT = sT + tbl_ref[var, kh]
        m = jnp.max(sT, axis=0, keepdims=True)
        pT = jnp.exp2(sT - m).astype(BF16)
        valsT = jnp.concatenate([vText[cols, r0:r0 + 2 * WINDOW], vmT[cols, :]], axis=1)
        den = ones
        if sT_next is not None:
            bits = pltpu.bitcast(sT_next[KEY_ROWS - ONES_ROWS:, :2 * WINDOW], jnp.uint32)
            zero = pltpu.bitcast((bits >> 16) >> 16, F32).astype(BF16)
            den = jnp.concatenate([ones[:, :2 * WINDOW] + zero, ones[:, 2 * WINDOW:]], axis=1)
        oT = _dot(jnp.concatenate([valsT, den], axis=0), pT)
        oT = oT[:HEAD_DIM] / oT[HEAD_DIM:HEAD_DIM + 1]
        for g in range(GROUP):
            h0 = (kh * GROUP + g) * HEAD_DIM
            oT_scr[h0:h0 + HEAD_DIM, r0:r0 + WINDOW] = oT[:, g * WINDOW:(g + 1) * WINDOW].astype(BF16)

    units = [(j, kh) for j in range(tm // WINDOW) for kh in range(N_KV)]
    sT = scores(*units[0])
    for idx, unit in enumerate(units):
        sT_next = scores(*units[idx + 1]) if idx + 1 < len(units) else None
        finish(*unit, sT, sT_next)
        sT = sT_next
    out_ref[...] = h_ref[...] + _dot_tn(oT_scr[...], wo_ref[...]) + bo_ref[...]


def _attn_core_call(qT, k, vT, km, vmT, tbl, h, wo, bo, batch, tm):
    n = h.shape[0]
    nt = n // batch // tm
    per = tm // WINDOW
    prev_idx = lambda b, i: jnp.maximum((b * nt + i) * per - 1, 0)
    row = lambda c: pl.BlockSpec((tm, c), lambda b, i: (b * nt + i, 0))
    col = lambda r: pl.BlockSpec((r, tm), lambda b, i: (0, b * nt + i))
    return pl.pallas_call(
        functools.partial(_attn_core_kernel, tm=tm),
        grid=(batch, nt),
        out_shape=jax.ShapeDtypeStruct((n, D_MODEL), F32),
        in_specs=[col(Q_DIM),
                  pl.BlockSpec((WINDOW, KV_DIM), lambda b, i: (prev_idx(b, i), 0)), row(KV_DIM),
                  pl.BlockSpec((KV_DIM, WINDOW), lambda b, i: (0, prev_idx(b, i))), col(KV_DIM),
                  _const_spec(km.shape), _const_spec(vmT.shape), _const_spec(tbl.shape),
                  row(D_MODEL), _const_spec(wo.shape), _const_spec((1, D_MODEL))],
        out_specs=row(D_MODEL),
        scratch_shapes=[pltpu.VMEM((Q_DIM, tm), BF16)],
        compiler_params=pltpu.CompilerParams(dimension_semantics=("parallel", "parallel"),
                                             vmem_limit_bytes=VMEM_LIMIT),
        name="attn_core",
    )(qT, k, k, vT, vT, km, vmT, tbl, h, wo, bo)


def _attn_meta_kernel(h_ref, g_ref, wqv_ref, bqv_ref, wk_ref, b_ref, tbl_ref, wo_ref, bo_ref,
                      out_ref, km_ref, vmT_ref):
    x = h_ref[...]
    a = _rms(x, g_ref[...]).astype(BF16)
    a2 = jnp.concatenate([a, jnp.zeros_like(a)], axis=0)
    lane = lax.broadcasted_iota(jnp.int32, (KV_DIM, 2 * N_META), 1)
    vT = _dot_nt(wqv_ref[Q_DIM:, :], a2) + jnp.where(lane < N_META, bqv_ref[Q_DIM:, :2 * N_META], 0.0)
    vmT_ref[...] = vT.astype(BF16)
    b = b_ref[...]
    q = ((_dot_nt(a, wqv_ref[:Q_DIM, :]) + b[:, :Q_DIM]) * (HEAD_DIM ** -0.5)).astype(BF16)
    k = (_dot(a, wk_ref[...]) + b[:, Q_DIM:Q_DIM + KV_DIM]).astype(BF16)
    v = (_dot_nt(a, wqv_ref[Q_DIM:, :]) + b[:, Q_DIM + KV_DIM:]).astype(BF16)
    km_ref[...] = k
    pad = META_KEY_COLS - N_META
    vcat = jnp.concatenate([v, jnp.zeros((pad, KV_DIM), BF16)], axis=0)
    pieces = []
    for kh in range(N_KV):
        cols = slice(kh * HEAD_DIM, (kh + 1) * HEAD_DIM)
        qs = jnp.concatenate([q[:, (kh * GROUP + g) * HEAD_DIM:(kh * GROUP + g + 1) * HEAD_DIM]
                              for g in range(GROUP)], axis=0)
        kcat = jnp.concatenate([k[:, cols], jnp.zeros((pad, HEAD_DIM), BF16)], axis=0)
        s = _dot_nt(qs, kcat) + tbl_ref[0, kh]
        p = jnp.exp(s - jnp.max(s, axis=-1, keepdims=True))
        l = jnp.sum(p, axis=-1, keepdims=True)
        o = _dot(p.astype(BF16), vcat)[:, cols] / l
        pieces += [o[g * N_META:(g + 1) * N_META] for g in range(GROUP)]
    o_all = jnp.concatenate(pieces, axis=-1).astype(BF16)
    out_ref[...] = x + _dot(o_all, wo_ref[...]) + bo_ref[...]


def _attn_meta_call(h, g, wqvT, bqvT, wk, bk, tbl, wo, bo):
    vmem = pl.BlockSpec(memory_space=pltpu.VMEM)
    return pl.pallas_call(
        _attn_meta_kernel,
        out_shape=(jax.ShapeDtypeStruct((N_META, D_MODEL), F32),
                   jax.ShapeDtypeStruct((N_META, KV_DIM), BF16),
                   jax.ShapeDtypeStruct((KV_DIM, 2 * N_META), BF16)),
        in_specs=[vmem] * 9,
        out_specs=(vmem, vmem, vmem),
        compiler_params=pltpu.CompilerParams(vmem_limit_bytes=VMEM_LIMIT),
        name="attn_meta",
    )(h, g, wqvT, bqvT, wk, bk, tbl, wo, bo)


def _conv_kernel(h_ref, g_ref, win_ref, cw_ref, wout_ref, zinit_ref, out_ref, ztail_ref, zprev_scr, *, tm):
    @pl.when(pl.program_id(1) == 0)
    def _():
        zprev_scr[...] = zinit_ref[...]

    x = h_ref[...]
    a = _rms(x, g_ref[...]).astype(BF16)
    bcu = _dot(a, win_ref[...])
    gate_b = bcu[:, :D_MODEL]
    z = bcu[:, D_MODEL:2 * D_MODEL] * bcu[:, 2 * D_MODEL:]
    zp = zprev_scr[...]
    row = lax.broadcasted_iota(jnp.int32, (tm, D_MODEL), 0)
    last, last2 = zp[CONV_TAIL - 1:CONV_TAIL], zp[CONV_TAIL - 2:CONV_TAIL - 1]
    z1 = jnp.where(row == 0, last, pltpu.roll(z, 1, 0))
    z2 = jnp.where(row == 0, last2, jnp.where(row == 1, last, pltpu.roll(z, 2, 0)))
    cw = cw_ref[...]
    conv = cw[0:1] * z2 + cw[1:2] * z1 + cw[2:3] * z
    out_ref[...] = x + _dot((gate_b * conv).astype(BF16), wout_ref[...])
    tail = z[tm - CONV_TAIL:]
    zprev_scr[...] = tail
    ztail_ref[...] = tail


def _conv_call(h, g, win, cw, wout, zinit, batch, tm):
    n = h.shape[0]
    nt = n // batch // tm
    row = pl.BlockSpec((tm, D_MODEL), lambda b, i: (b * nt + i, 0))
    tail_shape = (CONV_TAIL, D_MODEL)
    return pl.pallas_call(
        functools.partial(_conv_kernel, tm=tm),
        grid=(batch, nt),
        out_shape=(jax.ShapeDtypeStruct((n, D_MODEL), F32), jax.ShapeDtypeStruct(tail_shape, F32)),
        in_specs=[row, _const_spec((1, D_MODEL)), _const_spec(win.shape), _const_spec(cw.shape),
                  _const_spec(wout.shape), _const_spec(tail_shape)],
        out_specs=(row, pl.BlockSpec(tail_shape, lambda b, i: (0, 0))),
        scratch_shapes=[pltpu.VMEM(tail_shape, F32)],
        compiler_params=pltpu.CompilerParams(dimension_semantics=("arbitrary", "arbitrary"),
                                             vmem_limit_bytes=VMEM_LIMIT),
        name="conv_mixer",
    )(h, g, win, cw, wout, zinit)


def _pool_kernel(h_ref, g_ref, wp_ref, sc_ref, hinit_ref, out_ref, atail_ref, hist_scr, *, tm, from_start):
    @pl.when(pl.program_id(1) == 0)
    def _():
        hist_scr[...] = hinit_ref[...]

    x = h_ref[...]
    a = _rms(x, g_ref[...])
    ext = jnp.concatenate([hist_scr[...], a], axis=0)
    outs = []
    for gi, win in enumerate(POOL_WINDOWS):
        cols = slice(gi * POOL_GROUP_DIM, (gi + 1) * POOL_GROUP_DIM)
        s = ext[:, cols]
        k = 1
        while k < win:
            s = s + pltpu.roll(s, k, 0)
            k *= 2
        s = s[POOL_HIST:]
        if from_start:
            pos = lax.broadcasted_iota(jnp.int32, (tm, POOL_GROUP_DIM), 0)
            mean = s / jnp.minimum(win, pos + 1).astype(F32)
        else:
            mean = s * (1.0 / win)
        mix = (mean - a[:, cols]).astype(BF16)
        outs.append(_dot(mix, wp_ref[gi]))
    out_ref[...] = x + jnp.concatenate(outs, axis=-1) * sc_ref[...]
    tail = a[tm - POOL_HIST:]
    hist_scr[...] = tail
    atail_ref[...] = tail


def _pool_call(h, g, wp, sc, hinit, batch, tm, from_start):
    n = h.shape[0]
    nt = n // batch // tm
    row = pl.BlockSpec((tm, D_MODEL), lambda b, i: (b * nt + i, 0))
    hist_shape = (POOL_HIST, D_MODEL)
    return pl.pallas_call(
        functools.partial(_pool_kernel, tm=tm, from_start=from_start),
        grid=(batch, nt),
        out_shape=(jax.ShapeDtypeStruct((n, D_MODEL), F32), jax.ShapeDtypeStruct(hist_shape, F32)),
        in_specs=[row, _const_spec((1, D_MODEL)), _const_spec(wp.shape), _const_spec((1, D_MODEL)),
                  _const_spec(hist_shape)],
        out_specs=(row, pl.BlockSpec(hist_shape, lambda b, i: (0, 0))),
        scratch_shapes=[pltpu.VMEM(hist_shape, F32)],
        compiler_params=pltpu.CompilerParams(dimension_semantics=("arbitrary", "arbitrary"),
                                             vmem_limit_bytes=VMEM_LIMIT),
        name="pool_mixer",
    )(h, g, wp, sc, hinit)


def _ffn_kernel(h_ref, g_ref, wg_ref, wu_ref, wd_ref, gf_ref, out_ref, *, final_norm):
    x = h_ref[...]
    a = _rms(x, g_ref[...]).astype(BF16)
    gate = _dot(a, wg_ref[...])
    up = _dot(a, wu_ref[...])
    hmid = (gate * (1.0 / (1.0 + jnp.exp(-gate))) * up).astype(BF16)
    y = x + _dot(hmid, wd_ref[...])
    out_ref[...] = _rms(y, gf_ref[...]) if final_norm else y


def _ffn_call(h, g, wg, wu, wd, layer, gf, tm, final_norm):
    n = h.shape[0]
    row = pl.BlockSpec((tm, D_MODEL), lambda i: (i, 0))
    return pl.pallas_call(
        functools.partial(_ffn_kernel, final_norm=final_norm),
        grid=(n // tm,),
        out_shape=jax.ShapeDtypeStruct((n, D_MODEL), F32),
        in_specs=[row, _const_spec((1, D_MODEL)), _layer_spec(wg.shape, layer), _layer_spec(wu.shape, layer),
                  _layer_spec(wd.shape, layer), _const_spec((1, D_MODEL))],
        out_specs=row,
        compiler_params=pltpu.CompilerParams(dimension_semantics=("parallel",), vmem_limit_bytes=VMEM_LIMIT),
        name="ffn",
    )(h, g, wg, wu, wd, gf)


ATTN_TM = 512
ROW_TM = 512


def kernel(x, meta_tokens, rel_bias_table, norm_mix, norm_ffn, norm_final, attn_w_qkv, attn_b_qkv, attn_w_o,
           attn_b_o, attn_sinks, conv_w_in, conv_w, conv_w_out, pool_w, pool_scale, ffn_w_gate, ffn_w_up,
           ffn_w_down):
    batch, seq, _ = x.shape
    depth = norm_mix.shape[0]
    n_mixers = 3
    hr = x.reshape(batch * seq, D_MODEL)
    hm = meta_tokens.astype(F32)
    row = lambda v: v.reshape(1, -1)
    meta_codes = _meta_codes()
    wg, wu, wd = ffn_w_gate.astype(BF16), ffn_w_up.astype(BF16), ffn_w_down.astype(BF16)
    for i in range(depth):
        kind, j = i % n_mixers, i // n_mixers
        last = i == depth - 1
        gm = row(norm_mix[i])
        if kind == 0:
            w, b = attn_w_qkv[j], attn_b_qkv[j]
            wqvT = jnp.concatenate([w[:, :Q_DIM], w[:, Q_DIM + KV_DIM:]], axis=1).T.astype(BF16)
            bqvT = jnp.broadcast_to(jnp.concatenate([b[:Q_DIM], b[Q_DIM + KV_DIM:]])[:, None], (Q_DIM + KV_DIM, LANES))
            wk, bk = w[:, Q_DIM:Q_DIM + KV_DIM].astype(BF16), row(b[Q_DIM:Q_DIM + KV_DIM])
            wo, bo = attn_w_o[j].astype(BF16), row(attn_b_o[j])
            tbl_meta = _meta_table(meta_codes, rel_bias_table, attn_sinks[j])
            tbl_real = _real_table(rel_bias_table, attn_sinks[j])
            hm, km, vmT = _attn_meta_call(hm, gm, wqvT, bqvT, wk, row(b), tbl_meta, wo, bo)
            qT, k, vT = _qkv_call(hr, gm, wqvT, bqvT, wk, bk, ROW_TM)
            hr = _attn_core_call(qT, k, vT, km, vmT, tbl_real, hr, wo, bo, batch, ATTN_TM)
        elif kind == 1:
            win, wout = conv_w_in[j].astype(BF16), conv_w_out[j].astype(BF16)
            hm, ztail = _conv_call(hm, gm, win, conv_w[j], wout, jnp.zeros((CONV_TAIL, D_MODEL), F32), 1, N_META)
            hr, _ = _conv_call(hr, gm, win, conv_w[j], wout, ztail, batch, ROW_TM)
        else:
            wp, sc = pool_w[j].astype(BF16), row(pool_scale[j])
            hm, atail = _pool_call(hm, gm, wp, sc, jnp.zeros((POOL_HIST, D_MODEL), F32), 1, N_META, True)
            hr, _ = _pool_call(hr, gm, wp, sc, atail, batch, ROW_TM, False)
        gf = row(norm_ffn[i])
        if not last:
            hm = _ffn_call(hm, gf, wg, wu, wd, i, row(norm_final), N_META, False)
        hr = _ffn_call(hr, gf, wg, wu, wd, i, row(norm_final), ROW_TM, last)
    return hr.reshape(batch, seq, D_MODEL)
```

```python
import functools
import math

import numpy as np
import jax
import jax.numpy as jnp
from jax import lax
from jax.experimental import pallas as pl
from jax.experimental.pallas import tpu as pltpu

F32 = jnp.float32
BF16 = jnp.bfloat16

D_MODEL = 1024
N_META = 16
RMS_EPS = 1e-6
N_HEADS = 16
N_KV = 4
HEAD_DIM = 64
GROUP = N_HEADS // N_KV
Q_DIM = N_HEADS * HEAD_DIM
KV_DIM = N_KV * HEAD_DIM
WINDOW = 128
N_BUCKETS = 32
MAX_DISTANCE = 128
POOL_WINDOWS = (2, 4, 8, 16)
POOL_GROUP_DIM = D_MODEL // len(POOL_WINDOWS)
POOL_HIST = 16
CONV_TAIL = 8
LANES = 128
LOG2E = math.log2(math.e)

KEY_ROWS = 2 * WINDOW + 2 * N_META
SINK_ROW = 2 * WINDOW + N_META
ONES_ROWS = 16

META_KEY_COLS = 128
CODE_SINK = N_BUCKETS
CODE_MASKED = N_BUCKETS + 1

VMEM_LIMIT = 56 * 1024 * 1024


def _rel_bucket_np(dist):
    max_exact = N_BUCKETS // 2
    d = np.maximum(dist, 0)
    df = np.maximum(d, 1).astype(np.float64)
    large = max_exact + (np.log(df / max_exact) / math.log(MAX_DISTANCE / max_exact)
                         * (N_BUCKETS - max_exact)).astype(np.int64)
    large = np.minimum(large, N_BUCKETS - 1)
    return np.where(d < max_exact, d, large).astype(np.int32)


def _meta_codes():
    i = np.arange(N_META)[:, None]
    m = np.arange(N_META)[None, :]
    code = np.full((N_META, META_KEY_COLS), CODE_MASKED, np.int32)
    code[:, :N_META] = np.where(i >= m, _rel_bucket_np(i - m), CODE_MASKED)
    code[:, N_META] = CODE_SINK
    return code[None]


def _band_codes():
    return np.stack([_rel_bucket_np(np.arange(WINDOW)), np.zeros(WINDOW, np.int32)])


def _first_meta_codes():
    m = np.arange(N_META)[:, None]
    i = np.arange(WINDOW)[None, :]
    return _rel_bucket_np(N_META + i - m)


def _const_spec(shape):
    zeros = (0,) * len(shape)
    return pl.BlockSpec(shape, lambda *_: zeros, pipeline_mode=pl.Buffered(1))


def _layer_spec(stacked_shape, layer):
    index = (layer,) + (0,) * (len(stacked_shape) - 1)
    return pl.BlockSpec((None,) + tuple(stacked_shape[1:]), lambda *_: index, pipeline_mode=pl.Buffered(1))


def _smem_spec():
    return pl.BlockSpec(memory_space=pltpu.SMEM)


def _rms(x, g):
    return x * lax.rsqrt(jnp.mean(x * x, axis=-1, keepdims=True) + RMS_EPS) * g


def _dot(a, b):
    return jnp.dot(a, b, preferred_element_type=F32)


def _dot_nt(a, b):
    return lax.dot_general(a, b, (((1,), (1,)), ((), ())), preferred_element_type=F32)


def _dot_tn(a, b):
    return lax.dot_general(a, b, (((0,), (0,)), ((), ())), preferred_element_type=F32)


def _select_by_code(code, tbl_ref, h, otherwise):
    val = otherwise
    for b in range(N_BUCKETS):
        val = jnp.where(code == b, tbl_ref[b, h], val)
    return val


def _meta_table_kernel(code_ref, tbl_ref, sink_ref, out_ref):
    n_var, rows, _ = code_ref.shape
    for v in range(n_var):
        code = code_ref[v]
        for h in range(N_HEADS):
            kh, g = divmod(h, GROUP)
            val = _select_by_code(code, tbl_ref, h, jnp.where(code == CODE_SINK, sink_ref[h], -jnp.inf))
            out_ref[v, kh, g * rows:(g + 1) * rows, :] = val


def _meta_table(codes, rel_table, sinks):
    n_var, rows, cols = codes.shape
    return pl.pallas_call(
        _meta_table_kernel,
        out_shape=jax.ShapeDtypeStruct((n_var, N_KV, GROUP * rows, cols), F32),
        in_specs=[pl.BlockSpec(memory_space=pltpu.VMEM), _smem_spec(), _smem_spec()],
        out_specs=pl.BlockSpec(memory_space=pltpu.VMEM),
        name="meta_bias_table",
    )(jnp.asarray(codes), rel_table, sinks)


def _real_table_kernel(band_code_ref, meta_code_ref, tbl_ref, sink_ref, out_ref):
    neg = -jnp.inf
    band_code = band_code_ref[0:1, :]
    meta_code = meta_code_ref[...]
    t = lax.broadcasted_iota(jnp.int32, (WINDOW, WINDOW), 0)
    i = lax.broadcasted_iota(jnp.int32, (WINDOW, WINDOW), 1)
    row16 = lax.broadcasted_iota(jnp.int32, (N_META, WINDOW), 0)
    for h in range(N_HEADS):
        kh, g = divmod(h, GROUP)
        lanes = slice(g * WINDOW, (g + 1) * WINDOW)
        by_dist = _select_by_code(band_code, tbl_ref, h, jnp.zeros((1, WINDOW), F32)) * LOG2E
        rolled = pltpu.roll(jnp.broadcast_to(by_dist, (WINDOW, WINDOW)), 0, 1, stride=1, stride_axis=0)
        prev_blk = jnp.where(i < t, rolled, neg)
        cur_blk = jnp.where(i >= t, rolled, neg)
        far = jnp.full((N_META, WINDOW), tbl_ref[N_BUCKETS - 1, h], F32)
        near = _select_by_code(meta_code, tbl_ref, h, far) * LOG2E
        far = far * LOG2E
        sink_pad = jnp.where(row16 == 0, sink_ref[h] * LOG2E, neg)
        for v in range(2):
            out_ref[v, kh, 0:WINDOW, lanes] = jnp.full((WINDOW, WINDOW), neg, F32) if v else prev_blk
            out_ref[v, kh, WINDOW:2 * WINDOW, lanes] = cur_blk
            out_ref[v, kh, 2 * WINDOW:SINK_ROW, lanes] = near if v else far
            out_ref[v, kh, SINK_ROW:KEY_ROWS, lanes] = sink_pad


def _real_table(rel_table, sinks):
    vmem = pl.BlockSpec(memory_space=pltpu.VMEM)
    return pl.pallas_call(
        _real_table_kernel,
        out_shape=jax.ShapeDtypeStruct((2, N_KV, KEY_ROWS, GROUP * WINDOW), F32),
        in_specs=[vmem, vmem, _smem_spec(), _smem_spec()],
        out_specs=vmem,
        name="real_bias_table",
    )(jnp.asarray(_band_codes()), jnp.asarray(_first_meta_codes()), rel_table, sinks)


def _qkv_kernel(h_ref, g_ref, wqv_ref, bqv_ref, wk_ref, bk_ref, qT_ref, k_ref, vT_ref, *, tm):
    a = _rms(h_ref[...], g_ref[...]).astype(BF16)
    bias = jnp.concatenate([bqv_ref[...]] * (tm // LANES), axis=1)
    qvT = _dot_nt(wqv_ref[...], a) + bias
    qT_ref[...] = (qvT[:Q_DIM] * (HEAD_DIM ** -0.5 * LOG2E)).astype(BF16)
    vT_ref[...] = qvT[Q_DIM:].astype(BF16)
    k_ref[...] = (_dot(a, wk_ref[...]) + bk_ref[...]).astype(BF16)


def _qkv_call(h, g, wqvT, bqvT, wk, bk, tm):
    n = h.shape[0]
    row = lambda c: pl.BlockSpec((tm, c), lambda i: (i, 0))
    col = lambda r: pl.BlockSpec((r, tm), lambda i: (0, i))
    return pl.pallas_call(
        functools.partial(_qkv_kernel, tm=tm),
        grid=(n // tm,),
        out_shape=(jax.ShapeDtypeStruct((Q_DIM, n), BF16),
                   jax.ShapeDtypeStruct((n, KV_DIM), BF16),
                   jax.ShapeDtypeStruct((KV_DIM, n), BF16)),
        in_specs=[row(D_MODEL), _const_spec((1, D_MODEL)), _const_spec(wqvT.shape), _const_spec(bqvT.shape),
                  _const_spec(wk.shape), _const_spec((1, KV_DIM))],
        out_specs=(col(Q_DIM), row(KV_DIM), col(KV_DIM)),
        compiler_params=pltpu.CompilerParams(dimension_semantics=("parallel",), vmem_limit_bytes=VMEM_LIMIT),
        name="attn_qkv",
    )(h, g, wqvT, bqvT, wk, bk)


def _attn_core_kernel(qT_ref, kp_ref, kc_ref, vTp_ref, vTc_ref, km_ref, vmT_ref, tbl_ref, h_ref, wo_ref, bo_ref,
                      out_ref, oT_scr, *, tm):
    first_tile = pl.program_id(1) == 0
    kext = jnp.concatenate([kp_ref[...], kc_ref[...]], axis=0)
    vText = jnp.concatenate([vTp_ref[...], vTc_ref[...]], axis=1)
    km = km_ref[...]
    vmT = vmT_ref[...]
    zk = jnp.zeros((KEY_ROWS - SINK_ROW, HEAD_DIM), BF16)
    ones = jnp.ones((ONES_ROWS, KEY_ROWS), BF16)

    def scores(j, kh):
        r0 = j * WINDOW
        cols = slice(kh * HEAD_DIM, (kh + 1) * HEAD_DIM)
        qsT = jnp.concatenate(
            [qT_ref[(kh * GROUP + g) * HEAD_DIM:(kh * GROUP + g + 1) * HEAD_DIM, r0:r0 + WINDOW]
             for g in range(GROUP)], axis=1)
        keys = jnp.concatenate([kext[r0:r0 + 2 * WINDOW, cols], km[:, cols], zk], axis=0)
        return _dot(keys, qsT)

    def finish(j, kh, sT, sT_next):
        r0 = j * WINDOW
        cols = slice(kh * HEAD_DIM, (kh + 1) * HEAD_DIM)
        var = jnp.where(first_tile, 1, 0) if j == 0 else 0
        sT = sT + tbl_ref[var, kh]
        m = jnp.max(sT, axis=0, keepdims=True)
        pT = jnp.exp2(sT - m).astype(BF16)
        valsT = jnp.concatenate([vText[cols, r0:r0 + 2 * WINDOW], vmT[cols, :]], axis=1)
        den = ones
        if sT_next is not None:
            bits = pltpu.bitcast(sT_next[KEY_ROWS - ONES_ROWS:, :2 * WINDOW], jnp.uint32)
            zero = pltpu.bitcast((bits >> 16) >> 16, F32).astype(BF16)
            den = jnp.concatenate([ones[:, :2 * WINDOW] + zero, ones[:, 2 * WINDOW:]], axis=1)
        oT = _dot(jnp.concatenate([valsT, den], axis=0), pT)
        oT = oT[:HEAD_DIM] / oT[HEAD_DIM:HEAD_DIM + 1]
        for g in range(GROUP):
            h0 = (kh * GROUP + g) * HEAD_DIM
            oT_scr[h0:h0 + HEAD_DIM, r0:r0 + WINDOW] = oT[:, g * WINDOW:(g + 1) * WINDOW].astype(BF16)

    units = [(j, kh) for j in range(tm // WINDOW) for kh in range(N_KV)]
    sT = scores(*units[0])
    for idx, unit in enumerate(units):
        sT_next = scores(*units[idx + 1]) if idx + 1 < len(units) else None
        finish(*unit, sT, sT_next)
        sT = sT_next
    out_ref[...] = h_ref[...] + _dot_tn(oT_scr[...], wo_ref[...]) + bo_ref[...]


def _attn_core_call(qT, k, vT, km, vmT, tbl, h, wo, bo, batch, tm):
    n = h.shape[0]
    nt = n // batch // tm
    per = tm // WINDOW
    prev_idx = lambda b, i: jnp.maximum((b * nt + i) * per - 1, 0)
    row = lambda c: pl.BlockSpec((tm, c), lambda b, i: (b * nt + i, 0))
    col = lambda r: pl.BlockSpec((r, tm), lambda b, i: (0, b * nt + i))
    return pl.pallas_call(
        functools.partial(_attn_core_kernel, tm=tm),
        grid=(batch, nt),
        out_shape=jax.ShapeDtypeStruct((n, D_MODEL), F32),
        in_specs=[col(Q_DIM),
                  pl.BlockSpec((WINDOW, KV_DIM), lambda b, i: (prev_idx(b, i), 0)), row(KV_DIM),
                  pl.BlockSpec((KV_DIM, WINDOW), lambda b, i: (0, prev_idx(b, i))), col(KV_DIM),
                  _const_spec(km.shape), _const_spec(vmT.shape), _const_spec(tbl.shape),
                  row(D_MODEL), _const_spec(wo.shape), _const_spec((1, D_MODEL))],
        out_specs=row(D_MODEL),
        scratch_shapes=[pltpu.VMEM((Q_DIM, tm), BF16)],
        compiler_params=pltpu.CompilerParams(dimension_semantics=("parallel", "parallel"),
                                             vmem_limit_bytes=VMEM_LIMIT),
        name="attn_core",
    )(qT, k, k, vT, vT, km, vmT, tbl, h, wo, bo)


def _attn_meta_kernel(h_ref, g_ref, wqv_ref, bqv_ref, wk_ref, b_ref, tbl_ref, wo_ref, bo_ref,
                      out_ref, km_ref, vmT_ref):
    x = h_ref[...]
    a = _rms(x, g_ref[...]).astype(BF16)
    a2 = jnp.concatenate([a, jnp.zeros_like(a)], axis=0)
    lane = lax.broadcasted_iota(jnp.int32, (KV_DIM, 2 * N_META), 1)
    vT = _dot_nt(wqv_ref[Q_DIM:, :], a2) + jnp.where(lane < N_META, bqv_ref[Q_DIM:, :2 * N_META], 0.0)
    vmT_ref[...] = vT.astype(BF16)
    b = b_ref[...]
    q = ((_dot_nt(a, wqv_ref[:Q_DIM, :]) + b[:, :Q_DIM]) * (HEAD_DIM ** -0.5)).astype(BF16)
    k = (_dot(a, wk_ref[...]) + b[:, Q_DIM:Q_DIM + KV_DIM]).astype(BF16)
    v = (_dot_nt(a, wqv_ref[Q_DIM:, :]) + b[:, Q_DIM + KV_DIM:]).astype(BF16)
    km_ref[...] = k
    pad = META_KEY_COLS - N_META
    vcat = jnp.concatenate([v, jnp.zeros((pad, KV_DIM), BF16)], axis=0)
    pieces = []
    for kh in range(N_KV):
        cols = slice(kh * HEAD_DIM, (kh + 1) * HEAD_DIM)
        qs = jnp.concatenate([q[:, (kh * GROUP + g) * HEAD_DIM:(kh * GROUP + g + 1) * HEAD_DIM]
                              for g in range(GROUP)], axis=0)
        kcat = jnp.concatenate([k[:, cols], jnp.zeros((pad, HEAD_DIM), BF16)], axis=0)
        s = _dot_nt(qs, kcat) + tbl_ref[0, kh]
        p = jnp.exp(s - jnp.max(s, axis=-1, keepdims=True))
        l = jnp.sum(p, axis=-1, keepdims=True)
        o = _dot(p.astype(BF16), vcat)[:, cols] / l
        pieces += [o[g * N_META:(g + 1) * N_META] for g in range(GROUP)]
    o_all = jnp.concatenate(pieces, axis=-1).astype(BF16)
    out_ref[...] = x + _dot(o_all, wo_ref[...]) + bo_ref[...]


def _attn_meta_call(h, g, wqvT, bqvT, wk, bk, tbl, wo, bo):
    vmem = pl.BlockSpec(memory_space=pltpu.VMEM)
    return pl.pallas_call(
        _attn_meta_kernel,
        out_shape=(jax.ShapeDtypeStruct((N_META, D_MODEL), F32),
                   jax.ShapeDtypeStruct((N_META, KV_DIM), BF16),
                   jax.ShapeDtypeStruct((KV_DIM, 2 * N_META), BF16)),
        in_specs=[vmem] * 9,
        out_specs=(vmem, vmem, vmem),
        compiler_params=pltpu.CompilerParams(vmem_limit_bytes=VMEM_LIMIT),
        name="attn_meta",
    )(h, g, wqvT, bqvT, wk, bk, tbl, wo, bo)


def _ffn_math(x, g_ref, wg_ref, wu_ref, wd_ref):
    a = _rms(x, g_ref[...]).astype(BF16)
    gate = _dot(a, wg_ref[...])
    up = _dot(a, wu_ref[...])
    hmid = (gate * (1.0 / (1.0 + jnp.exp(-gate))) * up).astype(BF16)
    return x + _dot(hmid, wd_ref[...])


def _ffn_kernel(h_ref, g_ref, wg_ref, wu_ref, wd_ref, gf_ref, out_ref, *, final_norm):
    y = _ffn_math(h_ref[...], g_ref, wg_ref, wu_ref, wd_ref)
    out_ref[...] = _rms(y, gf_ref[...]) if final_norm else y


def _ffn_specs(wg, wu, wd, layer):
    return [_const_spec((1, D_MODEL)), _layer_spec(wg.shape, layer), _layer_spec(wu.shape, layer),
            _layer_spec(wd.shape, layer)]


def _ffn_call(h, g, wg, wu, wd, layer, gf, tm, final_norm):
    n = h.shape[0]
    row = pl.BlockSpec((tm, D_MODEL), lambda i: (i, 0))
    return pl.pallas_call(
        functools.partial(_ffn_kernel, final_norm=final_norm),
        grid=(n // tm,),
        out_shape=jax.ShapeDtypeStruct((n, D_MODEL), F32),
        in_specs=[row] + _ffn_specs(wg, wu, wd, layer) + [_const_spec((1, D_MODEL))],
        out_specs=row,
        compiler_params=pltpu.CompilerParams(dimension_semantics=("parallel",), vmem_limit_bytes=VMEM_LIMIT),
        name="ffn",
    )(h, g, wg, wu, wd, gf)


def _conv_ffn_kernel(h_ref, g_ref, win_ref, cw_ref, wout_ref, zinit_ref, gf_ref, wg_ref, wu_ref, wd_ref,
                     out_ref, ztail_ref, zprev_scr, *, tm):
    @pl.when(pl.program_id(1) == 0)
    def _():
        zprev_scr[...] = zinit_ref[...]

    x = h_ref[...]
    a = _rms(x, g_ref[...]).astype(BF16)
    bcu = _dot(a, win_ref[...])
    gate_b = bcu[:, :D_MODEL]
    z = bcu[:, D_MODEL:2 * D_MODEL] * bcu[:, 2 * D_MODEL:]
    zp = zprev_scr[...]
    row = lax.broadcasted_iota(jnp.int32, (tm, D_MODEL), 0)
    last, last2 = zp[CONV_TAIL - 1:CONV_TAIL], zp[CONV_TAIL - 2:CONV_TAIL - 1]
    z1 = jnp.where(row == 0, last, pltpu.roll(z, 1, 0))
    z2 = jnp.where(row == 0, last2, jnp.where(row == 1, last, pltpu.roll(z, 2, 0)))
    cw = cw_ref[...]
    conv = cw[0:1] * z2 + cw[1:2] * z1 + cw[2:3] * z
    y = x + _dot((gate_b * conv).astype(BF16), wout_ref[...])
    out_ref[...] = _ffn_math(y, gf_ref, wg_ref, wu_ref, wd_ref)
    tail = z[tm - CONV_TAIL:]
    zprev_scr[...] = tail
    ztail_ref[...] = tail


def _conv_ffn_call(h, g, win, cw, wout, zinit, gf, wg, wu, wd, layer, batch, tm):
    n = h.shape[0]
    nt = n // batch // tm
    row = pl.BlockSpec((tm, D_MODEL), lambda b, i: (b * nt + i, 0))
    tail_shape = (CONV_TAIL, D_MODEL)
    return pl.pallas_call(
        functools.partial(_conv_ffn_kernel, tm=tm),
        grid=(batch, nt),
        out_shape=(jax.ShapeDtypeStruct((n, D_MODEL), F32), jax.ShapeDtypeStruct(tail_shape, F32)),
        in_specs=[row, _const_spec((1, D_MODEL)), _const_spec(win.shape), _const_spec(cw.shape),
                  _const_spec(wout.shape), _const_spec(tail_shape)] + _ffn_specs(wg, wu, wd, layer),
        out_specs=(row, pl.BlockSpec(tail_shape, lambda b, i: (0, 0))),
        scratch_shapes=[pltpu.VMEM(tail_shape, F32)],
        compiler_params=pltpu.CompilerParams(dimension_semantics=("arbitrary", "arbitrary"),
                                             vmem_limit_bytes=VMEM_LIMIT),
        name="conv_ffn",
    )(h, g, win, cw, wout, zinit, gf, wg, wu, wd)


def _pool_ffn_kernel(h_ref, g_ref, wp_ref, sc_ref, hinit_ref, gf_ref, wg_ref, wu_ref, wd_ref,
                     out_ref, atail_ref, hist_scr, *, tm, from_start):
    @pl.when(pl.program_id(1) == 0)
    def _():
        hist_scr[...] = hinit_ref[...]

    x = h_ref[...]
    a = _rms(x, g_ref[...])
    ext = jnp.concatenate([hist_scr[...], a], axis=0)
    outs = []
    for gi, win in enumerate(POOL_WINDOWS):
        cols = slice(gi * POOL_GROUP_DIM, (gi + 1) * POOL_GROUP_DIM)
        s = ext[:, cols]
        k = 1
        while k < win:
            s = s + pltpu.roll(s, k, 0)
            k *= 2
        s = s[POOL_HIST:]
        if from_start:
            pos = lax.broadcasted_iota(jnp.int32, (tm, POOL_GROUP_DIM), 0)
            mean = s / jnp.minimum(win, pos + 1).astype(F32)
        else:
            mean = s * (1.0 / win)
        mix = (mean - a[:, cols]).astype(BF16)
        outs.append(_dot(mix, wp_ref[gi]))
    y = x + jnp.concatenate(outs, axis=-1) * sc_ref[...]
    out_ref[...] = _ffn_math(y, gf_ref, wg_ref, wu_ref, wd_ref)
    tail = a[tm - POOL_HIST:]
    hist_scr[...] = tail
    atail_ref[...] = tail


def _pool_ffn_call(h, g, wp, sc, hinit, gf, wg, wu, wd, layer, batch, tm, from_start):
    n = h.shape[0]
    nt = n // batch // tm
    row = pl.BlockSpec((tm, D_MODEL), lambda b, i: (b * nt + i, 0))
    hist_shape = (POOL_HIST, D_MODEL)
    return pl.pallas_call(
        functools.partial(_pool_ffn_kernel, tm=tm, from_start=from_start),
        grid=(batch, nt),
        out_shape=(jax.ShapeDtypeStruct((n, D_MODEL), F32), jax.ShapeDtypeStruct(hist_shape, F32)),
        in_specs=[row, _const_spec((1, D_MODEL)), _const_spec(wp.shape), _const_spec((1, D_MODEL)),
                  _const_spec(hist_shape)] + _ffn_specs(wg, wu, wd, layer),
        out_specs=(row, pl.BlockSpec(hist_shape, lambda b, i: (0, 0))),
        scratch_shapes=[pltpu.VMEM(hist_shape, F32)],
        compiler_params=pltpu.CompilerParams(dimension_semantics=("arbitrary", "arbitrary"),
                                             vmem_limit_bytes=VMEM_LIMIT),
        name="pool_ffn",
    )(h, g, wp, sc, hinit, gf, wg, wu, wd)


ATTN_TM = 512
ROW_TM = 512


def kernel(x, meta_tokens, rel_bias_table, norm_mix, norm_ffn, norm_final, attn_w_qkv, attn_b_qkv, attn_w_o,
           attn_b_o, attn_sinks, conv_w_in, conv_w, conv_w_out, pool_w, pool_scale, ffn_w_gate, ffn_w_up,
           ffn_w_down):
    batch, seq, _ = x.shape
    depth = norm_mix.shape[0]
    n_mixers = 3
    hr = x.reshape(batch * seq, D_MODEL)
    hm = meta_tokens.astype(F32)
    row = lambda v: v.reshape(1, -1)
    meta_codes = _meta_codes()
    wg, wu, wd = ffn_w_gate.astype(BF16), ffn_w_up.astype(BF16), ffn_w_down.astype(BF16)
    for i in range(depth):
        kind, j = i % n_mixers, i // n_mixers
        last = i == depth - 1
        gm, gf = row(norm_mix[i]), row(norm_ffn[i])
        if kind == 0:
            w, b = attn_w_qkv[j], attn_b_qkv[j]
            wqvT = jnp.concatenate([w[:, :Q_DIM], w[:, Q_DIM + KV_DIM:]], axis=1).T.astype(BF16)
            bqvT = jnp.broadcast_to(jnp.concatenate([b[:Q_DIM], b[Q_DIM + KV_DIM:]])[:, None], (Q_DIM + KV_DIM, LANES))
            wk, bk = w[:, Q_DIM:Q_DIM + KV_DIM].astype(BF16), row(b[Q_DIM:Q_DIM + KV_DIM])
            wo, bo = attn_w_o[j].astype(BF16), row(attn_b_o[j])
            tbl_meta = _meta_table(meta_codes, rel_bias_table, attn_sinks[j])
            tbl_real = _real_table(rel_bias_table, attn_sinks[j])
            hm, km, vmT = _attn_meta_call(hm, gm, wqvT, bqvT, wk, row(b), tbl_meta, wo, bo)
            qT, k, vT = _qkv_call(hr, gm, wqvT, bqvT, wk, bk, ROW_TM)
            hr = _attn_core_call(qT, k, vT, km, vmT, tbl_real, hr, wo, bo, batch, ATTN_TM)
        elif kind == 1:
            win, wout = conv_w_in[j].astype(BF16), conv_w_out[j].astype(BF16)
            hm, ztail = _conv_ffn_call(hm, gm, win, conv_w[j], wout, jnp.zeros((CONV_TAIL, D_MODEL), F32),
                                       gf, wg, wu, wd, i, 1, N_META)
            hr, _ = _conv_ffn_call(hr, gm, win, conv_w[j], wout, ztail, gf, wg, wu, wd, i, batch, ROW_TM)
            continue
        else:
            wp, sc = pool_w[j].astype(BF16), row(pool_scale[j])
            hm, atail = _pool_ffn_call(hm, gm, wp, sc, jnp.zeros((POOL_HIST, D_MODEL), F32),
                                       gf, wg, wu, wd, i, 1, N_META, True)
            hr, _ = _pool_ffn_call(hr, gm, wp, sc, atail, gf, wg, wu, wd, i, batch, ROW_TM, False)
            continue
        if not last:
            hm = _ffn_call(hm, gf, wg, wu, wd, i, row(norm_final), N_META, False)
        hr = _ffn_call(hr, gf, wg, wu, wd, i, row(norm_final), ROW_TM, last)
    return hr.reshape(batch, seq, D_MODEL)
```

```python
import functools
import math

import numpy as np
import jax
import jax.numpy as jnp
from jax import lax
from jax.experimental import pallas as pl
from jax.experimental.pallas import tpu as pltpu

F32 = jnp.float32
BF16 = jnp.bfloat16

D_MODEL = 1024
N_META = 16
RMS_EPS = 1e-6
N_HEADS = 16
N_KV = 4
HEAD_DIM = 64
GROUP = N_HEADS // N_KV
Q_DIM = N_HEADS * HEAD_DIM
KV_DIM = N_KV * HEAD_DIM
WINDOW = 128
N_BUCKETS = 32
MAX_DISTANCE = 128
POOL_WINDOWS = (2, 4, 8, 16)
POOL_GROUP_DIM = D_MODEL // len(POOL_WINDOWS)
POOL_HIST = 16
CONV_TAIL = 8
LANES = 128
LOG2E = math.log2(math.e)

KEY_ROWS = 2 * WINDOW + 2 * N_META
SINK_ROW = 2 * WINDOW + N_META
ONES_ROWS = 16

META_KEY_COLS = 128
CODE_SINK = N_BUCKETS
CODE_MASKED = N_BUCKETS + 1

VMEM_LIMIT = 56 * 1024 * 1024


def _rel_bucket_np(dist):
    max_exact = N_BUCKETS // 2
    d = np.maximum(dist, 0)
    df = np.maximum(d, 1).astype(np.float64)
    large = max_exact + (np.log(df / max_exact) / math.log(MAX_DISTANCE / max_exact)
                         * (N_BUCKETS - max_exact)).astype(np.int64)
    large = np.minimum(large, N_BUCKETS - 1)
    return np.where(d < max_exact, d, large).astype(np.int32)


def _meta_codes():
    i = np.arange(N_META)[:, None]
    m = np.arange(N_META)[None, :]
    code = np.full((N_META, META_KEY_COLS), CODE_MASKED, np.int32)
    code[:, :N_META] = np.where(i >= m, _rel_bucket_np(i - m), CODE_MASKED)
    code[:, N_META] = CODE_SINK
    return code[None]


def _band_codes():
    return np.stack([_rel_bucket_np(np.arange(WINDOW)), np.zeros(WINDOW, np.int32)])


def _first_meta_codes():
    m = np.arange(N_META)[:, None]
    i = np.arange(WINDOW)[None, :]
    return _rel_bucket_np(N_META + i - m)


def _const_spec(shape):
    zeros = (0,) * len(shape)
    return pl.BlockSpec(shape, lambda *_: zeros, pipeline_mode=pl.Buffered(1))


def _layer_spec(stacked_shape, layer):
    index = (layer,) + (0,) * (len(stacked_shape) - 1)
    return pl.BlockSpec((None,) + tuple(stacked_shape[1:]), lambda *_: index, pipeline_mode=pl.Buffered(1))


def _smem_spec():
    return pl.BlockSpec(memory_space=pltpu.SMEM)


def _rms(x, g):
    return x * lax.rsqrt(jnp.mean(x * x, axis=-1, keepdims=True) + RMS_EPS) * g


def _dot(a, b):
    return jnp.dot(a, b, preferred_element_type=F32)


def _dot_nt(a, b):
    return lax.dot_general(a, b, (((1,), (1,)), ((), ())), preferred_element_type=F32)


def _dot_tn(a, b):
    return lax.dot_general(a, b, (((0,), (0,)), ((), ())), preferred_element_type=F32)


def _select_by_code(code, tbl_ref, h, otherwise):
    val = otherwise
    for b in range(N_BUCKETS):
        val = jnp.where(code == b, tbl_ref[b, h], val)
    return val


def _meta_table_kernel(code_ref, tbl_ref, sink_ref, out_ref):
    n_var, rows, _ = code_ref.shape
    for v in range(n_var):
        code = code_ref[v]
        for h in range(N_HEADS):
            kh, g = divmod(h, GROUP)
            val = _select_by_code(code, tbl_ref, h, jnp.where(code == CODE_SINK, sink_ref[h], -jnp.inf))
            out_ref[v, kh, g * rows:(g + 1) * rows, :] = val


def _meta_table(codes, rel_table, sinks):
    n_var, rows, cols = codes.shape
    return pl.pallas_call(
        _meta_table_kernel,
        out_shape=jax.ShapeDtypeStruct((n_var, N_KV, GROUP * rows, cols), F32),
        in_specs=[pl.BlockSpec(memory_space=pltpu.VMEM), _smem_spec(), _smem_spec()],
        out_specs=pl.BlockSpec(memory_space=pltpu.VMEM),
        name="meta_bias_table",
    )(jnp.asarray(codes), rel_table, sinks)


def _real_table_kernel(band_code_ref, meta_code_ref, tbl_ref, sink_ref, out_ref):
    neg = -jnp.inf
    band_code = band_code_ref[0:1, :]
    meta_code = meta_code_ref[...]
    t = lax.broadcasted_iota(jnp.int32, (WINDOW, WINDOW), 0)
    i = lax.broadcasted_iota(jnp.int32, (WINDOW, WINDOW), 1)
    row16 = lax.broadcasted_iota(jnp.int32, (N_META, WINDOW), 0)
    for h in range(N_HEADS):
        kh, g = divmod(h, GROUP)
        lanes = slice(g * WINDOW, (g + 1) * WINDOW)
        by_dist = _select_by_code(band_code, tbl_ref, h, jnp.zeros((1, WINDOW), F32)) * LOG2E
        rolled = pltpu.roll(jnp.broadcast_to(by_dist, (WINDOW, WINDOW)), 0, 1, stride=1, stride_axis=0)
        prev_blk = jnp.where(i < t, rolled, neg)
        cur_blk = jnp.where(i >= t, rolled, neg)
        far = jnp.full((N_META, WINDOW), tbl_ref[N_BUCKETS - 1, h], F32)
        near = _select_by_code(meta_code, tbl_ref, h, far) * LOG2E
        far = far * LOG2E
        sink_pad = jnp.where(row16 == 0, sink_ref[h] * LOG2E, neg)
        for v in range(2):
            out_ref[v, kh, 0:WINDOW, lanes] = jnp.full((WINDOW, WINDOW), neg, F32) if v else prev_blk
            out_ref[v, kh, WINDOW:2 * WINDOW, lanes] = cur_blk
            out_ref[v, kh, 2 * WINDOW:SINK_ROW, lanes] = near if v else far
            out_ref[v, kh, SINK_ROW:KEY_ROWS, lanes] = sink_pad


def _real_table(rel_table, sinks):
    vmem = pl.BlockSpec(memory_space=pltpu.VMEM)
    return pl.pallas_call(
        _real_table_kernel,
        out_shape=jax.ShapeDtypeStruct((2, N_KV, KEY_ROWS, GROUP * WINDOW), F32),
        in_specs=[vmem, vmem, _smem_spec(), _smem_spec()],
        out_specs=vmem,
        name="real_bias_table",
    )(jnp.asarray(_band_codes()), jnp.asarray(_first_meta_codes()), rel_table, sinks)


def _qkv_kernel(h_ref, g_ref, wqv_ref, bqv_ref, wk_ref, bk_ref, qT_ref, k_ref, vT_ref, *, tm):
    a = _rms(h_ref[...], g_ref[...]).astype(BF16)
    bias = jnp.concatenate([bqv_ref[...]] * (tm // LANES), axis=1)
    qvT = _dot_nt(wqv_ref[...], a) + bias
    qT_ref[...] = (qvT[:Q_DIM] * (HEAD_DIM ** -0.5 * LOG2E)).astype(BF16)
    vT_ref[...] = qvT[Q_DIM:].astype(BF16)
    k_ref[...] = (_dot(a, wk_ref[...]) + bk_ref[...]).astype(BF16)


def _qkv_call(h, g, wqvT, bqvT, wk, bk, tm):
    n = h.shape[0]
    row = lambda c: pl.BlockSpec((tm, c), lambda i: (i, 0))
    col = lambda r: pl.BlockSpec((r, tm), lambda i: (0, i))
    return pl.pallas_call(
        functools.partial(_qkv_kernel, tm=tm),
        grid=(n // tm,),
        out_shape=(jax.ShapeDtypeStruct((Q_DIM, n), BF16),
                   jax.ShapeDtypeStruct((n, KV_DIM), BF16),
                   jax.ShapeDtypeStruct((KV_DIM, n), BF16)),
        in_specs=[row(D_MODEL), _const_spec((1, D_MODEL)), _const_spec(wqvT.shape), _const_spec(bqvT.shape),
                  _const_spec(wk.shape), _const_spec((1, KV_DIM))],
        out_specs=(col(Q_DIM), row(KV_DIM), col(KV_DIM)),
        compiler_params=pltpu.CompilerParams(dimension_semantics=("parallel",), vmem_limit_bytes=VMEM_LIMIT),
        name="attn_qkv",
    )(h, g, wqvT, bqvT, wk, bk)


def _attn_core_kernel(qT_ref, kp_ref, kc_ref, vTp_ref, vTc_ref, km_ref, vmT_ref, tbl_ref, h_ref, wo_ref, bo_ref,
                      out_ref, oT_scr, *, tm):
    first_tile = pl.program_id(1) == 0
    kext = jnp.concatenate([kp_ref[...], kc_ref[...]], axis=0)
    vText = jnp.concatenate([vTp_ref[...], vTc_ref[...]], axis=1)
    km = km_ref[...]
    vmT = vmT_ref[...]
    zk = jnp.zeros((KEY_ROWS - SINK_ROW, HEAD_DIM), BF16)
    ones = jnp.ones((ONES_ROWS, KEY_ROWS), BF16)

    def scores(j, kh):
        r0 = j * WINDOW
        cols = slice(kh * HEAD_DIM, (kh + 1) * HEAD_DIM)
        qsT = jnp.concatenate(
            [qT_ref[(kh * GROUP + g) * HEAD_DIM:(kh * GROUP + g + 1) * HEAD_DIM, r0:r0 + WINDOW]
             for g in range(GROUP)], axis=1)
        keys = jnp.concatenate([kext[r0:r0 + 2 * WINDOW, cols], km[:, cols], zk], axis=0)
        return _dot(keys, qsT)

    def finish(j, kh, sT, sT_next):
        r0 = j * WINDOW
        cols = slice(kh * HEAD_DIM, (kh + 1) * HEAD_DIM)
        var = jnp.where(first_tile, 1, 0) if j == 0 else 0
        sT = sT + tbl_ref[var, kh]
        m = jnp.max(sT, axis=0, keepdims=True)
        pT = jnp.exp2(sT - m).astype(BF16)
        valsT = jnp.concatenate([vText[cols, r0:r0 + 2 * WINDOW], vmT[cols, :]], axis=1)
        den = ones
        if sT_next is not None:
            bits = pltpu.bitcast(sT_next[KEY_ROWS - ONES_ROWS:, :2 * WINDOW], jnp.uint32)
            zero = pltpu.bitcast((bits >> 16) >> 16, F32).astype(BF16)
            den = jnp.concatenate([ones[:, :2 * WINDOW] + zero, ones[:, 2 * WINDOW:]], axis=1)
        oT = _dot(jnp.concatenate([valsT, den], axis=0), pT)
        oT = oT[:HEAD_DIM] / oT[HEAD_DIM:HEAD_DIM + 1]
        for g in range(GROUP):
            h0 = (kh * GROUP + g) * HEAD_DIM
            oT_scr[h0:h0 + HEAD_DIM, r0:r0 + WINDOW] = oT[:, g * WINDOW:(g + 1) * WINDOW].astype(BF16)

    units = [(j, kh) for j in range(tm // WINDOW) for kh in range(N_KV)]
    sT = scores(*units[0])
    for idx, unit in enumerate(units):
        sT_next = scores(*units[idx + 1]) if idx + 1 < len(units) else None
        finish(*unit, sT, sT_next)
        sT = sT_next
    out_ref[...] = h_ref[...] + _dot_tn(oT_scr[...], wo_ref[...]) + bo_ref[...]


def _attn_core_call(qT, k, vT, km, vmT, tbl, h, wo, bo, batch, tm):
    n = h.shape[0]
    nt = n // batch // tm
    per = tm // WINDOW
    prev_idx = lambda b, i: jnp.maximum((b * nt + i) * per - 1, 0)
    row = lambda c: pl.BlockSpec((tm, c), lambda b, i: (b * nt + i, 0))
    col = lambda r: pl.BlockSpec((r, tm), lambda b, i: (0, b * nt + i))
    return pl.pallas_call(
        functools.partial(_attn_core_kernel, tm=tm),
        grid=(batch, nt),
        out_shape=jax.ShapeDtypeStruct((n, D_MODEL), F32),
        in_specs=[col(Q_DIM),
                  pl.BlockSpec((WINDOW, KV_DIM), lambda b, i: (prev_idx(b, i), 0)), row(KV_DIM),
                  pl.BlockSpec((KV_DIM, WINDOW), lambda b, i: (0, prev_idx(b, i))), col(KV_DIM),
                  _const_spec(km.shape), _const_spec(vmT.shape), _const_spec(tbl.shape),
                  row(D_MODEL), _const_spec(wo.shape), _const_spec((1, D_MODEL))],
        out_specs=row(D_MODEL),
        scratch_shapes=[pltpu.VMEM((Q_DIM, tm), BF16)],
        compiler_params=pltpu.CompilerParams(dimension_semantics=("parallel", "parallel"),
                                             vmem_limit_bytes=VMEM_LIMIT),
        name="attn_core",
    )(qT, k, k, vT, vT, km, vmT, tbl, h, wo, bo)


def _attn_meta_kernel(h_ref, g_ref, wqv_ref, bqv_ref, wk_ref, b_ref, tbl_ref, wo_ref, bo_ref,
                      out_ref, km_ref, vmT_ref):
    x = h_ref[...]
    a = _rms(x, g_ref[...]).astype(BF16)
    a2 = jnp.concatenate([a, jnp.zeros_like(a)], axis=0)
    lane = lax.broadcasted_iota(jnp.int32, (KV_DIM, 2 * N_META), 1)
    vT = _dot_nt(wqv_ref[Q_DIM:, :], a2) + jnp.where(lane < N_META, bqv_ref[Q_DIM:, :2 * N_META], 0.0)
    vmT_ref[...] = vT.astype(BF16)
    b = b_ref[...]
    q = ((_dot_nt(a, wqv_ref[:Q_DIM, :]) + b[:, :Q_DIM]) * (HEAD_DIM ** -0.5)).astype(BF16)
    k = (_dot(a, wk_ref[...]) + b[:, Q_DIM:Q_DIM + KV_DIM]).astype(BF16)
    v = (_dot_nt(a, wqv_ref[Q_DIM:, :]) + b[:, Q_DIM + KV_DIM:]).astype(BF16)
    km_ref[...] = k
    pad = META_KEY_COLS - N_META
    vcat = jnp.concatenate([v, jnp.zeros((pad, KV_DIM), BF16)], axis=0)
    pieces = []
    for kh in range(N_KV):
        cols = slice(kh * HEAD_DIM, (kh + 1) * HEAD_DIM)
        qs = jnp.concatenate([q[:, (kh * GROUP + g) * HEAD_DIM:(kh * GROUP + g + 1) * HEAD_DIM]
                              for g in range(GROUP)], axis=0)
        kcat = jnp.concatenate([k[:, cols], jnp.zeros((pad, HEAD_DIM), BF16)], axis=0)
        s = _dot_nt(qs, kcat) + tbl_ref[0, kh]
        p = jnp.exp(s - jnp.max(s, axis=-1, keepdims=True))
        l = jnp.sum(p, axis=-1, keepdims=True)
        o = _dot(p.astype(BF16), vcat)[:, cols] / l
        pieces += [o[g * N_META:(g + 1) * N_META] for g in range(GROUP)]
    o_all = jnp.concatenate(pieces, axis=-1).astype(BF16)
    out_ref[...] = x + _dot(o_all, wo_ref[...]) + bo_ref[...]


def _attn_meta_call(h, g, wqvT, bqvT, wk, bk, tbl, wo, bo):
    vmem = pl.BlockSpec(memory_space=pltpu.VMEM)
    return pl.pallas_call(
        _attn_meta_kernel,
        out_shape=(jax.ShapeDtypeStruct((N_META, D_MODEL), F32),
                   jax.ShapeDtypeStruct((N_META, KV_DIM), BF16),
                   jax.ShapeDtypeStruct((KV_DIM, 2 * N_META), BF16)),
        in_specs=[vmem] * 9,
        out_specs=(vmem, vmem, vmem),
        compiler_params=pltpu.CompilerParams(vmem_limit_bytes=VMEM_LIMIT),
        name="attn_meta",
    )(h, g, wqvT, bqvT, wk, bk, tbl, wo, bo)


def _ffn_math(x, g_ref, wg_ref, wu_ref, wd_ref):
    a = _rms(x, g_ref[...]).astype(BF16)
    gate = _dot(a, wg_ref[...])
    up = _dot(a, wu_ref[...])
    hmid = (gate * (1.0 / (1.0 + jnp.exp(-gate))) * up).astype(BF16)
    return x + _dot(hmid, wd_ref[...])


def _ffn_kernel(h_ref, g_ref, wg_ref, wu_ref, wd_ref, gf_ref, out_ref, *, final_norm):
    y = _ffn_math(h_ref[...], g_ref, wg_ref, wu_ref, wd_ref)
    out_ref[...] = _rms(y, gf_ref[...]) if final_norm else y


def _ffn_specs(wg, wu, wd, layer):
    return [_const_spec((1, D_MODEL)), _layer_spec(wg.shape, layer), _layer_spec(wu.shape, layer),
            _layer_spec(wd.shape, layer)]


def _ffn_call(h, g, wg, wu, wd, layer, gf, tm, final_norm):
    n = h.shape[0]
    row = pl.BlockSpec((tm, D_MODEL), lambda i: (i, 0))
    return pl.pallas_call(
        functools.partial(_ffn_kernel, final_norm=final_norm),
        grid=(n // tm,),
        out_shape=jax.ShapeDtypeStruct((n, D_MODEL), F32),
        in_specs=[row] + _ffn_specs(wg, wu, wd, layer) + [_const_spec((1, D_MODEL))],
        out_specs=row,
        compiler_params=pltpu.CompilerParams(dimension_semantics=("parallel",), vmem_limit_bytes=VMEM_LIMIT),
        name="ffn",
    )(h, g, wg, wu, wd, gf)


def _conv_ffn_kernel(h_ref, g_ref, win_ref, cw_ref, wout_ref, zinit_ref, gf_ref, wg_ref, wu_ref, wd_ref,
                     out_ref, ztail_ref, zprev_scr, *, tm):
    @pl.when(pl.program_id(1) == 0)
    def _():
        zprev_scr[...] = zinit_ref[...]

    x = h_ref[...]
    a = _rms(x, g_ref[...]).astype(BF16)
    bcu = _dot(a, win_ref[...])
    gate_b = bcu[:, :D_MODEL]
    z = bcu[:, D_MODEL:2 * D_MODEL] * bcu[:, 2 * D_MODEL:]
    zp = zprev_scr[...]
    row = lax.broadcasted_iota(jnp.int32, (tm, D_MODEL), 0)
    last, last2 = zp[CONV_TAIL - 1:CONV_TAIL], zp[CONV_TAIL - 2:CONV_TAIL - 1]
    z1 = jnp.where(row == 0, last, pltpu.roll(z, 1, 0))
    z2 = jnp.where(row == 0, last2, jnp.where(row == 1, last, pltpu.roll(z, 2, 0)))
    cw = cw_ref[...]
    conv = cw[0:1] * z2 + cw[1:2] * z1 + cw[2:3] * z
    y = x + _dot((gate_b * conv).astype(BF16), wout_ref[...])
    out_ref[...] = _ffn_math(y, gf_ref, wg_ref, wu_ref, wd_ref)
    tail = z[tm - CONV_TAIL:]
    zprev_scr[...] = tail
    ztail_ref[...] = tail


def _conv_ffn_call(h, g, win, cw, wout, zinit, gf, wg, wu, wd, layer, batch, tm):
    n = h.shape[0]
    nt = n // batch // tm
    row = pl.BlockSpec((tm, D_MODEL), lambda b, i: (b * nt + i, 0))
    tail_shape = (CONV_TAIL, D_MODEL)
    return pl.pallas_call(
        functools.partial(_conv_ffn_kernel, tm=tm),
        grid=(batch, nt),
        out_shape=(jax.ShapeDtypeStruct((n, D_MODEL), F32), jax.ShapeDtypeStruct(tail_shape, F32)),
        in_specs=[row, _const_spec((1, D_MODEL)), _const_spec(win.shape), _const_spec(cw.shape),
                  _const_spec(wout.shape), _const_spec(tail_shape)] + _ffn_specs(wg, wu, wd, layer),
        out_specs=(row, pl.BlockSpec(tail_shape, lambda b, i: (0, 0))),
        scratch_shapes=[pltpu.VMEM(tail_shape, F32)],
        compiler_params=pltpu.CompilerParams(dimension_semantics=("arbitrary", "arbitrary"),
                                             vmem_limit_bytes=VMEM_LIMIT),
        name="conv_ffn",
    )(h, g, win, cw, wout, zinit, gf, wg, wu, wd)


def _pool_ffn_kernel(h_ref, g_ref, wp_ref, sc_ref, hinit_ref, gf_ref, wg_ref, wu_ref, wd_ref,
                     out_ref, atail_ref, hist_scr, *, tm, from_start):
    @pl.when(pl.program_id(1) == 0)
    def _():
        hist_scr[...] = hinit_ref[...]

    x = h_ref[...]
    a = _rms(x, g_ref[...])
    ext = jnp.concatenate([hist_scr[...], a], axis=0)
    outs = []
    for gi, win in enumerate(POOL_WINDOWS):
        cols = slice(gi * POOL_GROUP_DIM, (gi + 1) * POOL_GROUP_DIM)
        s = ext[:, cols]
        k = 1
        while k < win:
            s = s + pltpu.roll(s, k, 0)
            k *= 2
        s = s[POOL_HIST:]
        if from_start:
            pos = lax.broadcasted_iota(jnp.int32, (tm, POOL_GROUP_DIM), 0)
            mean = s / jnp.minimum(win, pos + 1).astype(F32)
        else:
            mean = s * (1.0 / win)
        mix = (mean - a[:, cols]).astype(BF16)
        outs.append(_dot(mix, wp_ref[gi]))
    y = x + jnp.concatenate(outs, axis=-1) * sc_ref[...]
    out_ref[...] = _ffn_math(y, gf_ref, wg_ref, wu_ref, wd_ref)
    tail = a[tm - POOL_HIST:]
    hist_scr[...] = tail
    atail_ref[...] = tail


def _pool_ffn_call(h, g, wp, sc, hinit, gf, wg, wu, wd, layer, batch, tm, from_start):
    n = h.shape[0]
    nt = n // batch // tm
    row = pl.BlockSpec((tm, D_MODEL), lambda b, i: (b * nt + i, 0))
    hist_shape = (POOL_HIST, D_MODEL)
    return pl.pallas_call(
        functools.partial(_pool_ffn_kernel, tm=tm, from_start=from_start),
        grid=(batch, nt),
        out_shape=(jax.ShapeDtypeStruct((n, D_MODEL), F32), jax.ShapeDtypeStruct(hist_shape, F32)),
        in_specs=[row, _const_spec((1, D_MODEL)), _const_spec(wp.shape), _const_spec((1, D_MODEL)),
                  _const_spec(hist_shape)] + _ffn_specs(wg, wu, wd, layer),
        out_specs=(row, pl.BlockSpec(hist_shape, lambda b, i: (0, 0))),
        scratch_shapes=[pltpu.VMEM(hist_shape, F32)],
        compiler_params=pltpu.CompilerParams(dimension_semantics=("arbitrary", "arbitrary"),
                                             vmem_limit_bytes=VMEM_LIMIT),
        name="pool_ffn",
    )(h, g, wp, sc, hinit, gf, wg, wu, wd)


ATTN_TM = 512
ROW_TM = 1024
CONV_TM = 512


def kernel(x, meta_tokens, rel_bias_table, norm_mix, norm_ffn, norm_final, attn_w_qkv, attn_b_qkv, attn_w_o,
           attn_b_o, attn_sinks, conv_w_in, conv_w, conv_w_out, pool_w, pool_scale, ffn_w_gate, ffn_w_up,
           ffn_w_down):
    batch, seq, _ = x.shape
    depth = norm_mix.shape[0]
    n_mixers = 3
    hr = x.reshape(batch * seq, D_MODEL)
    hm = meta_tokens.astype(F32)
    row = lambda v: v.reshape(1, -1)
    meta_codes = _meta_codes()
    wg, wu, wd = ffn_w_gate.astype(BF16), ffn_w_up.astype(BF16), ffn_w_down.astype(BF16)
    for i in range(depth):
        kind, j = i % n_mixers, i // n_mixers
        last = i == depth - 1
        gm, gf = row(norm_mix[i]), row(norm_ffn[i])
        if kind == 0:
            w, b = attn_w_qkv[j], attn_b_qkv[j]
            wqvT = jnp.concatenate([w[:, :Q_DIM], w[:, Q_DIM + KV_DIM:]], axis=1).T.astype(BF16)
            bqvT = jnp.broadcast_to(jnp.concatenate([b[:Q_DIM], b[Q_DIM + KV_DIM:]])[:, None], (Q_DIM + KV_DIM, LANES))
            wk, bk = w[:, Q_DIM:Q_DIM + KV_DIM].astype(BF16), row(b[Q_DIM:Q_DIM + KV_DIM])
            wo, bo = attn_w_o[j].astype(BF16), row(attn_b_o[j])
            tbl_meta = _meta_table(meta_codes, rel_bias_table, attn_sinks[j])
            tbl_real = _real_table(rel_bias_table, attn_sinks[j])
            hm, km, vmT = _attn_meta_call(hm, gm, wqvT, bqvT, wk, row(b), tbl_meta, wo, bo)
            qT, k, vT = _qkv_call(hr, gm, wqvT, bqvT, wk, bk, ROW_TM)
            hr = _attn_core_call(qT, k, vT, km, vmT, tbl_real, hr, wo, bo, batch, ATTN_TM)
        elif kind == 1:
            win, wout = conv_w_in[j].astype(BF16), conv_w_out[j].astype(BF16)
            hm, ztail = _conv_ffn_call(hm, gm, win, conv_w[j], wout, jnp.zeros((CONV_TAIL, D_MODEL), F32),
                                       gf, wg, wu, wd, i, 1, N_META)
            hr, _ = _conv_ffn_call(hr, gm, win, conv_w[j], wout, ztail, gf, wg, wu, wd, i, batch, CONV_TM)
            continue
        else:
            wp, sc = pool_w[j].astype(BF16), row(pool_scale[j])
            hm, atail = _pool_ffn_call(hm, gm, wp, sc, jnp.zeros((POOL_HIST, D_MODEL), F32),
                                       gf, wg, wu, wd, i, 1, N_META, True)
            hr, _ = _pool_ffn_call(hr, gm, wp, sc, atail, gf, wg, wu, wd, i, batch, ROW_TM, False)
            continue
        if not last:
            hm = _ffn_call(hm, gf, wg, wu, wd, i, row(norm_final), N_META, False)
        hr = _ffn_call(hr, gf, wg, wu, wd, i, row(norm_final), ROW_TM, last)
    return hr.reshape(batch, seq, D_MODEL)
```

```python
import functools
import math

import numpy as np
import jax
import jax.numpy as jnp
from jax import lax
from jax.experimental import pallas as pl
from jax.experimental.pallas import tpu as pltpu

F32 = jnp.float32
BF16 = jnp.bfloat16

D_MODEL = 1024
N_META = 16
RMS_EPS = 1e-6
N_HEADS = 16
N_KV = 4
HEAD_DIM = 64
GROUP = N_HEADS // N_KV
Q_DIM = N_HEADS * HEAD_DIM
KV_DIM = N_KV * HEAD_DIM
WINDOW = 128
N_BUCKETS = 32
MAX_DISTANCE = 128
POOL_WINDOWS = (2, 4, 8, 16)
POOL_GROUP_DIM = D_MODEL // len(POOL_WINDOWS)
POOL_HIST = 16
CONV_TAIL = 8
LANES = 128
FFN_LEAD_ROWS = 256
LOG2E = math.log2(math.e)

Q_BLK = 128
BAND_ROWS = WINDOW + Q_BLK
SINK_ROW = BAND_ROWS + N_META
KEY_ROWS = SINK_ROW + N_META
N_NEAR = WINDOW // Q_BLK
ONES_ROWS = 16

META_KEY_COLS = 128
CODE_SINK = N_BUCKETS
CODE_MASKED = N_BUCKETS + 1

VMEM_LIMIT = 56 * 1024 * 1024


def _rel_bucket_np(dist):
    max_exact = N_BUCKETS // 2
    d = np.maximum(dist, 0)
    df = np.maximum(d, 1).astype(np.float64)
    large = max_exact + (np.log(df / max_exact) / math.log(MAX_DISTANCE / max_exact)
                         * (N_BUCKETS - max_exact)).astype(np.int64)
    large = np.minimum(large, N_BUCKETS - 1)
    return np.where(d < max_exact, d, large).astype(np.int32)


def _meta_codes():
    i = np.arange(N_META)[:, None]
    m = np.arange(N_META)[None, :]
    code = np.full((N_META, META_KEY_COLS), CODE_MASKED, np.int32)
    code[:, :N_META] = np.where(i >= m, _rel_bucket_np(i - m), CODE_MASKED)
    code[:, N_META] = CODE_SINK
    return code[None]


def _band_codes():
    return np.stack([_rel_bucket_np(np.arange(WINDOW)), np.zeros(WINDOW, np.int32)])


def _near_meta_codes():
    m = np.arange(N_META)[:, None]
    i = np.arange(Q_BLK)[None, :]
    return np.stack([_rel_bucket_np(N_META + n * Q_BLK + i - m) for n in range(N_NEAR)])


def _const_spec(shape):
    zeros = (0,) * len(shape)
    return pl.BlockSpec(shape, lambda *_: zeros, pipeline_mode=pl.Buffered(1))


def _layer_spec(stacked_shape, layer):
    index = (layer,) + (0,) * (len(stacked_shape) - 1)
    return pl.BlockSpec((None,) + tuple(stacked_shape[1:]), lambda *_: index, pipeline_mode=pl.Buffered(1))


def _smem_spec():
    return pl.BlockSpec(memory_space=pltpu.SMEM)


def _rms(x, g):
    return x * lax.rsqrt(jnp.mean(x * x, axis=-1, keepdims=True) + RMS_EPS) * g


def _dot(a, b):
    return jnp.dot(a, b, preferred_element_type=F32)


def _dot_nt(a, b):
    return lax.dot_general(a, b, (((1,), (1,)), ((), ())), preferred_element_type=F32)


def _dot_tn(a, b):
    return lax.dot_general(a, b, (((0,), (0,)), ((), ())), preferred_element_type=F32)


def _select_by_code(code, tbl_ref, h, otherwise):
    val = otherwise
    for b in range(N_BUCKETS):
        val = jnp.where(code == b, tbl_ref[b, h], val)
    return val


def _meta_table_kernel(code_ref, tbl_ref, sink_ref, out_ref):
    n_var, rows, _ = code_ref.shape
    for v in range(n_var):
        code = code_ref[v]
        for h in range(N_HEADS):
            kh, g = divmod(h, GROUP)
            val = _select_by_code(code, tbl_ref, h, jnp.where(code == CODE_SINK, sink_ref[h], -jnp.inf))
            out_ref[v, kh, g * rows:(g + 1) * rows, :] = val


def _meta_table(codes, rel_table, sinks):
    n_var, rows, cols = codes.shape
    return pl.pallas_call(
        _meta_table_kernel,
        out_shape=jax.ShapeDtypeStruct((n_var, N_KV, GROUP * rows, cols), F32),
        in_specs=[pl.BlockSpec(memory_space=pltpu.VMEM), _smem_spec(), _smem_spec()],
        out_specs=pl.BlockSpec(memory_space=pltpu.VMEM),
        name="meta_bias_table",
    )(jnp.asarray(codes), rel_table, sinks)


def _real_table_kernel(band_code_ref, meta_code_ref, tbl_ref, sink_ref, out_ref):
    neg = -jnp.inf
    band_code = band_code_ref[0:1, :]
    t = lax.broadcasted_iota(jnp.int32, (BAND_ROWS, WINDOW), 0)
    i = lax.broadcasted_iota(jnp.int32, (BAND_ROWS, WINDOW), 1)
    in_window = (i < t) & (t <= i + WINDOW)
    row16 = lax.broadcasted_iota(jnp.int32, (N_META, Q_BLK), 0)
    for h in range(N_HEADS):
        kh, g = divmod(h, GROUP)
        lanes = slice(g * Q_BLK, (g + 1) * Q_BLK)
        by_dist = _select_by_code(band_code, tbl_ref, h, jnp.zeros((1, WINDOW), F32)) * LOG2E
        rolled = pltpu.roll(jnp.broadcast_to(by_dist, (WINDOW, WINDOW)), 0, 1, stride=1, stride_axis=0)
        rolled = jnp.concatenate([rolled, rolled[:BAND_ROWS - WINDOW]], axis=0)
        far = jnp.full((N_META, Q_BLK), tbl_ref[N_BUCKETS - 1, h], F32)
        sink_pad = jnp.where(row16 == 0, sink_ref[h] * LOG2E, neg)
        for v in range(1 + N_NEAR):
            if v == 0:
                valid, meta = in_window, far * LOG2E
            else:
                first_key_row = WINDOW - (v - 1) * Q_BLK
                valid = in_window & (t >= first_key_row)
                meta = _select_by_code(meta_code_ref[v - 1], tbl_ref, h, far) * LOG2E
            out_ref[v, kh, 0:BAND_ROWS, lanes] = jnp.where(valid, rolled, neg)[:, :Q_BLK]
            out_ref[v, kh, BAND_ROWS:SINK_ROW, lanes] = meta
            out_ref[v, kh, SINK_ROW:KEY_ROWS, lanes] = sink_pad


def _real_table(rel_table, sinks):
    vmem = pl.BlockSpec(memory_space=pltpu.VMEM)
    return pl.pallas_call(
        _real_table_kernel,
        out_shape=jax.ShapeDtypeStruct((1 + N_NEAR, N_KV, KEY_ROWS, GROUP * Q_BLK), F32),
        in_specs=[vmem, vmem, _smem_spec(), _smem_spec()],
        out_specs=vmem,
        name="real_bias_table",
    )(jnp.asarray(_band_codes()), jnp.asarray(_near_meta_codes()), rel_table, sinks)


def _qkv_kernel(h_ref, g_ref, wqv_ref, bqv_ref, wk_ref, bk_ref, qT_ref, k_ref, vT_ref, *, tm):
    a = _rms(h_ref[...], g_ref[...]).astype(BF16)
    bias = jnp.concatenate([bqv_ref[...]] * (tm // LANES), axis=1)
    qvT = _dot_nt(wqv_ref[...], a) + bias
    qT_ref[...] = (qvT[:Q_DIM] * (HEAD_DIM ** -0.5 * LOG2E)).astype(BF16)
    vT_ref[...] = qvT[Q_DIM:].astype(BF16)
    k_ref[...] = (_dot(a, wk_ref[...]) + bk_ref[...]).astype(BF16)


def _qkv_call(h, g, wqvT, bqvT, wk, bk, tm):
    n = h.shape[0]
    row = lambda c: pl.BlockSpec((tm, c), lambda i: (i, 0))
    col = lambda r: pl.BlockSpec((r, tm), lambda i: (0, i))
    return pl.pallas_call(
        functools.partial(_qkv_kernel, tm=tm),
        grid=(n // tm,),
        out_shape=(jax.ShapeDtypeStruct((Q_DIM, n), BF16),
                   jax.ShapeDtypeStruct((n, KV_DIM), BF16),
                   jax.ShapeDtypeStruct((KV_DIM, n), BF16)),
        in_specs=[row(D_MODEL), _const_spec((1, D_MODEL)), _const_spec(wqvT.shape), _const_spec(bqvT.shape),
                  _const_spec(wk.shape), _const_spec((1, KV_DIM))],
        out_specs=(col(Q_DIM), row(KV_DIM), col(KV_DIM)),
        compiler_params=pltpu.CompilerParams(dimension_semantics=("parallel",), vmem_limit_bytes=VMEM_LIMIT),
        name="attn_qkv",
    )(h, g, wqvT, bqvT, wk, bk)


def _attn_core_kernel(qT_ref, kp_ref, kc_ref, vTp_ref, vTc_ref, km_ref, vmT_ref, tbl_ref, h_ref, wo_ref, bo_ref,
                      out_ref, oT_scr, *, tm):
    first_tile = pl.program_id(1) == 0
    kext = jnp.concatenate([kp_ref[...], kc_ref[...]], axis=0)
    vText = jnp.concatenate([vTp_ref[...], vTc_ref[...]], axis=1)
    km = km_ref[...]
    vmT = vmT_ref[...]
    zk = jnp.zeros((KEY_ROWS - SINK_ROW, HEAD_DIM), BF16)
    ones = jnp.ones((ONES_ROWS, KEY_ROWS), BF16)

    def scores(j, kh):
        r0 = j * Q_BLK
        cols = slice(kh * HEAD_DIM, (kh + 1) * HEAD_DIM)
        qsT = jnp.concatenate(
            [qT_ref[(kh * GROUP + g) * HEAD_DIM:(kh * GROUP + g + 1) * HEAD_DIM, r0:r0 + Q_BLK]
             for g in range(GROUP)], axis=1)
        keys = jnp.concatenate([kext[r0:r0 + BAND_ROWS, cols], km[:, cols], zk], axis=0)
        return _dot(keys, qsT)

    def finish(j, kh, sT, sT_next):
        r0 = j * Q_BLK
        cols = slice(kh * HEAD_DIM, (kh + 1) * HEAD_DIM)
        var = jnp.where(first_tile, 1 + j, 0) if j < N_NEAR else 0
        sT = sT + tbl_ref[var, kh]
        m = jnp.max(sT, axis=0, keepdims=True)
        pT = jnp.exp2(sT - m).astype(BF16)
        valsT = jnp.concatenate([vText[cols, r0:r0 + BAND_ROWS], vmT[cols, :]], axis=1)
        den = ones
        if sT_next is not None:
            bits = pltpu.bitcast(sT_next[KEY_ROWS - ONES_ROWS:, :LANES], jnp.uint32)
            zero = pltpu.bitcast((bits >> 16) >> 16, F32).astype(BF16)
            den = jnp.concatenate([ones[:, :LANES] + zero, ones[:, LANES:]], axis=1)
        oT = _dot(jnp.concatenate([valsT, den], axis=0), pT)
        oT = oT[:HEAD_DIM] / oT[HEAD_DIM:HEAD_DIM + 1]
        for g in range(GROUP):
            h0 = (kh * GROUP + g) * HEAD_DIM
            oT_scr[h0:h0 + HEAD_DIM, r0:r0 + Q_BLK] = oT[:, g * Q_BLK:(g + 1) * Q_BLK].astype(BF16)

    units = [(j, kh) for j in range(tm // Q_BLK) for kh in range(N_KV)]
    pending = [scores(*u) for u in units[:ATTN_LOOKAHEAD]]
    for idx, unit in enumerate(units):
        ahead = idx + ATTN_LOOKAHEAD
        if ahead < len(units):
            pending.append(scores(*units[ahead]))
        sT = pending.pop(0)
        finish(*unit, sT, pending[-1] if pending else None)
    out_ref[...] = h_ref[...] + _dot_tn(oT_scr[...], wo_ref[...]) + bo_ref[...]


def _attn_core_call(qT, k, vT, km, vmT, tbl, h, wo, bo, batch, tm):
    n = h.shape[0]
    nt = n // batch // tm
    per = tm // WINDOW
    prev_idx = lambda b, i: jnp.maximum((b * nt + i) * per - 1, 0)
    row = lambda c: pl.BlockSpec((tm, c), lambda b, i: (b * nt + i, 0))
    col = lambda r: pl.BlockSpec((r, tm), lambda b, i: (0, b * nt + i))
    return pl.pallas_call(
        functools.partial(_attn_core_kernel, tm=tm),
        grid=(batch, nt),
        out_shape=jax.ShapeDtypeStruct((n, D_MODEL), F32),
        in_specs=[col(Q_DIM),
                  pl.BlockSpec((WINDOW, KV_DIM), lambda b, i: (prev_idx(b, i), 0)), row(KV_DIM),
                  pl.BlockSpec((KV_DIM, WINDOW), lambda b, i: (0, prev_idx(b, i))), col(KV_DIM),
                  _const_spec(km.shape), _const_spec(vmT.shape), _const_spec(tbl.shape),
                  row(D_MODEL), _const_spec(wo.shape), _const_spec((1, D_MODEL))],
        out_specs=row(D_MODEL),
        scratch_shapes=[pltpu.VMEM((Q_DIM, tm), BF16)],
        compiler_params=pltpu.CompilerParams(dimension_semantics=("parallel", "parallel"),
                                             vmem_limit_bytes=VMEM_LIMIT),
        name="attn_core",
    )(qT, k, k, vT, vT, km, vmT, tbl, h, wo, bo)


def _attn_meta_kernel(h_ref, g_ref, wqv_ref, bqv_ref, wk_ref, b_ref, tbl_ref, wo_ref, bo_ref,
                      out_ref, km_ref, vmT_ref):
    x = h_ref[...]
    a = _rms(x, g_ref[...]).astype(BF16)
    a2 = jnp.concatenate([a, jnp.zeros_like(a)], axis=0)
    lane = lax.broadcasted_iota(jnp.int32, (KV_DIM, 2 * N_META), 1)
    vT = _dot_nt(wqv_ref[Q_DIM:, :], a2) + jnp.where(lane < N_META, bqv_ref[Q_DIM:, :2 * N_META], 0.0)
    vmT_ref[...] = vT.astype(BF16)
    b = b_ref[...]
    q = ((_dot_nt(a, wqv_ref[:Q_DIM, :]) + b[:, :Q_DIM]) * (HEAD_DIM ** -0.5)).astype(BF16)
    k = (_dot(a, wk_ref[...]) + b[:, Q_DIM:Q_DIM + KV_DIM]).astype(BF16)
    v = (_dot_nt(a, wqv_ref[Q_DIM:, :]) + b[:, Q_DIM + KV_DIM:]).astype(BF16)
    km_ref[...] = k
    pad = META_KEY_COLS - N_META
    vcat = jnp.concatenate([v, jnp.zeros((pad, KV_DIM), BF16)], axis=0)
    pieces = []
    for kh in range(N_KV):
        cols = slice(kh * HEAD_DIM, (kh + 1) * HEAD_DIM)
        qs = jnp.concatenate([q[:, (kh * GROUP + g) * HEAD_DIM:(kh * GROUP + g + 1) * HEAD_DIM]
                              for g in range(GROUP)], axis=0)
        kcat = jnp.concatenate([k[:, cols], jnp.zeros((pad, HEAD_DIM), BF16)], axis=0)
        s = _dot_nt(qs, kcat) + tbl_ref[0, kh]
        p = jnp.exp(s - jnp.max(s, axis=-1, keepdims=True))
        l = jnp.sum(p, axis=-1, keepdims=True)
        o = _dot(p.astype(BF16), vcat)[:, cols] / l
        pieces += [o[g * N_META:(g + 1) * N_META] for g in range(GROUP)]
    o_all = jnp.concatenate(pieces, axis=-1).astype(BF16)
    out_ref[...] = x + _dot(o_all, wo_ref[...]) + bo_ref[...]


def _attn_meta_call(h, g, wqvT, bqvT, wk, bk, tbl, wo, bo):
    vmem = pl.BlockSpec(memory_space=pltpu.VMEM)
    return pl.pallas_call(
        _attn_meta_kernel,
        out_shape=(jax.ShapeDtypeStruct((N_META, D_MODEL), F32),
                   jax.ShapeDtypeStruct((N_META, KV_DIM), BF16),
                   jax.ShapeDtypeStruct((KV_DIM, 2 * N_META), BF16)),
        in_specs=[vmem] * 9,
        out_specs=(vmem, vmem, vmem),
        compiler_params=pltpu.CompilerParams(vmem_limit_bytes=VMEM_LIMIT),
        name="attn_meta",
    )(h, g, wqvT, bqvT, wk, bk, tbl, wo, bo)


def _ffn_math(x, g_ref, wg_ref, wu_ref, wd_ref):
    rows = x.shape[0]
    lead = FFN_LEAD_ROWS if rows > 2 * FFN_LEAD_ROWS else rows
    chunks = [x[:lead]] + ([x[lead:]] if lead < rows else [])
    hmids = []
    for xc in chunks:
        a = _rms(xc, g_ref[...]).astype(BF16)
        gate = _dot(a, wg_ref[...])
        up = _dot(a, wu_ref[...])
        hmids.append((gate * (1.0 / (1.0 + jnp.exp(-gate))) * up).astype(BF16))
    outs = [xc + _dot(hm, wd_ref[...]) for xc, hm in zip(chunks, hmids)]
    return outs[0] if len(outs) == 1 else jnp.concatenate(outs, axis=0)


def _ffn_kernel(h_ref, g_ref, wg_ref, wu_ref, wd_ref, gf_ref, out_ref, *, final_norm):
    y = _ffn_math(h_ref[...], g_ref, wg_ref, wu_ref, wd_ref)
    out_ref[...] = _rms(y, gf_ref[...]) if final_norm else y


def _ffn_specs(wg, wu, wd, layer):
    return [_const_spec((1, D_MODEL)), _layer_spec(wg.shape, layer), _layer_spec(wu.shape, layer),
            _layer_spec(wd.shape, layer)]


def _ffn_call(h, g, wg, wu, wd, layer, gf, tm, final_norm):
    n = h.shape[0]
    row = pl.BlockSpec((tm, D_MODEL), lambda i: (i, 0))
    return pl.pallas_call(
        functools.partial(_ffn_kernel, final_norm=final_norm),
        grid=(n // tm,),
        out_shape=jax.ShapeDtypeStruct((n, D_MODEL), F32),
        in_specs=[row] + _ffn_specs(wg, wu, wd, layer) + [_const_spec((1, D_MODEL))],
        out_specs=row,
        compiler_params=pltpu.CompilerParams(dimension_semantics=("parallel",), vmem_limit_bytes=VMEM_LIMIT),
        name="ffn",
    )(h, g, wg, wu, wd, gf)


def _conv_ffn_kernel(h_ref, g_ref, win_ref, cw_ref, wout_ref, zinit_ref, gf_ref, wg_ref, wu_ref, wd_ref,
                     out_ref, ztail_ref, zprev_scr, *, tm):
    @pl.when(pl.program_id(1) == 0)
    def _():
        zprev_scr[...] = zinit_ref[...]

    x = h_ref[...]
    a = _rms(x, g_ref[...]).astype(BF16)
    bcu = _dot(a, win_ref[...])
    gate_b = bcu[:, :D_MODEL]
    z = bcu[:, D_MODEL:2 * D_MODEL] * bcu[:, 2 * D_MODEL:]
    zp = zprev_scr[...]
    row = lax.broadcasted_iota(jnp.int32, (tm, D_MODEL), 0)
    last, last2 = zp[CONV_TAIL - 1:CONV_TAIL], zp[CONV_TAIL - 2:CONV_TAIL - 1]
    z1 = jnp.where(row == 0, last, pltpu.roll(z, 1, 0))
    z2 = jnp.where(row == 0, last2, jnp.where(row == 1, last, pltpu.roll(z, 2, 0)))
    cw = cw_ref[...]
    conv = cw[0:1] * z2 + cw[1:2] * z1 + cw[2:3] * z
    y = x + _dot((gate_b * conv).astype(BF16), wout_ref[...])
    out_ref[...] = _ffn_math(y, gf_ref, wg_ref, wu_ref, wd_ref)
    tail = z[tm - CONV_TAIL:]
    zprev_scr[...] = tail
    ztail_ref[...] = tail


def _conv_ffn_call(h, g, win, cw, wout, zinit, gf, wg, wu, wd, layer, batch, tm):
    n = h.shape[0]
    nt = n // batch // tm
    row = pl.BlockSpec((tm, D_MODEL), lambda b, i: (b * nt + i, 0))
    tail_shape = (CONV_TAIL, D_MODEL)
    return pl.pallas_call(
        functools.partial(_conv_ffn_kernel, tm=tm),
        grid=(batch, nt),
        out_shape=(jax.ShapeDtypeStruct((n, D_MODEL), F32), jax.ShapeDtypeStruct(tail_shape, F32)),
        in_specs=[row, _const_spec((1, D_MODEL)), _const_spec(win.shape), _const_spec(cw.shape),
                  _const_spec(wout.shape), _const_spec(tail_shape)] + _ffn_specs(wg, wu, wd, layer),
        out_specs=(row, pl.BlockSpec(tail_shape, lambda b, i: (0, 0))),
        scratch_shapes=[pltpu.VMEM(tail_shape, F32)],
        compiler_params=pltpu.CompilerParams(dimension_semantics=("arbitrary", "arbitrary"),
                                             vmem_limit_bytes=VMEM_LIMIT),
        name="conv_ffn",
    )(h, g, win, cw, wout, zinit, gf, wg, wu, wd)


def _pool_ffn_kernel(h_ref, g_ref, wp_ref, sc_ref, hinit_ref, gf_ref, wg_ref, wu_ref, wd_ref,
                     out_ref, atail_ref, hist_scr, *, tm, from_start):
    @pl.when(pl.program_id(1) == 0)
    def _():
        hist_scr[...] = hinit_ref[...]

    x = h_ref[...]
    a = _rms(x, g_ref[...])
    ext = jnp.concatenate([hist_scr[...], a], axis=0)
    outs = []
    for gi, win in enumerate(POOL_WINDOWS):
        cols = slice(gi * POOL_GROUP_DIM, (gi + 1) * POOL_GROUP_DIM)
        s = ext[:, cols]
        k = 1
        while k < win:
            s = s + pltpu.roll(s, k, 0)
            k *= 2
        s = s[POOL_HIST:]
        if from_start:
            pos = lax.broadcasted_iota(jnp.int32, (tm, POOL_GROUP_DIM), 0)
            mean = s / jnp.minimum(win, pos + 1).astype(F32)
        else:
            mean = s * (1.0 / win)
        mix = (mean - a[:, cols]).astype(BF16)
        outs.append(_dot(mix, wp_ref[gi]))
    y = x + jnp.concatenate(outs, axis=-1) * sc_ref[...]
    out_ref[...] = _ffn_math(y, gf_ref, wg_ref, wu_ref, wd_ref)
    tail = a[tm - POOL_HIST:]
    hist_scr[...] = tail
    atail_ref[...] = tail


def _pool_ffn_call(h, g, wp, sc, hinit, gf, wg, wu, wd, layer, batch, tm, from_start):
    n = h.shape[0]
    nt = n // batch // tm
    row = pl.BlockSpec((tm, D_MODEL), lambda b, i: (b * nt + i, 0))
    hist_shape = (POOL_HIST, D_MODEL)
    return pl.pallas_call(
        functools.partial(_pool_ffn_kernel, tm=tm, from_start=from_start),
        grid=(batch, nt),
        out_shape=(jax.ShapeDtypeStruct((n, D_MODEL), F32), jax.ShapeDtypeStruct(hist_shape, F32)),
        in_specs=[row, _const_spec((1, D_MODEL)), _const_spec(wp.shape), _const_spec((1, D_MODEL)),
                  _const_spec(hist_shape)] + _ffn_specs(wg, wu, wd, layer),
        out_specs=(row, pl.BlockSpec(hist_shape, lambda b, i: (0, 0))),
        scratch_shapes=[pltpu.VMEM(hist_shape, F32)],
        compiler_params=pltpu.CompilerParams(dimension_semantics=("arbitrary", "arbitrary"),
                                             vmem_limit_bytes=VMEM_LIMIT),
        name="pool_ffn",
    )(h, g, wp, sc, hinit, gf, wg, wu, wd)


ATTN_TM = 512
ATTN_LOOKAHEAD = 2
ROW_TM = 1024
CONV_TM = 512


def kernel(x, meta_tokens, rel_bias_table, norm_mix, norm_ffn, norm_final, attn_w_qkv, attn_b_qkv, attn_w_o,
           attn_b_o, attn_sinks, conv_w_in, conv_w, conv_w_out, pool_w, pool_scale, ffn_w_gate, ffn_w_up,
           ffn_w_down):
    batch, seq, _ = x.shape
    depth = norm_mix.shape[0]
    n_mixers = 3
    hr = x.reshape(batch * seq, D_MODEL)
    hm = meta_tokens.astype(F32)
    row = lambda v: v.reshape(1, -1)
    meta_codes = _meta_codes()
    wg, wu, wd = ffn_w_gate.astype(BF16), ffn_w_up.astype(BF16), ffn_w_down.astype(BF16)
    for i in range(depth):
        kind, j = i % n_mixers, i // n_mixers
        last = i == depth - 1
        gm, gf = row(norm_mix[i]), row(norm_ffn[i])
        if kind == 0:
            w, b = attn_w_qkv[j], attn_b_qkv[j]
            wqvT = jnp.concatenate([w[:, :Q_DIM], w[:, Q_DIM + KV_DIM:]], axis=1).T.astype(BF16)
            bqvT = jnp.broadcast_to(jnp.concatenate([b[:Q_DIM], b[Q_DIM + KV_DIM:]])[:, None], (Q_DIM + KV_DIM, LANES))
            wk, bk = w[:, Q_DIM:Q_DIM + KV_DIM].astype(BF16), row(b[Q_DIM:Q_DIM + KV_DIM])
            wo, bo = attn_w_o[j].astype(BF16), row(attn_b_o[j])
            tbl_meta = _meta_table(meta_codes, rel_bias_table, attn_sinks[j])
            tbl_real = _real_table(rel_bias_table, attn_sinks[j])
            hm, km, vmT = _attn_meta_call(hm, gm, wqvT, bqvT, wk, row(b), tbl_meta, wo, bo)
            qT, k, vT = _qkv_call(hr, gm, wqvT, bqvT, wk, bk, ROW_TM)
            hr = _attn_core_call(qT, k, vT, km, vmT, tbl_real, hr, wo, bo, batch, ATTN_TM)
        elif kind == 1:
            win, wout = conv_w_in[j].astype(BF16), conv_w_out[j].astype(BF16)
            hm, ztail = _conv_ffn_call(hm, gm, win, conv_w[j], wout, jnp.zeros((CONV_TAIL, D_MODEL), F32),
                                       gf, wg, wu, wd, i, 1, N_META)
            hr, _ = _conv_ffn_call(hr, gm, win, conv_w[j], wout, ztail, gf, wg, wu, wd, i, batch, CONV_TM)
            continue
        else:
            wp, sc = pool_w[j].astype(BF16), row(pool_scale[j])
            hm, atail = _pool_ffn_call(hm, gm, wp, sc, jnp.zeros((POOL_HIST, D_MODEL), F32),
                                       gf, wg, wu, wd, i, 1, N_META, True)
            hr, _ = _pool_ffn_call(hr, gm, wp, sc, atail, gf, wg, wu, wd, i, batch, ROW_TM, False)
            continue
        if not last:
            hm = _ffn_call(hm, gf, wg, wu, wd, i, row(norm_final), N_META, False)
        hr = _ffn_call(hr, gf, wg, wu, wd, i, row(norm_final), ROW_TM, last)
    return hr.reshape(batch, seq, D_MODEL)
```

```python
import functools
import math

import numpy as np
import jax
import jax.numpy as jnp
from jax import lax
from jax.experimental import pallas as pl
from jax.experimental.pallas import tpu as pltpu

F32 = jnp.float32
BF16 = jnp.bfloat16

D_MODEL = 1024
N_META = 16
RMS_EPS = 1e-6
N_HEADS = 16
N_KV = 4
HEAD_DIM = 64
GROUP = N_HEADS // N_KV
Q_DIM = N_HEADS * HEAD_DIM
KV_DIM = N_KV * HEAD_DIM
WINDOW = 128
N_BUCKETS = 32
MAX_DISTANCE = 128
POOL_WINDOWS = (2, 4, 8, 16)
POOL_GROUP_DIM = D_MODEL // len(POOL_WINDOWS)
POOL_HIST = 16
CONV_TAIL = 8
LANES = 128
FFN_LEAD_ROWS = 256
LOG2E = math.log2(math.e)

Q_BLK = 128
BAND_ROWS = WINDOW + Q_BLK
SINK_ROW = BAND_ROWS + N_META
KEY_ROWS = SINK_ROW + N_META
N_NEAR = WINDOW // Q_BLK
ONES_ROWS = 16

META_KEY_COLS = 128
CODE_SINK = N_BUCKETS
CODE_MASKED = N_BUCKETS + 1

VMEM_LIMIT = 56 * 1024 * 1024


def _rel_bucket_np(dist):
    max_exact = N_BUCKETS // 2
    d = np.maximum(dist, 0)
    df = np.maximum(d, 1).astype(np.float64)
    large = max_exact + (np.log(df / max_exact) / math.log(MAX_DISTANCE / max_exact)
                         * (N_BUCKETS - max_exact)).astype(np.int64)
    large = np.minimum(large, N_BUCKETS - 1)
    return np.where(d < max_exact, d, large).astype(np.int32)


def _meta_codes():
    i = np.arange(N_META)[:, None]
    m = np.arange(N_META)[None, :]
    code = np.full((N_META, META_KEY_COLS), CODE_MASKED, np.int32)
    code[:, :N_META] = np.where(i >= m, _rel_bucket_np(i - m), CODE_MASKED)
    code[:, N_META] = CODE_SINK
    return code[None]


def _band_codes():
    return np.stack([_rel_bucket_np(np.arange(WINDOW)), np.zeros(WINDOW, np.int32)])


def _near_meta_codes():
    m = np.arange(N_META)[:, None]
    i = np.arange(Q_BLK)[None, :]
    return np.stack([_rel_bucket_np(N_META + n * Q_BLK + i - m) for n in range(N_NEAR)])


def _const_spec(shape):
    zeros = (0,) * len(shape)
    return pl.BlockSpec(shape, lambda *_: zeros, pipeline_mode=pl.Buffered(1))


def _layer_spec(stacked_shape, layer):
    index = (layer,) + (0,) * (len(stacked_shape) - 1)
    return pl.BlockSpec((None,) + tuple(stacked_shape[1:]), lambda *_: index, pipeline_mode=pl.Buffered(1))


def _smem_spec():
    return pl.BlockSpec(memory_space=pltpu.SMEM)


def _rms(x, g):
    return x * lax.rsqrt(jnp.mean(x * x, axis=-1, keepdims=True) + RMS_EPS) * g


def _dot(a, b):
    return jnp.dot(a, b, preferred_element_type=F32)


def _dot_nt(a, b):
    return lax.dot_general(a, b, (((1,), (1,)), ((), ())), preferred_element_type=F32)


def _dot_tn(a, b):
    return lax.dot_general(a, b, (((0,), (0,)), ((), ())), preferred_element_type=F32)


def _select_by_code(code, tbl_ref, h, otherwise):
    val = otherwise
    for b in range(N_BUCKETS):
        val = jnp.where(code == b, tbl_ref[b, h], val)
    return val


def _meta_table_kernel(code_ref, tbl_ref, sink_ref, out_ref):
    n_var, rows, _ = code_ref.shape
    for v in range(n_var):
        code = code_ref[v]
        for h in range(N_HEADS):
            kh, g = divmod(h, GROUP)
            val = _select_by_code(code, tbl_ref, h, jnp.where(code == CODE_SINK, sink_ref[h], -jnp.inf))
            out_ref[v, kh, g * rows:(g + 1) * rows, :] = val


def _meta_table(codes, rel_table, sinks):
    n_var, rows, cols = codes.shape
    return pl.pallas_call(
        _meta_table_kernel,
        out_shape=jax.ShapeDtypeStruct((n_var, N_KV, GROUP * rows, cols), F32),
        in_specs=[pl.BlockSpec(memory_space=pltpu.VMEM), _smem_spec(), _smem_spec()],
        out_specs=pl.BlockSpec(memory_space=pltpu.VMEM),
        name="meta_bias_table",
    )(jnp.asarray(codes), rel_table, sinks)


def _real_table_kernel(band_code_ref, meta_code_ref, tbl_ref, sink_ref, out_ref):
    neg = -jnp.inf
    band_code = band_code_ref[0:1, :]
    t = lax.broadcasted_iota(jnp.int32, (BAND_ROWS, WINDOW), 0)
    i = lax.broadcasted_iota(jnp.int32, (BAND_ROWS, WINDOW), 1)
    in_window = (i < t) & (t <= i + WINDOW)
    row16 = lax.broadcasted_iota(jnp.int32, (N_META, Q_BLK), 0)
    for h in range(N_HEADS):
        kh, g = divmod(h, GROUP)
        lanes = slice(g * Q_BLK, (g + 1) * Q_BLK)
        by_dist = _select_by_code(band_code, tbl_ref, h, jnp.zeros((1, WINDOW), F32)) * LOG2E
        rolled = pltpu.roll(jnp.broadcast_to(by_dist, (WINDOW, WINDOW)), 0, 1, stride=1, stride_axis=0)
        rolled = jnp.concatenate([rolled, rolled[:BAND_ROWS - WINDOW]], axis=0)
        far = jnp.full((N_META, Q_BLK), tbl_ref[N_BUCKETS - 1, h], F32)
        sink_pad = jnp.where(row16 == 0, sink_ref[h] * LOG2E, neg)
        for v in range(1 + N_NEAR):
            if v == 0:
                valid, meta = in_window, far * LOG2E
            else:
                first_key_row = WINDOW - (v - 1) * Q_BLK
                valid = in_window & (t >= first_key_row)
                meta = _select_by_code(meta_code_ref[v - 1], tbl_ref, h, far) * LOG2E
            out_ref[v, kh, 0:BAND_ROWS, lanes] = jnp.where(valid, rolled, neg)[:, :Q_BLK]
            out_ref[v, kh, BAND_ROWS:SINK_ROW, lanes] = meta
            out_ref[v, kh, SINK_ROW:KEY_ROWS, lanes] = sink_pad


def _real_table(rel_table, sinks):
    vmem = pl.BlockSpec(memory_space=pltpu.VMEM)
    return pl.pallas_call(
        _real_table_kernel,
        out_shape=jax.ShapeDtypeStruct((1 + N_NEAR, N_KV, KEY_ROWS, GROUP * Q_BLK), F32),
        in_specs=[vmem, vmem, _smem_spec(), _smem_spec()],
        out_specs=vmem,
        name="real_bias_table",
    )(jnp.asarray(_band_codes()), jnp.asarray(_near_meta_codes()), rel_table, sinks)


def _qkv_kernel(h_ref, g_ref, wqv_ref, bqv_ref, wk_ref, bk_ref, qT_ref, k_ref, vT_ref, *, tm, blk):
    a = _rms(h_ref[...], g_ref[...]).astype(BF16)
    bias = jnp.concatenate([bqv_ref[...]] * (tm // LANES), axis=1)
    qvT = _dot_nt(wqv_ref[...], a) + bias
    qT = (qvT[:Q_DIM] * (HEAD_DIM ** -0.5 * LOG2E)).astype(BF16)
    vT = qvT[Q_DIM:].astype(BF16)
    for c in range(tm // blk):
        qT_ref[c] = qT[:, c * blk:(c + 1) * blk]
        vT_ref[c] = vT[:, c * blk:(c + 1) * blk]
    k_ref[...] = (_dot(a, wk_ref[...]) + bk_ref[...]).astype(BF16)


def _qkv_call(h, g, wqvT, bqvT, wk, bk, tm, blk):
    n = h.shape[0]
    row = lambda c: pl.BlockSpec((tm, c), lambda i: (i, 0))
    col = lambda r: pl.BlockSpec((tm // blk, r, blk), lambda i: (i, 0, 0))
    return pl.pallas_call(
        functools.partial(_qkv_kernel, tm=tm, blk=blk),
        grid=(n // tm,),
        out_shape=(jax.ShapeDtypeStruct((n // blk, Q_DIM, blk), BF16),
                   jax.ShapeDtypeStruct((n, KV_DIM), BF16),
                   jax.ShapeDtypeStruct((n // blk, KV_DIM, blk), BF16)),
        in_specs=[row(D_MODEL), _const_spec((1, D_MODEL)), _const_spec(wqvT.shape), _const_spec(bqvT.shape),
                  _const_spec(wk.shape), _const_spec((1, KV_DIM))],
        out_specs=(col(Q_DIM), row(KV_DIM), col(KV_DIM)),
        compiler_params=pltpu.CompilerParams(dimension_semantics=("parallel",), vmem_limit_bytes=VMEM_LIMIT),
        name="attn_qkv",
    )(h, g, wqvT, bqvT, wk, bk)


def _attn_core_kernel(qT_ref, kp_ref, kc_ref, vTp_ref, vTc_ref, km_ref, vmT_ref, tbl_ref, h_ref, wo_ref, bo_ref,
                      out_ref, oT_scr, *, tm):
    first_tile = pl.program_id(1) == 0
    kext = jnp.concatenate([kp_ref[...], kc_ref[...]], axis=0)
    vText = jnp.concatenate([vTp_ref[...], vTc_ref[...]], axis=1)
    km = km_ref[...]
    vmT = vmT_ref[...]
    zk = jnp.zeros((KEY_ROWS - SINK_ROW, HEAD_DIM), BF16)
    ones = jnp.ones((ONES_ROWS, KEY_ROWS), BF16)

    def scores(j, kh):
        r0 = j * Q_BLK
        cols = slice(kh * HEAD_DIM, (kh + 1) * HEAD_DIM)
        qsT = jnp.concatenate(
            [qT_ref[(kh * GROUP + g) * HEAD_DIM:(kh * GROUP + g + 1) * HEAD_DIM, r0:r0 + Q_BLK]
             for g in range(GROUP)], axis=1)
        keys = jnp.concatenate([kext[r0:r0 + BAND_ROWS, cols], km[:, cols], zk], axis=0)
        return _dot(keys, qsT)

    def finish(j, kh, sT, sT_next):
        r0 = j * Q_BLK
        cols = slice(kh * HEAD_DIM, (kh + 1) * HEAD_DIM)
        var = jnp.where(first_tile, 1 + j, 0) if j < N_NEAR else 0
        sT = sT + tbl_ref[var, kh]
        m = jnp.max(sT, axis=0, keepdims=True)
        pT = jnp.exp2(sT - m).astype(BF16)
        valsT = jnp.concatenate([vText[cols, r0:r0 + BAND_ROWS], vmT[cols, :]], axis=1)
        den = ones
        if sT_next is not None:
            bits = pltpu.bitcast(sT_next[KEY_ROWS - ONES_ROWS:, :LANES], jnp.uint32)
            zero = pltpu.bitcast((bits >> 16) >> 16, F32).astype(BF16)
            den = jnp.concatenate([ones[:, :LANES] + zero, ones[:, LANES:]], axis=1)
        oT = _dot(jnp.concatenate([valsT, den], axis=0), pT)
        oT = oT[:HEAD_DIM] / oT[HEAD_DIM:HEAD_DIM + 1]
        for g in range(GROUP):
            h0 = (kh * GROUP + g) * HEAD_DIM
            oT_scr[h0:h0 + HEAD_DIM, r0:r0 + Q_BLK] = oT[:, g * Q_BLK:(g + 1) * Q_BLK].astype(BF16)

    units = [(j, kh) for j in range(tm // Q_BLK) for kh in range(N_KV)]
    pending = [scores(*u) for u in units[:ATTN_LOOKAHEAD]]
    for idx, unit in enumerate(units):
        ahead = idx + ATTN_LOOKAHEAD
        if ahead < len(units):
            pending.append(scores(*units[ahead]))
        sT = pending.pop(0)
        finish(*unit, sT, pending[-1] if pending else None)
    out_ref[...] = h_ref[...] + _dot_tn(oT_scr[...], wo_ref[...]) + bo_ref[...]


def _attn_core_call(qT, k, vT, km, vmT, tbl, h, wo, bo, batch, tm):
    n = h.shape[0]
    nt = n // batch // tm
    per = tm // WINDOW
    prev_idx = lambda b, i: jnp.maximum((b * nt + i) * per - 1, 0)
    prev_tile = lambda b, i: jnp.maximum(b * nt + i - 1, 0)
    row = lambda c: pl.BlockSpec((tm, c), lambda b, i: (b * nt + i, 0))
    col = lambda r: pl.BlockSpec((None, r, tm), lambda b, i: (b * nt + i, 0, 0))
    return pl.pallas_call(
        functools.partial(_attn_core_kernel, tm=tm),
        grid=(batch, nt),
        out_shape=jax.ShapeDtypeStruct((n, D_MODEL), F32),
        in_specs=[col(Q_DIM),
                  pl.BlockSpec((WINDOW, KV_DIM), lambda b, i: (prev_idx(b, i), 0)), row(KV_DIM),
                  pl.BlockSpec((None, KV_DIM, WINDOW), lambda b, i: (prev_tile(b, i), 0, per - 1)), col(KV_DIM),
                  _const_spec(km.shape), _const_spec(vmT.shape), _const_spec(tbl.shape),
                  row(D_MODEL), _const_spec(wo.shape), _const_spec((1, D_MODEL))],
        out_specs=row(D_MODEL),
        scratch_shapes=[pltpu.VMEM((Q_DIM, tm), BF16)],
        compiler_params=pltpu.CompilerParams(dimension_semantics=("parallel", "parallel"),
                                             vmem_limit_bytes=VMEM_LIMIT),
        name="attn_core",
    )(qT, k, k, vT, vT, km, vmT, tbl, h, wo, bo)


def _attn_meta_kernel(h_ref, g_ref, wqv_ref, bqv_ref, wk_ref, b_ref, tbl_ref, wo_ref, bo_ref,
                      out_ref, km_ref, vmT_ref):
    x = h_ref[...]
    a = _rms(x, g_ref[...]).astype(BF16)
    a2 = jnp.concatenate([a, jnp.zeros_like(a)], axis=0)
    lane = lax.broadcasted_iota(jnp.int32, (KV_DIM, 2 * N_META), 1)
    vT = _dot_nt(wqv_ref[Q_DIM:, :], a2) + jnp.where(lane < N_META, bqv_ref[Q_DIM:, :2 * N_META], 0.0)
    vmT_ref[...] = vT.astype(BF16)
    b = b_ref[...]
    q = ((_dot_nt(a, wqv_ref[:Q_DIM, :]) + b[:, :Q_DIM]) * (HEAD_DIM ** -0.5)).astype(BF16)
    k = (_dot(a, wk_ref[...]) + b[:, Q_DIM:Q_DIM + KV_DIM]).astype(BF16)
    v = (_dot_nt(a, wqv_ref[Q_DIM:, :]) + b[:, Q_DIM + KV_DIM:]).astype(BF16)
    km_ref[...] = k
    pad = META_KEY_COLS - N_META
    vcat = jnp.concatenate([v, jnp.zeros((pad, KV_DIM), BF16)], axis=0)
    pieces = []
    for kh in range(N_KV):
        cols = slice(kh * HEAD_DIM, (kh + 1) * HEAD_DIM)
        qs = jnp.concatenate([q[:, (kh * GROUP + g) * HEAD_DIM:(kh * GROUP + g + 1) * HEAD_DIM]
                              for g in range(GROUP)], axis=0)
        kcat = jnp.concatenate([k[:, cols], jnp.zeros((pad, HEAD_DIM), BF16)], axis=0)
        s = _dot_nt(qs, kcat) + tbl_ref[0, kh]
        p = jnp.exp(s - jnp.max(s, axis=-1, keepdims=True))
        l = jnp.sum(p, axis=-1, keepdims=True)
        o = _dot(p.astype(BF16), vcat)[:, cols] / l
        pieces += [o[g * N_META:(g + 1) * N_META] for g in range(GROUP)]
    o_all = jnp.concatenate(pieces, axis=-1).astype(BF16)
    out_ref[...] = x + _dot(o_all, wo_ref[...]) + bo_ref[...]


def _attn_meta_call(h, g, wqvT, bqvT, wk, bk, tbl, wo, bo):
    vmem = pl.BlockSpec(memory_space=pltpu.VMEM)
    return pl.pallas_call(
        _attn_meta_kernel,
        out_shape=(jax.ShapeDtypeStruct((N_META, D_MODEL), F32),
                   jax.ShapeDtypeStruct((N_META, KV_DIM), BF16),
                   jax.ShapeDtypeStruct((KV_DIM, 2 * N_META), BF16)),
        in_specs=[vmem] * 9,
        out_specs=(vmem, vmem, vmem),
        compiler_params=pltpu.CompilerParams(vmem_limit_bytes=VMEM_LIMIT),
        name="attn_meta",
    )(h, g, wqvT, bqvT, wk, bk, tbl, wo, bo)


def _ffn_math(x, g_ref, wg_ref, wu_ref, wd_ref):
    rows = x.shape[0]
    lead = FFN_LEAD_ROWS if rows > 2 * FFN_LEAD_ROWS else rows
    chunks = [x[:lead]] + ([x[lead:]] if lead < rows else [])
    hmids = []
    for xc in chunks:
        a = _rms(xc, g_ref[...]).astype(BF16)
        gate = _dot(a, wg_ref[...])
        up = _dot(a, wu_ref[...])
        hmids.append((gate * (1.0 / (1.0 + jnp.exp(-gate))) * up).astype(BF16))
    outs = [xc + _dot(hm, wd_ref[...]) for xc, hm in zip(chunks, hmids)]
    return outs[0] if len(outs) == 1 else jnp.concatenate(outs, axis=0)


def _ffn_kernel(h_ref, g_ref, wg_ref, wu_ref, wd_ref, gf_ref, out_ref, *, final_norm):
    y = _ffn_math(h_ref[...], g_ref, wg_ref, wu_ref, wd_ref)
    out_ref[...] = _rms(y, gf_ref[...]) if final_norm else y


def _ffn_specs(wg, wu, wd, layer):
    return [_const_spec((1, D_MODEL)), _layer_spec(wg.shape, layer), _layer_spec(wu.shape, layer),
            _layer_spec(wd.shape, layer)]


def _ffn_call(h, g, wg, wu, wd, layer, gf, tm, final_norm):
    n = h.shape[0]
    row = pl.BlockSpec((tm, D_MODEL), lambda i: (i, 0))
    return pl.pallas_call(
        functools.partial(_ffn_kernel, final_norm=final_norm),
        grid=(n // tm,),
        out_shape=jax.ShapeDtypeStruct((n, D_MODEL), F32),
        in_specs=[row] + _ffn_specs(wg, wu, wd, layer) + [_const_spec((1, D_MODEL))],
        out_specs=row,
        compiler_params=pltpu.CompilerParams(dimension_semantics=("parallel",), vmem_limit_bytes=VMEM_LIMIT),
        name="ffn",
    )(h, g, wg, wu, wd, gf)


def _conv_ffn_kernel(h_ref, g_ref, win_ref, cw_ref, wout_ref, zinit_ref, gf_ref, wg_ref, wu_ref, wd_ref,
                     out_ref, ztail_ref, zprev_scr, *, tm):
    @pl.when(pl.program_id(1) == 0)
    def _():
        zprev_scr[...] = zinit_ref[...]

    x = h_ref[...]
    a = _rms(x, g_ref[...]).astype(BF16)
    bcu = _dot(a, win_ref[...])
    gate_b = bcu[:, :D_MODEL]
    z = bcu[:, D_MODEL:2 * D_MODEL] * bcu[:, 2 * D_MODEL:]
    zp = zprev_scr[...]
    row = lax.broadcasted_iota(jnp.int32, (tm, D_MODEL), 0)
    last, last2 = zp[CONV_TAIL - 1:CONV_TAIL], zp[CONV_TAIL - 2:CONV_TAIL - 1]
    z1 = jnp.where(row == 0, last, pltpu.roll(z, 1, 0))
    z2 = jnp.where(row == 0, last2, jnp.where(row == 1, last, pltpu.roll(z, 2, 0)))
    cw = cw_ref[...]
    conv = cw[0:1] * z2 + cw[1:2] * z1 + cw[2:3] * z
    y = x + _dot((gate_b * conv).astype(BF16), wout_ref[...])
    out_ref[...] = _ffn_math(y, gf_ref, wg_ref, wu_ref, wd_ref)
    tail = z[tm - CONV_TAIL:]
    zprev_scr[...] = tail
    ztail_ref[...] = tail


def _conv_ffn_call(h, g, win, cw, wout, zinit, gf, wg, wu, wd, layer, batch, tm):
    n = h.shape[0]
    nt = n // batch // tm
    row = pl.BlockSpec((tm, D_MODEL), lambda b, i: (b * nt + i, 0))
    tail_shape = (CONV_TAIL, D_MODEL)
    return pl.pallas_call(
        functools.partial(_conv_ffn_kernel, tm=tm),
        grid=(batch, nt),
        out_shape=(jax.ShapeDtypeStruct((n, D_MODEL), F32), jax.ShapeDtypeStruct(tail_shape, F32)),
        in_specs=[row, _const_spec((1, D_MODEL)), _const_spec(win.shape), _const_spec(cw.shape),
                  _const_spec(wout.shape), _const_spec(tail_shape)] + _ffn_specs(wg, wu, wd, layer),
        out_specs=(row, pl.BlockSpec(tail_shape, lambda b, i: (0, 0))),
        scratch_shapes=[pltpu.VMEM(tail_shape, F32)],
        compiler_params=pltpu.CompilerParams(dimension_semantics=("arbitrary", "arbitrary"),
                                             vmem_limit_bytes=VMEM_LIMIT),
        name="conv_ffn",
    )(h, g, win, cw, wout, zinit, gf, wg, wu, wd)


def _pool_ffn_kernel(h_ref, g_ref, wp_ref, sc_ref, hinit_ref, gf_ref, wg_ref, wu_ref, wd_ref,
                     out_ref, atail_ref, hist_scr, *, tm, from_start):
    @pl.when(pl.program_id(1) == 0)
    def _():
        hist_scr[...] = hinit_ref[...]

    x = h_ref[...]
    a = _rms(x, g_ref[...])
    ext = jnp.concatenate([hist_scr[...], a], axis=0)
    outs = []
    for gi, win in enumerate(POOL_WINDOWS):
        cols = slice(gi * POOL_GROUP_DIM, (gi + 1) * POOL_GROUP_DIM)
        s = ext[:, cols]
        k = 1
        while k < win:
            s = s + pltpu.roll(s, k, 0)
            k *= 2
        s = s[POOL_HIST:]
        if from_start:
            pos = lax.broadcasted_iota(jnp.int32, (tm, POOL_GROUP_DIM), 0)
            mean = s / jnp.minimum(win, pos + 1).astype(F32)
        else:
            mean = s * (1.0 / win)
        mix = (mean - a[:, cols]).astype(BF16)
        outs.append(_dot(mix, wp_ref[gi]))
    y = x + jnp.concatenate(outs, axis=-1) * sc_ref[...]
    out_ref[...] = _ffn_math(y, gf_ref, wg_ref, wu_ref, wd_ref)
    tail = a[tm - POOL_HIST:]
    hist_scr[...] = tail
    atail_ref[...] = tail


def _pool_ffn_call(h, g, wp, sc, hinit, gf, wg, wu, wd, layer, batch, tm, from_start):
    n = h.shape[0]
    nt = n // batch // tm
    row = pl.BlockSpec((tm, D_MODEL), lambda b, i: (b * nt + i, 0))
    hist_shape = (POOL_HIST, D_MODEL)
    return pl.pallas_call(
        functools.partial(_pool_ffn_kernel, tm=tm, from_start=from_start),
        grid=(batch, nt),
        out_shape=(jax.ShapeDtypeStruct((n, D_MODEL), F32), jax.ShapeDtypeStruct(hist_shape, F32)),
        in_specs=[row, _const_spec((1, D_MODEL)), _const_spec(wp.shape), _const_spec((1, D_MODEL)),
                  _const_spec(hist_shape)] + _ffn_specs(wg, wu, wd, layer),
        out_specs=(row, pl.BlockSpec(hist_shape, lambda b, i: (0, 0))),
        scratch_shapes=[pltpu.VMEM(hist_shape, F32)],
        compiler_params=pltpu.CompilerParams(dimension_semantics=("arbitrary", "arbitrary"),
                                             vmem_limit_bytes=VMEM_LIMIT),
        name="pool_ffn",
    )(h, g, wp, sc, hinit, gf, wg, wu, wd)


ATTN_TM = 512
ATTN_LOOKAHEAD = 2
ROW_TM = 1024
CONV_TM = 512


def kernel(x, meta_tokens, rel_bias_table, norm_mix, norm_ffn, norm_final, attn_w_qkv, attn_b_qkv, attn_w_o,
           attn_b_o, attn_sinks, conv_w_in, conv_w, conv_w_out, pool_w, pool_scale, ffn_w_gate, ffn_w_up,
           ffn_w_down):
    batch, seq, _ = x.shape
    depth = norm_mix.shape[0]
    n_mixers = 3
    hr = x.reshape(batch * seq, D_MODEL)
    hm = meta_tokens.astype(F32)
    row = lambda v: v.reshape(1, -1)
    meta_codes = _meta_codes()
    wg, wu, wd = ffn_w_gate.astype(BF16), ffn_w_up.astype(BF16), ffn_w_down.astype(BF16)
    for i in range(depth):
        kind, j = i % n_mixers, i // n_mixers
        last = i == depth - 1
        gm, gf = row(norm_mix[i]), row(norm_ffn[i])
        if kind == 0:
            w, b = attn_w_qkv[j], attn_b_qkv[j]
            wqvT = jnp.concatenate([w[:, :Q_DIM], w[:, Q_DIM + KV_DIM:]], axis=1).T.astype(BF16)
            bqvT = jnp.broadcast_to(jnp.concatenate([b[:Q_DIM], b[Q_DIM + KV_DIM:]])[:, None], (Q_DIM + KV_DIM, LANES))
            wk, bk = w[:, Q_DIM:Q_DIM + KV_DIM].astype(BF16), row(b[Q_DIM:Q_DIM + KV_DIM])
            wo, bo = attn_w_o[j].astype(BF16), row(attn_b_o[j])
            tbl_meta = _meta_table(meta_codes, rel_bias_table, attn_sinks[j])
            tbl_real = _real_table(rel_bias_table, attn_sinks[j])
            hm, km, vmT = _attn_meta_call(hm, gm, wqvT, bqvT, wk, row(b), tbl_meta, wo, bo)
            qT, k, vT = _qkv_call(hr, gm, wqvT, bqvT, wk, bk, ROW_TM, ATTN_TM)
            hr = _attn_core_call(qT, k, vT, km, vmT, tbl_real, hr, wo, bo, batch, ATTN_TM)
        elif kind == 1:
            win, wout = conv_w_in[j].astype(BF16), conv_w_out[j].astype(BF16)
            hm, ztail = _conv_ffn_call(hm, gm, win, conv_w[j], wout, jnp.zeros((CONV_TAIL, D_MODEL), F32),
                                       gf, wg, wu, wd, i, 1, N_META)
            hr, _ = _conv_ffn_call(hr, gm, win, conv_w[j], wout, ztail, gf, wg, wu, wd, i, batch, CONV_TM)
            continue
        else:
            wp, sc = pool_w[j].astype(BF16), row(pool_scale[j])
            hm, atail = _pool_ffn_call(hm, gm, wp, sc, jnp.zeros((POOL_HIST, D_MODEL), F32),
                                       gf, wg, wu, wd, i, 1, N_META, True)
            hr, _ = _pool_ffn_call(hr, gm, wp, sc, atail, gf, wg, wu, wd, i, batch, ROW_TM, False)
            continue
        if not last:
            hm = _ffn_call(hm, gf, wg, wu, wd, i, row(norm_final), N_META, False)
        hr = _ffn_call(hr, gf, wg, wu, wd, i, row(norm_final), ROW_TM, last)
    return hr.reshape(batch, seq, D_MODEL)
```

```python
import functools
import math

import numpy as np
import jax
import jax.numpy as jnp
from jax import lax
from jax.experimental import pallas as pl
from jax.experimental.pallas import tpu as pltpu

F32 = jnp.float32
BF16 = jnp.bfloat16

D_MODEL = 1024
N_META = 16
RMS_EPS = 1e-6
N_HEADS = 16
N_KV = 4
HEAD_DIM = 64
GROUP = N_HEADS // N_KV
Q_DIM = N_HEADS * HEAD_DIM
KV_DIM = N_KV * HEAD_DIM
WINDOW = 128
N_BUCKETS = 32
MAX_DISTANCE = 128
POOL_WINDOWS = (2, 4, 8, 16)
POOL_GROUP_DIM = D_MODEL // len(POOL_WINDOWS)
POOL_HIST = 16
CONV_TAIL = 8
LANES = 128
FFN_LEAD_ROWS = 256
CAST_ROWS = 512
LOG2E = math.log2(math.e)

Q_BLK = 128
BAND_ROWS = WINDOW + Q_BLK
SINK_ROW = BAND_ROWS + N_META
KEY_ROWS = SINK_ROW + N_META
N_NEAR = WINDOW // Q_BLK
ONES_ROWS = 16

META_KEY_COLS = 128
CODE_SINK = N_BUCKETS
CODE_MASKED = N_BUCKETS + 1

VMEM_LIMIT = 56 * 1024 * 1024


def _rel_bucket_np(dist):
    max_exact = N_BUCKETS // 2
    d = np.maximum(dist, 0)
    df = np.maximum(d, 1).astype(np.float64)
    large = max_exact + (np.log(df / max_exact) / math.log(MAX_DISTANCE / max_exact)
                         * (N_BUCKETS - max_exact)).astype(np.int64)
    large = np.minimum(large, N_BUCKETS - 1)
    return np.where(d < max_exact, d, large).astype(np.int32)


def _meta_codes():
    i = np.arange(N_META)[:, None]
    m = np.arange(N_META)[None, :]
    code = np.full((N_META, META_KEY_COLS), CODE_MASKED, np.int32)
    code[:, :N_META] = np.where(i >= m, _rel_bucket_np(i - m), CODE_MASKED)
    code[:, N_META] = CODE_SINK
    return code[None]


def _band_codes():
    return np.stack([_rel_bucket_np(np.arange(WINDOW)), np.zeros(WINDOW, np.int32)])


def _near_meta_codes():
    m = np.arange(N_META)[:, None]
    i = np.arange(Q_BLK)[None, :]
    return np.stack([_rel_bucket_np(N_META + n * Q_BLK + i - m) for n in range(N_NEAR)])


def _const_spec(shape):
    zeros = (0,) * len(shape)
    return pl.BlockSpec(shape, lambda *_: zeros, pipeline_mode=pl.Buffered(1))


def _cast_view(w):
    return w.reshape(w.shape[0], CAST_ROWS, -1)


def _hosting(kernel_fn, n_in, n_out, n_cast):
    def wrapped(*refs):
        ins, rest = refs[:n_in], refs[n_in:]
        cast_in, rest = rest[:n_cast], rest[n_cast:]
        outs, rest = rest[:n_out], rest[n_out:]
        cast_out, scratch = rest[:n_cast], rest[n_cast:]
        for src, dst in zip(cast_in, cast_out):
            dst[...] = src[...].astype(BF16)
        kernel_fn(*ins, *outs, *scratch)
    return wrapped


def _cast_specs(casts, steps, flat_step):
    rows = CAST_ROWS // steps
    in_specs = [pl.BlockSpec((None, rows, v.shape[2]), lambda *idx, l=l: (l, flat_step(*idx), 0)) for v, l in casts]
    out_specs = [pl.BlockSpec((rows, v.shape[2]), lambda *idx: (flat_step(*idx), 0)) for v, _ in casts]
    out_shapes = [jax.ShapeDtypeStruct(v.shape[1:], BF16) for v, _ in casts]
    return in_specs, out_specs, out_shapes


def _smem_spec():
    return pl.BlockSpec(memory_space=pltpu.SMEM)


def _rms(x, g):
    return x * lax.rsqrt(jnp.mean(x * x, axis=-1, keepdims=True) + RMS_EPS) * g


def _dot(a, b):
    return jnp.dot(a, b, preferred_element_type=F32)


def _dot_nt(a, b):
    return lax.dot_general(a, b, (((1,), (1,)), ((), ())), preferred_element_type=F32)


def _dot_tn(a, b):
    return lax.dot_general(a, b, (((0,), (0,)), ((), ())), preferred_element_type=F32)


def _select_by_code(code, tbl_ref, h, otherwise):
    val = otherwise
    for b in range(N_BUCKETS):
        val = jnp.where(code == b, tbl_ref[b, h], val)
    return val


def _meta_table_kernel(code_ref, tbl_ref, sink_ref, out_ref):
    n_var, rows, _ = code_ref.shape
    for v in range(n_var):
        code = code_ref[v]
        for h in range(N_HEADS):
            kh, g = divmod(h, GROUP)
            val = _select_by_code(code, tbl_ref, h, jnp.where(code == CODE_SINK, sink_ref[h], -jnp.inf))
            out_ref[v, kh, g * rows:(g + 1) * rows, :] = val


def _meta_table(codes, rel_table, sinks):
    n_var, rows, cols = codes.shape
    return pl.pallas_call(
        _meta_table_kernel,
        out_shape=jax.ShapeDtypeStruct((n_var, N_KV, GROUP * rows, cols), F32),
        in_specs=[pl.BlockSpec(memory_space=pltpu.VMEM), _smem_spec(), _smem_spec()],
        out_specs=pl.BlockSpec(memory_space=pltpu.VMEM),
        name="meta_bias_table",
    )(jnp.asarray(codes), rel_table, sinks)


def _real_table_kernel(band_code_ref, meta_code_ref, tbl_ref, sink_ref, out_ref):
    neg = -jnp.inf
    band_code = band_code_ref[0:1, :]
    t = lax.broadcasted_iota(jnp.int32, (BAND_ROWS, WINDOW), 0)
    i = lax.broadcasted_iota(jnp.int32, (BAND_ROWS, WINDOW), 1)
    in_window = (i < t) & (t <= i + WINDOW)
    row16 = lax.broadcasted_iota(jnp.int32, (N_META, Q_BLK), 0)
    for h in range(N_HEADS):
        kh, g = divmod(h, GROUP)
        lanes = slice(g * Q_BLK, (g + 1) * Q_BLK)
        by_dist = _select_by_code(band_code, tbl_ref, h, jnp.zeros((1, WINDOW), F32)) * LOG2E
        rolled = pltpu.roll(jnp.broadcast_to(by_dist, (WINDOW, WINDOW)), 0, 1, stride=1, stride_axis=0)
        rolled = jnp.concatenate([rolled, rolled[:BAND_ROWS - WINDOW]], axis=0)
        far = jnp.full((N_META, Q_BLK), tbl_ref[N_BUCKETS - 1, h], F32)
        sink_pad = jnp.where(row16 == 0, sink_ref[h] * LOG2E, neg)
        for v in range(1 + N_NEAR):
            if v == 0:
                valid, meta = in_window, far * LOG2E
            else:
                first_key_row = WINDOW - (v - 1) * Q_BLK
                valid = in_window & (t >= first_key_row)
                meta = _select_by_code(meta_code_ref[v - 1], tbl_ref, h, far) * LOG2E
            out_ref[v, kh, 0:BAND_ROWS, lanes] = jnp.where(valid, rolled, neg)[:, :Q_BLK]
            out_ref[v, kh, BAND_ROWS:SINK_ROW, lanes] = meta
            out_ref[v, kh, SINK_ROW:KEY_ROWS, lanes] = sink_pad


def _real_table(rel_table, sinks):
    vmem = pl.BlockSpec(memory_space=pltpu.VMEM)
    return pl.pallas_call(
        _real_table_kernel,
        out_shape=jax.ShapeDtypeStruct((1 + N_NEAR, N_KV, KEY_ROWS, GROUP * Q_BLK), F32),
        in_specs=[vmem, vmem, _smem_spec(), _smem_spec()],
        out_specs=vmem,
        name="real_bias_table",
    )(jnp.asarray(_band_codes()), jnp.asarray(_near_meta_codes()), rel_table, sinks)


def _qkv_kernel(h_ref, g_ref, wqv_ref, bqv_ref, wk_ref, bk_ref, qT_ref, k_ref, vT_ref, *, tm, blk):
    a = _rms(h_ref[...], g_ref[...]).astype(BF16)
    bias = jnp.concatenate([bqv_ref[...]] * (tm // LANES), axis=1)
    qvT = _dot_nt(wqv_ref[...], a) + bias
    qT = (qvT[:Q_DIM] * (HEAD_DIM ** -0.5 * LOG2E)).astype(BF16)
    vT = qvT[Q_DIM:].astype(BF16)
    for c in range(tm // blk):
        qT_ref[c] = qT[:, c * blk:(c + 1) * blk]
        vT_ref[c] = vT[:, c * blk:(c + 1) * blk]
    k_ref[...] = (_dot(a, wk_ref[...]) + bk_ref[...]).astype(BF16)


def _qkv_call(h, g, wqvT, bqvT, wk, bk, tm, blk):
    n = h.shape[0]
    row = lambda c: pl.BlockSpec((tm, c), lambda i: (i, 0))
    col = lambda r: pl.BlockSpec((tm // blk, r, blk), lambda i: (i, 0, 0))
    return pl.pallas_call(
        functools.partial(_qkv_kernel, tm=tm, blk=blk),
        grid=(n // tm,),
        out_shape=(jax.ShapeDtypeStruct((n // blk, Q_DIM, blk), BF16),
                   jax.ShapeDtypeStruct((n, KV_DIM), BF16),
                   jax.ShapeDtypeStruct((n // blk, KV_DIM, blk), BF16)),
        in_specs=[row(D_MODEL), _const_spec((1, D_MODEL)), _const_spec(wqvT.shape), _const_spec(bqvT.shape),
                  _const_spec(wk.shape), _const_spec((1, KV_DIM))],
        out_specs=(col(Q_DIM), row(KV_DIM), col(KV_DIM)),
        compiler_params=pltpu.CompilerParams(dimension_semantics=("parallel",), vmem_limit_bytes=VMEM_LIMIT),
        name="attn_qkv",
    )(h, g, wqvT, bqvT, wk, bk)


def _attn_core_kernel(qT_ref, kp_ref, kc_ref, vTp_ref, vTc_ref, km_ref, vmT_ref, tbl_ref, h_ref, wo_ref, bo_ref,
                      out_ref, oT_scr, *, tm):
    first_tile = pl.program_id(1) == 0
    kext = jnp.concatenate([kp_ref[...], kc_ref[...]], axis=0)
    vText = jnp.concatenate([vTp_ref[...], vTc_ref[...]], axis=1)
    km = km_ref[...]
    vmT = vmT_ref[...]
    zk = jnp.zeros((KEY_ROWS - SINK_ROW, HEAD_DIM), BF16)
    ones = jnp.ones((ONES_ROWS, KEY_ROWS), BF16)

    def scores(j, kh):
        r0 = j * Q_BLK
        cols = slice(kh * HEAD_DIM, (kh + 1) * HEAD_DIM)
        qsT = jnp.concatenate(
            [qT_ref[(kh * GROUP + g) * HEAD_DIM:(kh * GROUP + g + 1) * HEAD_DIM, r0:r0 + Q_BLK]
             for g in range(GROUP)], axis=1)
        keys = jnp.concatenate([kext[r0:r0 + BAND_ROWS, cols], km[:, cols], zk], axis=0)
        return _dot(keys, qsT)

    def finish(j, kh, sT, sT_next):
        r0 = j * Q_BLK
        cols = slice(kh * HEAD_DIM, (kh + 1) * HEAD_DIM)
        var = jnp.where(first_tile, 1 + j, 0) if j < N_NEAR else 0
        sT = sT + tbl_ref[var, kh]
        m = jnp.max(sT, axis=0, keepdims=True)
        pT = jnp.exp2(sT - m).astype(BF16)
        valsT = jnp.concatenate([vText[cols, r0:r0 + BAND_ROWS], vmT[cols, :]], axis=1)
        den = ones
        if sT_next is not None:
            bits = pltpu.bitcast(sT_next[KEY_ROWS - ONES_ROWS:, :LANES], jnp.uint32)
            zero = pltpu.bitcast((bits >> 16) >> 16, F32).astype(BF16)
            den = jnp.concatenate([ones[:, :LANES] + zero, ones[:, LANES:]], axis=1)
        oT = _dot(jnp.concatenate([valsT, den], axis=0), pT)
        oT = oT[:HEAD_DIM] / oT[HEAD_DIM:HEAD_DIM + 1]
        for g in range(GROUP):
            h0 = (kh * GROUP + g) * HEAD_DIM
            oT_scr[h0:h0 + HEAD_DIM, r0:r0 + Q_BLK] = oT[:, g * Q_BLK:(g + 1) * Q_BLK].astype(BF16)

    units = [(j, kh) for j in range(tm // Q_BLK) for kh in range(N_KV)]
    pending = [scores(*u) for u in units[:ATTN_LOOKAHEAD]]
    for idx, unit in enumerate(units):
        ahead = idx + ATTN_LOOKAHEAD
        if ahead < len(units):
            pending.append(scores(*units[ahead]))
        sT = pending.pop(0)
        finish(*unit, sT, pending[-1] if pending else None)
    out_ref[...] = h_ref[...] + _dot_tn(oT_scr[...], wo_ref[...]) + bo_ref[...]


def _attn_core_call(qT, k, vT, km, vmT, tbl, h, wo, bo, batch, tm, casts=()):
    n = h.shape[0]
    nt = n // batch // tm
    per = tm // WINDOW
    prev_idx = lambda b, i: jnp.maximum((b * nt + i) * per - 1, 0)
    prev_tile = lambda b, i: jnp.maximum(b * nt + i - 1, 0)
    row = lambda c: pl.BlockSpec((tm, c), lambda b, i: (b * nt + i, 0))
    col = lambda r: pl.BlockSpec((None, r, tm), lambda b, i: (b * nt + i, 0, 0))
    cast_in, cast_out, cast_shapes = _cast_specs(casts, batch * nt, lambda b, i: b * nt + i)
    in_specs = [col(Q_DIM),
                pl.BlockSpec((WINDOW, KV_DIM), lambda b, i: (prev_idx(b, i), 0)), row(KV_DIM),
                pl.BlockSpec((None, KV_DIM, WINDOW), lambda b, i: (prev_tile(b, i), 0, per - 1)), col(KV_DIM),
                _const_spec(km.shape), _const_spec(vmT.shape), _const_spec(tbl.shape),
                row(D_MODEL), _const_spec(wo.shape), _const_spec((1, D_MODEL))]
    outs = pl.pallas_call(
        _hosting(functools.partial(_attn_core_kernel, tm=tm), len(in_specs), 1, len(casts)),
        grid=(batch, nt),
        out_shape=[jax.ShapeDtypeStruct((n, D_MODEL), F32)] + cast_shapes,
        in_specs=in_specs + cast_in,
        out_specs=[row(D_MODEL)] + cast_out,
        scratch_shapes=[pltpu.VMEM((Q_DIM, tm), BF16)],
        compiler_params=pltpu.CompilerParams(dimension_semantics=("parallel", "parallel"),
                                             vmem_limit_bytes=VMEM_LIMIT),
        name="attn_core",
    )(qT, k, k, vT, vT, km, vmT, tbl, h, wo, bo, *[v for v, _ in casts])
    return outs[0], outs[1:]


def _attn_meta_kernel(h_ref, g_ref, wqv_ref, bqv_ref, wk_ref, b_ref, tbl_ref, wo_ref, bo_ref,
                      out_ref, km_ref, vmT_ref):
    x = h_ref[...]
    a = _rms(x, g_ref[...]).astype(BF16)
    a2 = jnp.concatenate([a, jnp.zeros_like(a)], axis=0)
    lane = lax.broadcasted_iota(jnp.int32, (KV_DIM, 2 * N_META), 1)
    vT = _dot_nt(wqv_ref[Q_DIM:, :], a2) + jnp.where(lane < N_META, bqv_ref[Q_DIM:, :2 * N_META], 0.0)
    vmT_ref[...] = vT.astype(BF16)
    b = b_ref[...]
    q = ((_dot_nt(a, wqv_ref[:Q_DIM, :]) + b[:, :Q_DIM]) * (HEAD_DIM ** -0.5)).astype(BF16)
    k = (_dot(a, wk_ref[...]) + b[:, Q_DIM:Q_DIM + KV_DIM]).astype(BF16)
    v = (_dot_nt(a, wqv_ref[Q_DIM:, :]) + b[:, Q_DIM + KV_DIM:]).astype(BF16)
    km_ref[...] = k
    pad = META_KEY_COLS - N_META
    vcat = jnp.concatenate([v, jnp.zeros((pad, KV_DIM), BF16)], axis=0)
    pieces = []
    for kh in range(N_KV):
        cols = slice(kh * HEAD_DIM, (kh + 1) * HEAD_DIM)
        qs = jnp.concatenate([q[:, (kh * GROUP + g) * HEAD_DIM:(kh * GROUP + g + 1) * HEAD_DIM]
                              for g in range(GROUP)], axis=0)
        kcat = jnp.concatenate([k[:, cols], jnp.zeros((pad, HEAD_DIM), BF16)], axis=0)
        s = _dot_nt(qs, kcat) + tbl_ref[0, kh]
        p = jnp.exp(s - jnp.max(s, axis=-1, keepdims=True))
        l = jnp.sum(p, axis=-1, keepdims=True)
        o = _dot(p.astype(BF16), vcat)[:, cols] / l
        pieces += [o[g * N_META:(g + 1) * N_META] for g in range(GROUP)]
    o_all = jnp.concatenate(pieces, axis=-1).astype(BF16)
    out_ref[...] = x + _dot(o_all, wo_ref[...]) + bo_ref[...]


def _attn_meta_call(h, g, wqvT, bqvT, wk, bk, tbl, wo, bo):
    vmem = pl.BlockSpec(memory_space=pltpu.VMEM)
    return pl.pallas_call(
        _attn_meta_kernel,
        out_shape=(jax.ShapeDtypeStruct((N_META, D_MODEL), F32),
                   jax.ShapeDtypeStruct((N_META, KV_DIM), BF16),
                   jax.ShapeDtypeStruct((KV_DIM, 2 * N_META), BF16)),
        in_specs=[vmem] * 9,
        out_specs=(vmem, vmem, vmem),
        compiler_params=pltpu.CompilerParams(vmem_limit_bytes=VMEM_LIMIT),
        name="attn_meta",
    )(h, g, wqvT, bqvT, wk, bk, tbl, wo, bo)


def _ffn_math(x, g_ref, wg_ref, wu_ref, wd_ref):
    rows = x.shape[0]
    lead = FFN_LEAD_ROWS if rows > 2 * FFN_LEAD_ROWS else rows
    chunks = [x[:lead]] + ([x[lead:]] if lead < rows else [])
    hmids = []
    for xc in chunks:
        a = _rms(xc, g_ref[...]).astype(BF16)
        gate = _dot(a, wg_ref[...])
        up = _dot(a, wu_ref[...])
        hmids.append((gate * (1.0 / (1.0 + jnp.exp(-gate))) * up).astype(BF16))
    outs = [xc + _dot(hm, wd_ref[...]) for xc, hm in zip(chunks, hmids)]
    return outs[0] if len(outs) == 1 else jnp.concatenate(outs, axis=0)


def _ffn_kernel(h_ref, g_ref, wg_ref, wu_ref, wd_ref, gf_ref, out_ref, *, final_norm):
    y = _ffn_math(h_ref[...], g_ref, wg_ref, wu_ref, wd_ref)
    out_ref[...] = _rms(y, gf_ref[...]) if final_norm else y


def _ffn_specs(wg, wu, wd):
    return [_const_spec((1, D_MODEL)), _const_spec(wg.shape), _const_spec(wu.shape), _const_spec(wd.shape)]


def _ffn_call(h, g, wg, wu, wd, gf, tm, final_norm, casts=()):
    n = h.shape[0]
    row = pl.BlockSpec((tm, D_MODEL), lambda i: (i, 0))
    cast_in, cast_out, cast_shapes = _cast_specs(casts, n // tm, lambda i: i)
    in_specs = [row] + _ffn_specs(wg, wu, wd) + [_const_spec((1, D_MODEL))]
    outs = pl.pallas_call(
        _hosting(functools.partial(_ffn_kernel, final_norm=final_norm), len(in_specs), 1, len(casts)),
        grid=(n // tm,),
        out_shape=[jax.ShapeDtypeStruct((n, D_MODEL), F32)] + cast_shapes,
        in_specs=in_specs + cast_in,
        out_specs=[row] + cast_out,
        compiler_params=pltpu.CompilerParams(dimension_semantics=("parallel",), vmem_limit_bytes=VMEM_LIMIT),
        name="ffn",
    )(h, g, wg, wu, wd, gf, *[v for v, _ in casts])
    return outs[0], outs[1:]


def _conv_ffn_kernel(h_ref, g_ref, win_ref, cw_ref, wout_ref, zinit_ref, gf_ref, wg_ref, wu_ref, wd_ref,
                     out_ref, ztail_ref, zprev_scr, *, tm):
    @pl.when(pl.program_id(1) == 0)
    def _():
        zprev_scr[...] = zinit_ref[...]

    x = h_ref[...]
    a = _rms(x, g_ref[...]).astype(BF16)
    bcu = _dot(a, win_ref[...])
    gate_b = bcu[:, :D_MODEL]
    z = bcu[:, D_MODEL:2 * D_MODEL] * bcu[:, 2 * D_MODEL:]
    zp = zprev_scr[...]
    row = lax.broadcasted_iota(jnp.int32, (tm, D_MODEL), 0)
    last, last2 = zp[CONV_TAIL - 1:CONV_TAIL], zp[CONV_TAIL - 2:CONV_TAIL - 1]
    z1 = jnp.where(row == 0, last, pltpu.roll(z, 1, 0))
    z2 = jnp.where(row == 0, last2, jnp.where(row == 1, last, pltpu.roll(z, 2, 0)))
    cw = cw_ref[...]
    conv = cw[0:1] * z2 + cw[1:2] * z1 + cw[2:3] * z
    y = x + _dot((gate_b * conv).astype(BF16), wout_ref[...])
    out_ref[...] = _ffn_math(y, gf_ref, wg_ref, wu_ref, wd_ref)
    tail = z[tm - CONV_TAIL:]
    zprev_scr[...] = tail
    ztail_ref[...] = tail


def _conv_ffn_call(h, g, win, cw, wout, zinit, gf, wg, wu, wd, batch, tm, casts=()):
    n = h.shape[0]
    nt = n // batch // tm
    row = pl.BlockSpec((tm, D_MODEL), lambda b, i: (b * nt + i, 0))
    tail_shape = (CONV_TAIL, D_MODEL)
    cast_in, cast_out, cast_shapes = _cast_specs(casts, batch * nt, lambda b, i: b * nt + i)
    in_specs = [row, _const_spec((1, D_MODEL)), _const_spec(win.shape), _const_spec(cw.shape),
                _const_spec(wout.shape), _const_spec(tail_shape)] + _ffn_specs(wg, wu, wd)
    outs = pl.pallas_call(
        _hosting(functools.partial(_conv_ffn_kernel, tm=tm), len(in_specs), 2, len(casts)),
        grid=(batch, nt),
        out_shape=[jax.ShapeDtypeStruct((n, D_MODEL), F32), jax.ShapeDtypeStruct(tail_shape, F32)] + cast_shapes,
        in_specs=in_specs + cast_in,
        out_specs=[row, pl.BlockSpec(tail_shape, lambda b, i: (0, 0))] + cast_out,
        scratch_shapes=[pltpu.VMEM(tail_shape, F32)],
        compiler_params=pltpu.CompilerParams(dimension_semantics=("arbitrary", "arbitrary"),
                                             vmem_limit_bytes=VMEM_LIMIT),
        name="conv_ffn",
    )(h, g, win, cw, wout, zinit, gf, wg, wu, wd, *[v for v, _ in casts])
    return outs[0], outs[1], outs[2:]


def _pool_ffn_kernel(h_ref, g_ref, wp_ref, sc_ref, hinit_ref, gf_ref, wg_ref, wu_ref, wd_ref,
                     out_ref, atail_ref, hist_scr, *, tm, from_start):
    @pl.when(pl.program_id(1) == 0)
    def _():
        hist_scr[...] = hinit_ref[...]

    x = h_ref[...]
    a = _rms(x, g_ref[...])
    ext = jnp.concatenate([hist_scr[...], a], axis=0)
    outs = []
    for gi, win in enumerate(POOL_WINDOWS):
        cols = slice(gi * POOL_GROUP_DIM, (gi + 1) * POOL_GROUP_DIM)
        s = ext[:, cols]
        k = 1
        while k < win:
            s = s + pltpu.roll(s, k, 0)
            k *= 2
        s = s[POOL_HIST:]
        if from_start:
            pos = lax.broadcasted_iota(jnp.int32, (tm, POOL_GROUP_DIM), 0)
            mean = s / jnp.minimum(win, pos + 1).astype(F32)
        else:
            mean = s * (1.0 / win)
        mix = (mean - a[:, cols]).astype(BF16)
        outs.append(_dot(mix, wp_ref[gi]))
    y = x + jnp.concatenate(outs, axis=-1) * sc_ref[...]
    out_ref[...] = _ffn_math(y, gf_ref, wg_ref, wu_ref, wd_ref)
    tail = a[tm - POOL_HIST:]
    hist_scr[...] = tail
    atail_ref[...] = tail


def _pool_ffn_call(h, g, wp, sc, hinit, gf, wg, wu, wd, batch, tm, from_start):
    n = h.shape[0]
    nt = n // batch // tm
    row = pl.BlockSpec((tm, D_MODEL), lambda b, i: (b * nt + i, 0))
    hist_shape = (POOL_HIST, D_MODEL)
    return pl.pallas_call(
        functools.partial(_pool_ffn_kernel, tm=tm, from_start=from_start),
        grid=(batch, nt),
        out_shape=(jax.ShapeDtypeStruct((n, D_MODEL), F32), jax.ShapeDtypeStruct(hist_shape, F32)),
        in_specs=[row, _const_spec((1, D_MODEL)), _const_spec(wp.shape), _const_spec((1, D_MODEL)),
                  _const_spec(hist_shape)] + _ffn_specs(wg, wu, wd),
        out_specs=(row, pl.BlockSpec(hist_shape, lambda b, i: (0, 0))),
        scratch_shapes=[pltpu.VMEM(hist_shape, F32)],
        compiler_params=pltpu.CompilerParams(dimension_semantics=("arbitrary", "arbitrary"),
                                             vmem_limit_bytes=VMEM_LIMIT),
        name="pool_ffn",
    )(h, g, wp, sc, hinit, gf, wg, wu, wd)


ATTN_TM = 512
ATTN_LOOKAHEAD = 2
ROW_TM = 1024
CONV_TM = 512


def kernel(x, meta_tokens, rel_bias_table, norm_mix, norm_ffn, norm_final, attn_w_qkv, attn_b_qkv, attn_w_o,
           attn_b_o, attn_sinks, conv_w_in, conv_w, conv_w_out, pool_w, pool_scale, ffn_w_gate, ffn_w_up,
           ffn_w_down):
    batch, seq, _ = x.shape
    depth = norm_mix.shape[0]
    n_mixers = 3
    hr = x.reshape(batch * seq, D_MODEL)
    hm = meta_tokens.astype(F32)
    row = lambda v: v.reshape(1, -1)
    meta_codes = _meta_codes()
    d_ff = ffn_w_gate.shape[2]
    ffn_views = [_cast_view(ffn_w_gate), _cast_view(ffn_w_up), _cast_view(ffn_w_down)]
    conv_views = [_cast_view(conv_w_in), _cast_view(conv_w_out)]

    def ffn_casts(layer):
        return [(v, layer) for v in ffn_views]

    def ffn_weights(cast):
        return (cast[0].reshape(D_MODEL, d_ff), cast[1].reshape(D_MODEL, d_ff), cast[2].reshape(d_ff, D_MODEL))

    def next_layer_casts(layer):
        nxt = layer + 1
        if nxt >= depth or nxt % n_mixers == 0:
            return []
        conv = [(v, nxt // n_mixers) for v in conv_views] if nxt % n_mixers == 1 else []
        return conv + ffn_casts(nxt)

    def unpack_next(layer, cast):
        nonlocal wg, wu, wd, conv_bf16
        if not cast:
            return
        if (layer + 1) % n_mixers == 1:
            conv_bf16 = (cast[0].reshape(D_MODEL, 3 * D_MODEL), cast[1].reshape(D_MODEL, D_MODEL))
            cast = cast[2:]
        wg, wu, wd = ffn_weights(cast)

    wg = wu = wd = conv_bf16 = None
    for i in range(depth):
        kind, j = i % n_mixers, i // n_mixers
        last = i == depth - 1
        gm, gf = row(norm_mix[i]), row(norm_ffn[i])
        if kind == 0:
            w, b = attn_w_qkv[j], attn_b_qkv[j]
            wqvT = jnp.concatenate([w[:, :Q_DIM], w[:, Q_DIM + KV_DIM:]], axis=1).T.astype(BF16)
            bqvT = jnp.broadcast_to(jnp.concatenate([b[:Q_DIM], b[Q_DIM + KV_DIM:]])[:, None], (Q_DIM + KV_DIM, LANES))
            wk, bk = w[:, Q_DIM:Q_DIM + KV_DIM].astype(BF16), row(b[Q_DIM:Q_DIM + KV_DIM])
            wo, bo = attn_w_o[j].astype(BF16), row(attn_b_o[j])
            tbl_meta = _meta_table(meta_codes, rel_bias_table, attn_sinks[j])
            tbl_real = _real_table(rel_bias_table, attn_sinks[j])
            hm, km, vmT = _attn_meta_call(hm, gm, wqvT, bqvT, wk, row(b), tbl_meta, wo, bo)
            qT, k, vT = _qkv_call(hr, gm, wqvT, bqvT, wk, bk, ROW_TM, ATTN_TM)
            hr, cast = _attn_core_call(qT, k, vT, km, vmT, tbl_real, hr, wo, bo, batch, ATTN_TM, ffn_casts(i))
            wg, wu, wd = ffn_weights(cast)
        elif kind == 1:
            win, wout = conv_bf16
            hm, ztail, _ = _conv_ffn_call(hm, gm, win, conv_w[j], wout, jnp.zeros((CONV_TAIL, D_MODEL), F32),
                                          gf, wg, wu, wd, 1, N_META)
            hr, _, cast = _conv_ffn_call(hr, gm, win, conv_w[j], wout, ztail, gf, wg, wu, wd, batch, CONV_TM,
                                         next_layer_casts(i))
            unpack_next(i, cast)
            continue
        else:
            wp, sc = pool_w[j].astype(BF16), row(pool_scale[j])
            hm, atail = _pool_ffn_call(hm, gm, wp, sc, jnp.zeros((POOL_HIST, D_MODEL), F32),
                                       gf, wg, wu, wd, 1, N_META, True)
            hr, _ = _pool_ffn_call(hr, gm, wp, sc, atail, gf, wg, wu, wd, batch, ROW_TM, False)
            wg = wu = wd = None
            continue
        if not last:
            hm, _ = _ffn_call(hm, gf, wg, wu, wd, row(norm_final), N_META, False)
        hr, cast = _ffn_call(hr, gf, wg, wu, wd, row(norm_final), ROW_TM, last, next_layer_casts(i))
        wg = wu = wd = None
        unpack_next(i, cast)
    return hr.reshape(batch, seq, D_MODEL)
```

```python
import functools
import math

import numpy as np
import jax
import jax.numpy as jnp
from jax import lax
from jax.experimental import pallas as pl
from jax.experimental.pallas import tpu as pltpu

F32 = jnp.float32
BF16 = jnp.bfloat16

D_MODEL = 1024
N_META = 16
RMS_EPS = 1e-6
N_HEADS = 16
N_KV = 4
HEAD_DIM = 64
GROUP = N_HEADS // N_KV
Q_DIM = N_HEADS * HEAD_DIM
KV_DIM = N_KV * HEAD_DIM
WINDOW = 128
N_BUCKETS = 32
MAX_DISTANCE = 128
POOL_WINDOWS = (2, 4, 8, 16)
POOL_GROUP_DIM = D_MODEL // len(POOL_WINDOWS)
POOL_HIST = 16
CONV_TAIL = 8
LANES = 128
FFN_LEAD_ROWS = 256
BF16_SUBLANES = 16
LOG2E = math.log2(math.e)

Q_BLK = 128
BAND_ROWS = WINDOW + Q_BLK
SINK_ROW = BAND_ROWS + N_META
KEY_ROWS = SINK_ROW + N_META
N_NEAR = WINDOW // Q_BLK
ONES_ROWS = 16

META_KEY_COLS = 128
CODE_SINK = N_BUCKETS
CODE_MASKED = N_BUCKETS + 1

VMEM_LIMIT = 56 * 1024 * 1024


def _rel_bucket_np(dist):
    max_exact = N_BUCKETS // 2
    d = np.maximum(dist, 0)
    df = np.maximum(d, 1).astype(np.float64)
    large = max_exact + (np.log(df / max_exact) / math.log(MAX_DISTANCE / max_exact)
                         * (N_BUCKETS - max_exact)).astype(np.int64)
    large = np.minimum(large, N_BUCKETS - 1)
    return np.where(d < max_exact, d, large).astype(np.int32)


def _meta_codes():
    i = np.arange(N_META)[:, None]
    m = np.arange(N_META)[None, :]
    code = np.full((N_META, META_KEY_COLS), CODE_MASKED, np.int32)
    code[:, :N_META] = np.where(i >= m, _rel_bucket_np(i - m), CODE_MASKED)
    code[:, N_META] = CODE_SINK
    return code[None]


def _band_codes():
    return np.stack([_rel_bucket_np(np.arange(WINDOW)), np.zeros(WINDOW, np.int32)])


def _near_meta_codes():
    m = np.arange(N_META)[:, None]
    i = np.arange(Q_BLK)[None, :]
    return np.stack([_rel_bucket_np(N_META + n * Q_BLK + i - m) for n in range(N_NEAR)])


def _const_spec(shape):
    zeros = (0,) * len(shape)
    return pl.BlockSpec(shape, lambda *_: zeros, pipeline_mode=pl.Buffered(1))


def _hosting(kernel_fn, n_in, n_out, n_cast):
    def wrapped(*refs):
        ins, rest = refs[:n_in], refs[n_in:]
        cast_in, rest = rest[:n_cast], rest[n_cast:]
        outs, rest = rest[:n_out], rest[n_out:]
        cast_out, scratch = rest[:n_cast], rest[n_cast:]
        for src, dst in zip(cast_in, cast_out):
            dst[...] = src[...].astype(BF16)
        kernel_fn(*ins, *outs, *scratch)
    return wrapped


def _cast_specs(casts, steps, flat_step):
    in_specs, out_specs, out_shapes = [], [], []
    for w, layer in casts:
        _, rows, cols = w.shape
        n_slices = max(n for n in range(1, steps + 1)
                       if steps % n == 0 and rows % n == 0 and (rows // n) % BF16_SUBLANES == 0)
        per = steps // n_slices
        blk = rows // n_slices
        in_specs.append(pl.BlockSpec((None, blk, cols), lambda *idx, l=layer, p=per: (l, flat_step(*idx) // p, 0)))
        out_specs.append(pl.BlockSpec((blk, cols), lambda *idx, p=per: (flat_step(*idx) // p, 0)))
        out_shapes.append(jax.ShapeDtypeStruct((rows, cols), BF16))
    return in_specs, out_specs, out_shapes


def _smem_spec():
    return pl.BlockSpec(memory_space=pltpu.SMEM)


def _rms(x, g):
    return x * lax.rsqrt(jnp.mean(x * x, axis=-1, keepdims=True) + RMS_EPS) * g


def _dot(a, b):
    return jnp.dot(a, b, preferred_element_type=F32)


def _dot_nt(a, b):
    return lax.dot_general(a, b, (((1,), (1,)), ((), ())), preferred_element_type=F32)


def _dot_tn(a, b):
    return lax.dot_general(a, b, (((0,), (0,)), ((), ())), preferred_element_type=F32)


def _select_by_code(code, tbl_ref, h, otherwise):
    val = otherwise
    for b in range(N_BUCKETS):
        val = jnp.where(code == b, tbl_ref[b, h], val)
    return val


def _meta_table_kernel(code_ref, tbl_ref, sink_ref, out_ref):
    n_var, rows, _ = code_ref.shape
    for v in range(n_var):
        code = code_ref[v]
        for h in range(N_HEADS):
            kh, g = divmod(h, GROUP)
            val = _select_by_code(code, tbl_ref, h, jnp.where(code == CODE_SINK, sink_ref[h], -jnp.inf))
            out_ref[v, kh, g * rows:(g + 1) * rows, :] = val


def _meta_table(codes, rel_table, sinks):
    n_var, rows, cols = codes.shape
    return pl.pallas_call(
        _meta_table_kernel,
        out_shape=jax.ShapeDtypeStruct((n_var, N_KV, GROUP * rows, cols), F32),
        in_specs=[pl.BlockSpec(memory_space=pltpu.VMEM), _smem_spec(), _smem_spec()],
        out_specs=pl.BlockSpec(memory_space=pltpu.VMEM),
        name="meta_bias_table",
    )(jnp.asarray(codes), rel_table, sinks)


def _real_table_kernel(band_code_ref, meta_code_ref, tbl_ref, sink_ref, out_ref):
    neg = -jnp.inf
    band_code = band_code_ref[0:1, :]
    t = lax.broadcasted_iota(jnp.int32, (BAND_ROWS, WINDOW), 0)
    i = lax.broadcasted_iota(jnp.int32, (BAND_ROWS, WINDOW), 1)
    in_window = (i < t) & (t <= i + WINDOW)
    row16 = lax.broadcasted_iota(jnp.int32, (N_META, Q_BLK), 0)
    for h in range(N_HEADS):
        kh, g = divmod(h, GROUP)
        lanes = slice(g * Q_BLK, (g + 1) * Q_BLK)
        by_dist = _select_by_code(band_code, tbl_ref, h, jnp.zeros((1, WINDOW), F32)) * LOG2E
        rolled = pltpu.roll(jnp.broadcast_to(by_dist, (WINDOW, WINDOW)), 0, 1, stride=1, stride_axis=0)
        rolled = jnp.concatenate([rolled, rolled[:BAND_ROWS - WINDOW]], axis=0)
        far = jnp.full((N_META, Q_BLK), tbl_ref[N_BUCKETS - 1, h], F32)
        sink_pad = jnp.where(row16 == 0, sink_ref[h] * LOG2E, neg)
        for v in range(1 + N_NEAR):
            if v == 0:
                valid, meta = in_window, far * LOG2E
            else:
                first_key_row = WINDOW - (v - 1) * Q_BLK
                valid = in_window & (t >= first_key_row)
                meta = _select_by_code(meta_code_ref[v - 1], tbl_ref, h, far) * LOG2E
            out_ref[v, kh, 0:BAND_ROWS, lanes] = jnp.where(valid, rolled, neg)[:, :Q_BLK]
            out_ref[v, kh, BAND_ROWS:SINK_ROW, lanes] = meta
            out_ref[v, kh, SINK_ROW:KEY_ROWS, lanes] = sink_pad


def _real_table(rel_table, sinks):
    vmem = pl.BlockSpec(memory_space=pltpu.VMEM)
    return pl.pallas_call(
        _real_table_kernel,
        out_shape=jax.ShapeDtypeStruct((1 + N_NEAR, N_KV, KEY_ROWS, GROUP * Q_BLK), F32),
        in_specs=[vmem, vmem, _smem_spec(), _smem_spec()],
        out_specs=vmem,
        name="real_bias_table",
    )(jnp.asarray(_band_codes()), jnp.asarray(_near_meta_codes()), rel_table, sinks)


def _qkv_kernel(h_ref, g_ref, wqv_ref, bqv_ref, wk_ref, bk_ref, qT_ref, k_ref, vT_ref, *, tm, blk):
    a = _rms(h_ref[...], g_ref[...]).astype(BF16)
    bias = jnp.concatenate([bqv_ref[...]] * (tm // LANES), axis=1)
    qvT = _dot_nt(wqv_ref[...], a) + bias
    qT = (qvT[:Q_DIM] * (HEAD_DIM ** -0.5 * LOG2E)).astype(BF16)
    vT = qvT[Q_DIM:].astype(BF16)
    for c in range(tm // blk):
        qT_ref[c] = qT[:, c * blk:(c + 1) * blk]
        vT_ref[c] = vT[:, c * blk:(c + 1) * blk]
    k_ref[...] = (_dot(a, wk_ref[...]) + bk_ref[...]).astype(BF16)


def _qkv_call(h, g, wqvT, bqvT, wk, bk, tm, blk):
    n = h.shape[0]
    row = lambda c: pl.BlockSpec((tm, c), lambda i: (i, 0))
    col = lambda r: pl.BlockSpec((tm // blk, r, blk), lambda i: (i, 0, 0))
    return pl.pallas_call(
        functools.partial(_qkv_kernel, tm=tm, blk=blk),
        grid=(n // tm,),
        out_shape=(jax.ShapeDtypeStruct((n // blk, Q_DIM, blk), BF16),
                   jax.ShapeDtypeStruct((n, KV_DIM), BF16),
                   jax.ShapeDtypeStruct((n // blk, KV_DIM, blk), BF16)),
        in_specs=[row(D_MODEL), _const_spec((1, D_MODEL)), _const_spec(wqvT.shape), _const_spec(bqvT.shape),
                  _const_spec(wk.shape), _const_spec((1, KV_DIM))],
        out_specs=(col(Q_DIM), row(KV_DIM), col(KV_DIM)),
        compiler_params=pltpu.CompilerParams(dimension_semantics=("parallel",), vmem_limit_bytes=VMEM_LIMIT),
        name="attn_qkv",
    )(h, g, wqvT, bqvT, wk, bk)


def _attn_core_kernel(qT_ref, kp_ref, kc_ref, vTp_ref, vTc_ref, km_ref, vmT_ref, tbl_ref, h_ref, wo_ref, bo_ref,
                      out_ref, oT_scr, *, tm):
    first_tile = pl.program_id(1) == 0
    kext = jnp.concatenate([kp_ref[...], kc_ref[...]], axis=0)
    vText = jnp.concatenate([vTp_ref[...], vTc_ref[...]], axis=1)
    km = km_ref[...]
    vmT = vmT_ref[...]
    zk = jnp.zeros((KEY_ROWS - SINK_ROW, HEAD_DIM), BF16)
    ones = jnp.ones((ONES_ROWS, KEY_ROWS), BF16)

    def scores(j, kh):
        r0 = j * Q_BLK
        cols = slice(kh * HEAD_DIM, (kh + 1) * HEAD_DIM)
        qsT = jnp.concatenate(
            [qT_ref[(kh * GROUP + g) * HEAD_DIM:(kh * GROUP + g + 1) * HEAD_DIM, r0:r0 + Q_BLK]
             for g in range(GROUP)], axis=1)
        keys = jnp.concatenate([kext[r0:r0 + BAND_ROWS, cols], km[:, cols], zk], axis=0)
        return _dot(keys, qsT)

    def finish(j, kh, sT, sT_next):
        r0 = j * Q_BLK
        cols = slice(kh * HEAD_DIM, (kh + 1) * HEAD_DIM)
        var = jnp.where(first_tile, 1 + j, 0) if j < N_NEAR else 0
        sT = sT + tbl_ref[var, kh]
        m = jnp.max(sT, axis=0, keepdims=True)
        pT = jnp.exp2(sT - m).astype(BF16)
        valsT = jnp.concatenate([vText[cols, r0:r0 + BAND_ROWS], vmT[cols, :]], axis=1)
        den = ones
        if sT_next is not None:
            bits = pltpu.bitcast(sT_next[KEY_ROWS - ONES_ROWS:, :LANES], jnp.uint32)
            zero = pltpu.bitcast((bits >> 16) >> 16, F32).astype(BF16)
            den = jnp.concatenate([ones[:, :LANES] + zero, ones[:, LANES:]], axis=1)
        oT = _dot(jnp.concatenate([valsT, den], axis=0), pT)
        oT = oT[:HEAD_DIM] / oT[HEAD_DIM:HEAD_DIM + 1]
        for g in range(GROUP):
            h0 = (kh * GROUP + g) * HEAD_DIM
            oT_scr[h0:h0 + HEAD_DIM, r0:r0 + Q_BLK] = oT[:, g * Q_BLK:(g + 1) * Q_BLK].astype(BF16)

    units = [(j, kh) for j in range(tm // Q_BLK) for kh in range(N_KV)]
    pending = [scores(*u) for u in units[:ATTN_LOOKAHEAD]]
    for idx, unit in enumerate(units):
        ahead = idx + ATTN_LOOKAHEAD
        if ahead < len(units):
            pending.append(scores(*units[ahead]))
        sT = pending.pop(0)
        finish(*unit, sT, pending[-1] if pending else None)
    out_ref[...] = h_ref[...] + _dot_tn(oT_scr[...], wo_ref[...]) + bo_ref[...]


def _attn_core_call(qT, k, vT, km, vmT, tbl, h, wo, bo, batch, tm, casts=()):
    n = h.shape[0]
    nt = n // batch // tm
    per = tm // WINDOW
    prev_idx = lambda b, i: jnp.maximum((b * nt + i) * per - 1, 0)
    prev_tile = lambda b, i: jnp.maximum(b * nt + i - 1, 0)
    row = lambda c: pl.BlockSpec((tm, c), lambda b, i: (b * nt + i, 0))
    col = lambda r: pl.BlockSpec((None, r, tm), lambda b, i: (b * nt + i, 0, 0))
    cast_in, cast_out, cast_shapes = _cast_specs(casts, batch * nt, lambda b, i: b * nt + i)
    in_specs = [col(Q_DIM),
                pl.BlockSpec((WINDOW, KV_DIM), lambda b, i: (prev_idx(b, i), 0)), row(KV_DIM),
                pl.BlockSpec((None, KV_DIM, WINDOW), lambda b, i: (prev_tile(b, i), 0, per - 1)), col(KV_DIM),
                _const_spec(km.shape), _const_spec(vmT.shape), _const_spec(tbl.shape),
                row(D_MODEL), _const_spec(wo.shape), _const_spec((1, D_MODEL))]
    outs = pl.pallas_call(
        _hosting(functools.partial(_attn_core_kernel, tm=tm), len(in_specs), 1, len(casts)),
        grid=(batch, nt),
        out_shape=[jax.ShapeDtypeStruct((n, D_MODEL), F32)] + cast_shapes,
        in_specs=in_specs + cast_in,
        out_specs=[row(D_MODEL)] + cast_out,
        scratch_shapes=[pltpu.VMEM((Q_DIM, tm), BF16)],
        compiler_params=pltpu.CompilerParams(dimension_semantics=("arbitrary", "arbitrary"),
                                             vmem_limit_bytes=VMEM_LIMIT),
        name="attn_core",
    )(qT, k, k, vT, vT, km, vmT, tbl, h, wo, bo, *[v for v, _ in casts])
    return outs[0], outs[1:]


def _attn_meta_kernel(h_ref, g_ref, wqv_ref, bqv_ref, wk_ref, b_ref, tbl_ref, wo_ref, bo_ref,
                      out_ref, km_ref, vmT_ref):
    x = h_ref[...]
    a = _rms(x, g_ref[...]).astype(BF16)
    a2 = jnp.concatenate([a, jnp.zeros_like(a)], axis=0)
    lane = lax.broadcasted_iota(jnp.int32, (KV_DIM, 2 * N_META), 1)
    vT = _dot_nt(wqv_ref[Q_DIM:, :], a2) + jnp.where(lane < N_META, bqv_ref[Q_DIM:, :2 * N_META], 0.0)
    vmT_ref[...] = vT.astype(BF16)
    b = b_ref[...]
    q = ((_dot_nt(a, wqv_ref[:Q_DIM, :]) + b[:, :Q_DIM]) * (HEAD_DIM ** -0.5)).astype(BF16)
    k = (_dot(a, wk_ref[...]) + b[:, Q_DIM:Q_DIM + KV_DIM]).astype(BF16)
    v = (_dot_nt(a, wqv_ref[Q_DIM:, :]) + b[:, Q_DIM + KV_DIM:]).astype(BF16)
    km_ref[...] = k
    pad = META_KEY_COLS - N_META
    vcat = jnp.concatenate([v, jnp.zeros((pad, KV_DIM), BF16)], axis=0)
    pieces = []
    for kh in range(N_KV):
        cols = slice(kh * HEAD_DIM, (kh + 1) * HEAD_DIM)
        qs = jnp.concatenate([q[:, (kh * GROUP + g) * HEAD_DIM:(kh * GROUP + g + 1) * HEAD_DIM]
                              for g in range(GROUP)], axis=0)
        kcat = jnp.concatenate([k[:, cols], jnp.zeros((pad, HEAD_DIM), BF16)], axis=0)
        s = _dot_nt(qs, kcat) + tbl_ref[0, kh]
        p = jnp.exp(s - jnp.max(s, axis=-1, keepdims=True))
        l = jnp.sum(p, axis=-1, keepdims=True)
        o = _dot(p.astype(BF16), vcat)[:, cols] / l
        pieces += [o[g * N_META:(g + 1) * N_META] for g in range(GROUP)]
    o_all = jnp.concatenate(pieces, axis=-1).astype(BF16)
    out_ref[...] = x + _dot(o_all, wo_ref[...]) + bo_ref[...]


def _attn_meta_call(h, g, wqvT, bqvT, wk, bk, tbl, wo, bo):
    vmem = pl.BlockSpec(memory_space=pltpu.VMEM)
    return pl.pallas_call(
        _attn_meta_kernel,
        out_shape=(jax.ShapeDtypeStruct((N_META, D_MODEL), F32),
                   jax.ShapeDtypeStruct((N_META, KV_DIM), BF16),
                   jax.ShapeDtypeStruct((KV_DIM, 2 * N_META), BF16)),
        in_specs=[vmem] * 9,
        out_specs=(vmem, vmem, vmem),
        compiler_params=pltpu.CompilerParams(vmem_limit_bytes=VMEM_LIMIT),
        name="attn_meta",
    )(h, g, wqvT, bqvT, wk, bk, tbl, wo, bo)


def _ffn_math(x, g_ref, wg_ref, wu_ref, wd_ref):
    rows = x.shape[0]
    lead = FFN_LEAD_ROWS if rows > 2 * FFN_LEAD_ROWS else rows
    chunks = [x[:lead]] + ([x[lead:]] if lead < rows else [])
    hmids = []
    for xc in chunks:
        a = _rms(xc, g_ref[...]).astype(BF16)
        gate = _dot(a, wg_ref[...])
        up = _dot(a, wu_ref[...])
        hmids.append((gate * (1.0 / (1.0 + jnp.exp(-gate))) * up).astype(BF16))
    outs = [xc + _dot(hm, wd_ref[...]) for xc, hm in zip(chunks, hmids)]
    return outs[0] if len(outs) == 1 else jnp.concatenate(outs, axis=0)


def _ffn_kernel(h_ref, g_ref, wg_ref, wu_ref, wd_ref, gf_ref, out_ref, *, final_norm):
    y = _ffn_math(h_ref[...], g_ref, wg_ref, wu_ref, wd_ref)
    out_ref[...] = _rms(y, gf_ref[...]) if final_norm else y


def _ffn_specs(wg, wu, wd):
    return [_const_spec((1, D_MODEL)), _const_spec(wg.shape), _const_spec(wu.shape), _const_spec(wd.shape)]


def _ffn_call(h, g, wg, wu, wd, gf, tm, final_norm, casts=()):
    n = h.shape[0]
    row = pl.BlockSpec((tm, D_MODEL), lambda i: (i, 0))
    cast_in, cast_out, cast_shapes = _cast_specs(casts, n // tm, lambda i: i)
    in_specs = [row] + _ffn_specs(wg, wu, wd) + [_const_spec((1, D_MODEL))]
    outs = pl.pallas_call(
        _hosting(functools.partial(_ffn_kernel, final_norm=final_norm), len(in_specs), 1, len(casts)),
        grid=(n // tm,),
        out_shape=[jax.ShapeDtypeStruct((n, D_MODEL), F32)] + cast_shapes,
        in_specs=in_specs + cast_in,
        out_specs=[row] + cast_out,
        compiler_params=pltpu.CompilerParams(dimension_semantics=("parallel",), vmem_limit_bytes=VMEM_LIMIT),
        name="ffn",
    )(h, g, wg, wu, wd, gf, *[v for v, _ in casts])
    return outs[0], outs[1:]


def _conv_ffn_kernel(h_ref, g_ref, win_ref, cw_ref, wout_ref, zinit_ref, gf_ref, wg_ref, wu_ref, wd_ref,
                     out_ref, ztail_ref, zprev_scr, *, tm):
    @pl.when(pl.program_id(1) == 0)
    def _():
        zprev_scr[...] = zinit_ref[...]

    x = h_ref[...]
    a = _rms(x, g_ref[...]).astype(BF16)
    bcu = _dot(a, win_ref[...])
    gate_b = bcu[:, :D_MODEL]
    z = bcu[:, D_MODEL:2 * D_MODEL] * bcu[:, 2 * D_MODEL:]
    zp = zprev_scr[...]
    row = lax.broadcasted_iota(jnp.int32, (tm, D_MODEL), 0)
    last, last2 = zp[CONV_TAIL - 1:CONV_TAIL], zp[CONV_TAIL - 2:CONV_TAIL - 1]
    z1 = jnp.where(row == 0, last, pltpu.roll(z, 1, 0))
    z2 = jnp.where(row == 0, last2, jnp.where(row == 1, last, pltpu.roll(z, 2, 0)))
    cw = cw_ref[...]
    conv = cw[0:1] * z2 + cw[1:2] * z1 + cw[2:3] * z
    y = x + _dot((gate_b * conv).astype(BF16), wout_ref[...])
    out_ref[...] = _ffn_math(y, gf_ref, wg_ref, wu_ref, wd_ref)
    tail = z[tm - CONV_TAIL:]
    zprev_scr[...] = tail
    ztail_ref[...] = tail


def _conv_ffn_call(h, g, win, cw, wout, zinit, gf, wg, wu, wd, batch, tm, casts=()):
    n = h.shape[0]
    nt = n // batch // tm
    row = pl.BlockSpec((tm, D_MODEL), lambda b, i: (b * nt + i, 0))
    tail_shape = (CONV_TAIL, D_MODEL)
    cast_in, cast_out, cast_shapes = _cast_specs(casts, batch * nt, lambda b, i: b * nt + i)
    in_specs = [row, _const_spec((1, D_MODEL)), _const_spec(win.shape), _const_spec(cw.shape),
                _const_spec(wout.shape), _const_spec(tail_shape)] + _ffn_specs(wg, wu, wd)
    outs = pl.pallas_call(
        _hosting(functools.partial(_conv_ffn_kernel, tm=tm), len(in_specs), 2, len(casts)),
        grid=(batch, nt),
        out_shape=[jax.ShapeDtypeStruct((n, D_MODEL), F32), jax.ShapeDtypeStruct(tail_shape, F32)] + cast_shapes,
        in_specs=in_specs + cast_in,
        out_specs=[row, pl.BlockSpec(tail_shape, lambda b, i: (0, 0))] + cast_out,
        scratch_shapes=[pltpu.VMEM(tail_shape, F32)],
        compiler_params=pltpu.CompilerParams(dimension_semantics=("arbitrary", "arbitrary"),
                                             vmem_limit_bytes=VMEM_LIMIT),
        name="conv_ffn",
    )(h, g, win, cw, wout, zinit, gf, wg, wu, wd, *[v for v, _ in casts])
    return outs[0], outs[1], outs[2:]


def _pool_ffn_kernel(h_ref, g_ref, wp_ref, sc_ref, hinit_ref, gf_ref, wg_ref, wu_ref, wd_ref,
                     out_ref, atail_ref, hist_scr, *, tm, from_start):
    @pl.when(pl.program_id(1) == 0)
    def _():
        hist_scr[...] = hinit_ref[...]

    x = h_ref[...]
    a = _rms(x, g_ref[...])
    ext = jnp.concatenate([hist_scr[...], a], axis=0)
    outs = []
    for gi, win in enumerate(POOL_WINDOWS):
        cols = slice(gi * POOL_GROUP_DIM, (gi + 1) * POOL_GROUP_DIM)
        s = ext[:, cols]
        k = 1
        while k < win:
            s = s + pltpu.roll(s, k, 0)
            k *= 2
        s = s[POOL_HIST:]
        if from_start:
            pos = lax.broadcasted_iota(jnp.int32, (tm, POOL_GROUP_DIM), 0)
            mean = s / jnp.minimum(win, pos + 1).astype(F32)
        else:
            mean = s * (1.0 / win)
        mix = (mean - a[:, cols]).astype(BF16)
        outs.append(_dot(mix, wp_ref[gi]))
    y = x + jnp.concatenate(outs, axis=-1) * sc_ref[...]
    out_ref[...] = _ffn_math(y, gf_ref, wg_ref, wu_ref, wd_ref)
    tail = a[tm - POOL_HIST:]
    hist_scr[...] = tail
    atail_ref[...] = tail


def _pool_ffn_call(h, g, wp, sc, hinit, gf, wg, wu, wd, batch, tm, from_start):
    n = h.shape[0]
    nt = n // batch // tm
    row = pl.BlockSpec((tm, D_MODEL), lambda b, i: (b * nt + i, 0))
    hist_shape = (POOL_HIST, D_MODEL)
    return pl.pallas_call(
        functools.partial(_pool_ffn_kernel, tm=tm, from_start=from_start),
        grid=(batch, nt),
        out_shape=(jax.ShapeDtypeStruct((n, D_MODEL), F32), jax.ShapeDtypeStruct(hist_shape, F32)),
        in_specs=[row, _const_spec((1, D_MODEL)), _const_spec(wp.shape), _const_spec((1, D_MODEL)),
                  _const_spec(hist_shape)] + _ffn_specs(wg, wu, wd),
        out_specs=(row, pl.BlockSpec(hist_shape, lambda b, i: (0, 0))),
        scratch_shapes=[pltpu.VMEM(hist_shape, F32)],
        compiler_params=pltpu.CompilerParams(dimension_semantics=("arbitrary", "arbitrary"),
                                             vmem_limit_bytes=VMEM_LIMIT),
        name="pool_ffn",
    )(h, g, wp, sc, hinit, gf, wg, wu, wd)


ATTN_TM = 512
ATTN_LOOKAHEAD = 2
ROW_TM = 1024
CONV_TM = 512


def kernel(x, meta_tokens, rel_bias_table, norm_mix, norm_ffn, norm_final, attn_w_qkv, attn_b_qkv, attn_w_o,
           attn_b_o, attn_sinks, conv_w_in, conv_w, conv_w_out, pool_w, pool_scale, ffn_w_gate, ffn_w_up,
           ffn_w_down):
    batch, seq, _ = x.shape
    depth = norm_mix.shape[0]
    n_mixers = 3
    hr = x.reshape(batch * seq, D_MODEL)
    hm = meta_tokens.astype(F32)
    row = lambda v: v.reshape(1, -1)
    meta_codes = _meta_codes()

    def ffn_casts(layer):
        return [(w, layer) for w in (ffn_w_gate, ffn_w_up, ffn_w_down)]

    def next_layer_casts(layer):
        nxt = layer + 1
        if nxt >= depth or nxt % n_mixers == 0:
            return []
        conv = [(w, nxt // n_mixers) for w in (conv_w_in, conv_w_out)] if nxt % n_mixers == 1 else []
        return conv + ffn_casts(nxt)

    def unpack_next(layer, cast):
        nonlocal wg, wu, wd, conv_bf16
        if not cast:
            return
        if (layer + 1) % n_mixers == 1:
            conv_bf16, cast = cast[:2], cast[2:]
        wg, wu, wd = cast

    wg = wu = wd = conv_bf16 = None
    for i in range(depth):
        kind, j = i % n_mixers, i // n_mixers
        last = i == depth - 1
        gm, gf = row(norm_mix[i]), row(norm_ffn[i])
        if kind == 0:
            w, b = attn_w_qkv[j], attn_b_qkv[j]
            wqvT = jnp.concatenate([w[:, :Q_DIM], w[:, Q_DIM + KV_DIM:]], axis=1).T.astype(BF16)
            bqvT = jnp.broadcast_to(jnp.concatenate([b[:Q_DIM], b[Q_DIM + KV_DIM:]])[:, None], (Q_DIM + KV_DIM, LANES))
            wk, bk = w[:, Q_DIM:Q_DIM + KV_DIM].astype(BF16), row(b[Q_DIM:Q_DIM + KV_DIM])
            wo, bo = attn_w_o[j].astype(BF16), row(attn_b_o[j])
            tbl_meta = _meta_table(meta_codes, rel_bias_table, attn_sinks[j])
            tbl_real = _real_table(rel_bias_table, attn_sinks[j])
            hm, km, vmT = _attn_meta_call(hm, gm, wqvT, bqvT, wk, row(b), tbl_meta, wo, bo)
            qT, k, vT = _qkv_call(hr, gm, wqvT, bqvT, wk, bk, ROW_TM, ATTN_TM)
            hr, (wg, wu, wd) = _attn_core_call(qT, k, vT, km, vmT, tbl_real, hr, wo, bo, batch, ATTN_TM, ffn_casts(i))
        elif kind == 1:
            win, wout = conv_bf16
            hm, ztail, _ = _conv_ffn_call(hm, gm, win, conv_w[j], wout, jnp.zeros((CONV_TAIL, D_MODEL), F32),
                                          gf, wg, wu, wd, 1, N_META)
            hr, _, cast = _conv_ffn_call(hr, gm, win, conv_w[j], wout, ztail, gf, wg, wu, wd, batch, CONV_TM,
                                         next_layer_casts(i))
            unpack_next(i, cast)
            continue
        else:
            wp, sc = pool_w[j].astype(BF16), row(pool_scale[j])
            hm, atail = _pool_ffn_call(hm, gm, wp, sc, jnp.zeros((POOL_HIST, D_MODEL), F32),
                                       gf, wg, wu, wd, 1, N_META, True)
            hr, _ = _pool_ffn_call(hr, gm, wp, sc, atail, gf, wg, wu, wd, batch, ROW_TM, False)
            wg = wu = wd = None
            continue
        if not last:
            hm, _ = _ffn_call(hm, gf, wg, wu, wd, row(norm_final), N_META, False)
        hr, cast = _ffn_call(hr, gf, wg, wu, wd, row(norm_final), ROW_TM, last, next_layer_casts(i))
        wg = wu = wd = None
        unpack_next(i, cast)
    return hr.reshape(batch, seq, D_MODEL)
```

```python
import functools
import math

import numpy as np
import jax
import jax.numpy as jnp
from jax import lax
from jax.experimental import pallas as pl
from jax.experimental.pallas import tpu as pltpu

F32 = jnp.float32
BF16 = jnp.bfloat16

D_MODEL = 1024
N_META = 16
RMS_EPS = 1e-6
N_HEADS = 16
N_KV = 4
HEAD_DIM = 64
GROUP = N_HEADS // N_KV
Q_DIM = N_HEADS * HEAD_DIM
KV_DIM = N_KV * HEAD_DIM
WINDOW = 128
N_BUCKETS = 32
MAX_DISTANCE = 128
POOL_WINDOWS = (2, 4, 8, 16)
POOL_GROUP_DIM = D_MODEL // len(POOL_WINDOWS)
POOL_HIST = 16
CONV_TAIL = 8
LANES = 128
FFN_LEAD_ROWS = 256
BF16_SUBLANES = 16
LOG2E = math.log2(math.e)

Q_BLK = 128
BAND_ROWS = WINDOW + Q_BLK
SINK_ROW = BAND_ROWS + N_META
KEY_ROWS = SINK_ROW + N_META
N_NEAR = WINDOW // Q_BLK
ONES_ROWS = 16

META_KEY_COLS = 128
CODE_SINK = N_BUCKETS
CODE_MASKED = N_BUCKETS + 1

VMEM_LIMIT = 56 * 1024 * 1024


def _rel_bucket_np(dist):
    max_exact = N_BUCKETS // 2
    d = np.maximum(dist, 0)
    df = np.maximum(d, 1).astype(np.float64)
    large = max_exact + (np.log(df / max_exact) / math.log(MAX_DISTANCE / max_exact)
                         * (N_BUCKETS - max_exact)).astype(np.int64)
    large = np.minimum(large, N_BUCKETS - 1)
    return np.where(d < max_exact, d, large).astype(np.int32)


def _meta_codes():
    i = np.arange(N_META)[:, None]
    m = np.arange(N_META)[None, :]
    code = np.full((N_META, META_KEY_COLS), CODE_MASKED, np.int32)
    code[:, :N_META] = np.where(i >= m, _rel_bucket_np(i - m), CODE_MASKED)
    code[:, N_META] = CODE_SINK
    return code[None]


def _band_codes():
    return np.stack([_rel_bucket_np(np.arange(WINDOW)), np.zeros(WINDOW, np.int32)])


def _near_meta_codes():
    m = np.arange(N_META)[:, None]
    i = np.arange(Q_BLK)[None, :]
    return np.stack([_rel_bucket_np(N_META + n * Q_BLK + i - m) for n in range(N_NEAR)])


def _const_spec(shape):
    zeros = (0,) * len(shape)
    return pl.BlockSpec(shape, lambda *_: zeros, pipeline_mode=pl.Buffered(1))


def _hosting(kernel_fn, n_in, n_out, n_cast):
    def wrapped(*refs):
        ins, rest = refs[:n_in], refs[n_in:]
        cast_in, rest = rest[:n_cast], rest[n_cast:]
        outs, rest = rest[:n_out], rest[n_out:]
        cast_out, scratch = rest[:n_cast], rest[n_cast:]
        for src, dst in zip(cast_in, cast_out):
            dst[...] = src[...].astype(BF16)
        kernel_fn(*ins, *outs, *scratch)
    return wrapped


def _cast_specs(casts, steps, flat_step):
    in_specs, out_specs, out_shapes = [], [], []
    for w, layer in casts:
        _, rows, cols = w.shape
        n_slices = max(n for n in range(1, steps + 1)
                       if steps % n == 0 and rows % n == 0 and (rows // n) % BF16_SUBLANES == 0)
        per = steps // n_slices
        blk = rows // n_slices
        in_specs.append(pl.BlockSpec((None, blk, cols), lambda *idx, l=layer, p=per: (l, flat_step(*idx) // p, 0)))
        out_specs.append(pl.BlockSpec((blk, cols), lambda *idx, p=per: (flat_step(*idx) // p, 0)))
        out_shapes.append(jax.ShapeDtypeStruct((rows, cols), BF16))
    return in_specs, out_specs, out_shapes


def _smem_spec():
    return pl.BlockSpec(memory_space=pltpu.SMEM)


def _rms(x, g):
    return x * lax.rsqrt(jnp.mean(x * x, axis=-1, keepdims=True) + RMS_EPS) * g


def _dot(a, b):
    return jnp.dot(a, b, preferred_element_type=F32)


def _dot_nt(a, b):
    return lax.dot_general(a, b, (((1,), (1,)), ((), ())), preferred_element_type=F32)


def _dot_tn(a, b):
    return lax.dot_general(a, b, (((0,), (0,)), ((), ())), preferred_element_type=F32)


def _select_by_code(code, tbl_ref, h, otherwise):
    val = otherwise
    for b in range(N_BUCKETS):
        val = jnp.where(code == b, tbl_ref[b, h], val)
    return val


def _meta_table_kernel(code_ref, tbl_ref, sink_ref, out_ref):
    n_var, rows, _ = code_ref.shape
    for v in range(n_var):
        code = code_ref[v]
        for h in range(N_HEADS):
            kh, g = divmod(h, GROUP)
            val = _select_by_code(code, tbl_ref, h, jnp.where(code == CODE_SINK, sink_ref[h], -jnp.inf))
            out_ref[v, kh, g * rows:(g + 1) * rows, :] = val


def _meta_table(codes, rel_table, sinks):
    n_var, rows, cols = codes.shape
    return pl.pallas_call(
        _meta_table_kernel,
        out_shape=jax.ShapeDtypeStruct((n_var, N_KV, GROUP * rows, cols), F32),
        in_specs=[pl.BlockSpec(memory_space=pltpu.VMEM), _smem_spec(), _smem_spec()],
        out_specs=pl.BlockSpec(memory_space=pltpu.VMEM),
        name="meta_bias_table",
    )(jnp.asarray(codes), rel_table, sinks)


def _real_table_kernel(band_code_ref, meta_code_ref, tbl_ref, sink_ref, out_ref):
    neg = -jnp.inf
    band_code = band_code_ref[0:1, :]
    t = lax.broadcasted_iota(jnp.int32, (BAND_ROWS, WINDOW), 0)
    i = lax.broadcasted_iota(jnp.int32, (BAND_ROWS, WINDOW), 1)
    in_window = (i < t) & (t <= i + WINDOW)
    row16 = lax.broadcasted_iota(jnp.int32, (N_META, Q_BLK), 0)
    for h in range(N_HEADS):
        kh, g = divmod(h, GROUP)
        lanes = slice(g * Q_BLK, (g + 1) * Q_BLK)
        by_dist = _select_by_code(band_code, tbl_ref, h, jnp.zeros((1, WINDOW), F32)) * LOG2E
        rolled = pltpu.roll(jnp.broadcast_to(by_dist, (WINDOW, WINDOW)), 0, 1, stride=1, stride_axis=0)
        rolled = jnp.concatenate([rolled, rolled[:BAND_ROWS - WINDOW]], axis=0)
        far = jnp.full((N_META, Q_BLK), tbl_ref[N_BUCKETS - 1, h], F32)
        sink_pad = jnp.where(row16 == 0, sink_ref[h] * LOG2E, neg)
        for v in range(1 + N_NEAR):
            if v == 0:
                valid, meta = in_window, far * LOG2E
            else:
                first_key_row = WINDOW - (v - 1) * Q_BLK
                valid = in_window & (t >= first_key_row)
                meta = _select_by_code(meta_code_ref[v - 1], tbl_ref, h, far) * LOG2E
            out_ref[v, kh, 0:BAND_ROWS, lanes] = jnp.where(valid, rolled, neg)[:, :Q_BLK]
            out_ref[v, kh, BAND_ROWS:SINK_ROW, lanes] = meta
            out_ref[v, kh, SINK_ROW:KEY_ROWS, lanes] = sink_pad


def _real_table(rel_table, sinks):
    vmem = pl.BlockSpec(memory_space=pltpu.VMEM)
    return pl.pallas_call(
        _real_table_kernel,
        out_shape=jax.ShapeDtypeStruct((1 + N_NEAR, N_KV, KEY_ROWS, GROUP * Q_BLK), F32),
        in_specs=[vmem, vmem, _smem_spec(), _smem_spec()],
        out_specs=vmem,
        name="real_bias_table",
    )(jnp.asarray(_band_codes()), jnp.asarray(_near_meta_codes()), rel_table, sinks)


def _attn_kernel(h_ref, g_ref, wqv_ref, bqv_ref, wk_ref, bk_ref, km_ref, vmT_ref, tbl_ref, wo_ref, bo_ref,
                 out_ref, qT_scr, k_scr, vT_scr, oT_scr, *, tm):
    first_tile = pl.program_id(1) == 0

    @pl.when(first_tile)
    def _():
        k_scr[0:WINDOW, :] = jnp.zeros((WINDOW, KV_DIM), BF16)
        vT_scr[:, 0:WINDOW] = jnp.zeros((KV_DIM, WINDOW), BF16)

    @pl.when(jnp.logical_not(first_tile))
    def _():
        k_scr[0:WINDOW, :] = k_scr[tm:tm + WINDOW, :]
        vT_scr[:, 0:WINDOW] = vT_scr[:, tm:tm + WINDOW]

    x = h_ref[...]
    a = _rms(x, g_ref[...]).astype(BF16)
    bias = jnp.concatenate([bqv_ref[...]] * (tm // LANES), axis=1)
    qvT = _dot_nt(wqv_ref[...], a) + bias
    qT_scr[...] = (qvT[:Q_DIM] * (HEAD_DIM ** -0.5 * LOG2E)).astype(BF16)
    vT_scr[:, WINDOW:] = qvT[Q_DIM:].astype(BF16)
    k_scr[WINDOW:, :] = (_dot(a, wk_ref[...]) + bk_ref[...]).astype(BF16)

    kext = k_scr[...]
    vText = vT_scr[...]
    km = km_ref[...]
    vmT = vmT_ref[...]
    zk = jnp.zeros((KEY_ROWS - SINK_ROW, HEAD_DIM), BF16)
    ones = jnp.ones((ONES_ROWS, KEY_ROWS), BF16)

    def scores(j, kh):
        r0 = j * Q_BLK
        cols = slice(kh * HEAD_DIM, (kh + 1) * HEAD_DIM)
        qsT = jnp.concatenate(
            [qT_scr[(kh * GROUP + g) * HEAD_DIM:(kh * GROUP + g + 1) * HEAD_DIM, r0:r0 + Q_BLK]
             for g in range(GROUP)], axis=1)
        keys = jnp.concatenate([kext[r0:r0 + BAND_ROWS, cols], km[:, cols], zk], axis=0)
        return _dot(keys, qsT)

    def finish(j, kh, sT, sT_next):
        r0 = j * Q_BLK
        cols = slice(kh * HEAD_DIM, (kh + 1) * HEAD_DIM)
        var = jnp.where(first_tile, 1 + j, 0) if j < N_NEAR else 0
        sT = sT + tbl_ref[var, kh]
        m = jnp.max(sT, axis=0, keepdims=True)
        pT = jnp.exp2(sT - m).astype(BF16)
        valsT = jnp.concatenate([vText[cols, r0:r0 + BAND_ROWS], vmT[cols, :]], axis=1)
        den = ones
        if sT_next is not None:
            bits = pltpu.bitcast(sT_next[KEY_ROWS - ONES_ROWS:, :LANES], jnp.uint32)
            zero = pltpu.bitcast((bits >> 16) >> 16, F32).astype(BF16)
            den = jnp.concatenate([ones[:, :LANES] + zero, ones[:, LANES:]], axis=1)
        oT = _dot(jnp.concatenate([valsT, den], axis=0), pT)
        oT = oT[:HEAD_DIM] / oT[HEAD_DIM:HEAD_DIM + 1]
        for g in range(GROUP):
            h0 = (kh * GROUP + g) * HEAD_DIM
            oT_scr[h0:h0 + HEAD_DIM, r0:r0 + Q_BLK] = oT[:, g * Q_BLK:(g + 1) * Q_BLK].astype(BF16)

    units = [(j, kh) for j in range(tm // Q_BLK) for kh in range(N_KV)]
    pending = [scores(*u) for u in units[:ATTN_LOOKAHEAD]]
    for idx, unit in enumerate(units):
        ahead = idx + ATTN_LOOKAHEAD
        if ahead < len(units):
            pending.append(scores(*units[ahead]))
        sT = pending.pop(0)
        finish(*unit, sT, pending[-1] if pending else None)
    out_ref[...] = x + _dot_tn(oT_scr[...], wo_ref[...]) + bo_ref[...]


def _attn_call(h, g, wqvT, bqvT, wk, bk, km, vmT, tbl, wo, bo, batch, tm, casts=()):
    n = h.shape[0]
    nt = n // batch // tm
    row = pl.BlockSpec((tm, D_MODEL), lambda b, i: (b * nt + i, 0))
    cast_in, cast_out, cast_shapes = _cast_specs(casts, batch * nt, lambda b, i: b * nt + i)
    in_specs = [row, _const_spec((1, D_MODEL)), _const_spec(wqvT.shape), _const_spec(bqvT.shape),
                _const_spec(wk.shape), _const_spec((1, KV_DIM)), _const_spec(km.shape), _const_spec(vmT.shape),
                _const_spec(tbl.shape), _const_spec(wo.shape), _const_spec((1, D_MODEL))]
    outs = pl.pallas_call(
        _hosting(functools.partial(_attn_kernel, tm=tm), len(in_specs), 1, len(casts)),
        grid=(batch, nt),
        out_shape=[jax.ShapeDtypeStruct((n, D_MODEL), F32)] + cast_shapes,
        in_specs=in_specs + cast_in,
        out_specs=[row] + cast_out,
        scratch_shapes=[pltpu.VMEM((Q_DIM, tm), BF16), pltpu.VMEM((WINDOW + tm, KV_DIM), BF16),
                        pltpu.VMEM((KV_DIM, WINDOW + tm), BF16), pltpu.VMEM((Q_DIM, tm), BF16)],
        compiler_params=pltpu.CompilerParams(dimension_semantics=("arbitrary", "arbitrary"),
                                             vmem_limit_bytes=VMEM_LIMIT),
        name="attn",
    )(h, g, wqvT, bqvT, wk, bk, km, vmT, tbl, wo, bo, *[v for v, _ in casts])
    return outs[0], outs[1:]


def _attn_meta_kernel(h_ref, g_ref, wqv_ref, bqv_ref, wk_ref, b_ref, tbl_ref, wo_ref, bo_ref,
                      out_ref, km_ref, vmT_ref):
    x = h_ref[...]
    a = _rms(x, g_ref[...]).astype(BF16)
    a2 = jnp.concatenate([a, jnp.zeros_like(a)], axis=0)
    lane = lax.broadcasted_iota(jnp.int32, (KV_DIM, 2 * N_META), 1)
    vT = _dot_nt(wqv_ref[Q_DIM:, :], a2) + jnp.where(lane < N_META, bqv_ref[Q_DIM:, :2 * N_META], 0.0)
    vmT_ref[...] = vT.astype(BF16)
    b = b_ref[...]
    q = ((_dot_nt(a, wqv_ref[:Q_DIM, :]) + b[:, :Q_DIM]) * (HEAD_DIM ** -0.5)).astype(BF16)
    k = (_dot(a, wk_ref[...]) + b[:, Q_DIM:Q_DIM + KV_DIM]).astype(BF16)
    v = (_dot_nt(a, wqv_ref[Q_DIM:, :]) + b[:, Q_DIM + KV_DIM:]).astype(BF16)
    km_ref[...] = k
    pad = META_KEY_COLS - N_META
    vcat = jnp.concatenate([v, jnp.zeros((pad, KV_DIM), BF16)], axis=0)
    pieces = []
    for kh in range(N_KV):
        cols = slice(kh * HEAD_DIM, (kh + 1) * HEAD_DIM)
        qs = jnp.concatenate([q[:, (kh * GROUP + g) * HEAD_DIM:(kh * GROUP + g + 1) * HEAD_DIM]
                              for g in range(GROUP)], axis=0)
        kcat = jnp.concatenate([k[:, cols], jnp.zeros((pad, HEAD_DIM), BF16)], axis=0)
        s = _dot_nt(qs, kcat) + tbl_ref[0, kh]
        p = jnp.exp(s - jnp.max(s, axis=-1, keepdims=True))
        l = jnp.sum(p, axis=-1, keepdims=True)
        o = _dot(p.astype(BF16), vcat)[:, cols] / l
        pieces += [o[g * N_META:(g + 1) * N_META] for g in range(GROUP)]
    o_all = jnp.concatenate(pieces, axis=-1).astype(BF16)
    out_ref[...] = x + _dot(o_all, wo_ref[...]) + bo_ref[...]


def _attn_meta_call(h, g, wqvT, bqvT, wk, bk, tbl, wo, bo):
    vmem = pl.BlockSpec(memory_space=pltpu.VMEM)
    return pl.pallas_call(
        _attn_meta_kernel,
        out_shape=(jax.ShapeDtypeStruct((N_META, D_MODEL), F32),
                   jax.ShapeDtypeStruct((N_META, KV_DIM), BF16),
                   jax.ShapeDtypeStruct((KV_DIM, 2 * N_META), BF16)),
        in_specs=[vmem] * 9,
        out_specs=(vmem, vmem, vmem),
        compiler_params=pltpu.CompilerParams(vmem_limit_bytes=VMEM_LIMIT),
        name="attn_meta",
    )(h, g, wqvT, bqvT, wk, bk, tbl, wo, bo)


def _ffn_math(x, g_ref, wg_ref, wu_ref, wd_ref):
    rows = x.shape[0]
    lead = FFN_LEAD_ROWS if rows > 2 * FFN_LEAD_ROWS else rows
    chunks = [x[:lead]] + ([x[lead:]] if lead < rows else [])
    hmids = []
    for xc in chunks:
        a = _rms(xc, g_ref[...]).astype(BF16)
        gate = _dot(a, wg_ref[...])
        up = _dot(a, wu_ref[...])
        hmids.append((gate * (1.0 / (1.0 + jnp.exp(-gate))) * up).astype(BF16))
    outs = [xc + _dot(hm, wd_ref[...]) for xc, hm in zip(chunks, hmids)]
    return outs[0] if len(outs) == 1 else jnp.concatenate(outs, axis=0)


def _ffn_kernel(h_ref, g_ref, wg_ref, wu_ref, wd_ref, gf_ref, out_ref, *, final_norm):
    y = _ffn_math(h_ref[...], g_ref, wg_ref, wu_ref, wd_ref)
    out_ref[...] = _rms(y, gf_ref[...]) if final_norm else y


def _ffn_specs(wg, wu, wd):
    return [_const_spec((1, D_MODEL)), _const_spec(wg.shape), _const_spec(wu.shape), _const_spec(wd.shape)]


def _ffn_call(h, g, wg, wu, wd, gf, tm, final_norm, casts=()):
    n = h.shape[0]
    row = pl.BlockSpec((tm, D_MODEL), lambda i: (i, 0))
    cast_in, cast_out, cast_shapes = _cast_specs(casts, n // tm, lambda i: i)
    in_specs = [row] + _ffn_specs(wg, wu, wd) + [_const_spec((1, D_MODEL))]
    outs = pl.pallas_call(
        _hosting(functools.partial(_ffn_kernel, final_norm=final_norm), len(in_specs), 1, len(casts)),
        grid=(n // tm,),
        out_shape=[jax.ShapeDtypeStruct((n, D_MODEL), F32)] + cast_shapes,
        in_specs=in_specs + cast_in,
        out_specs=[row] + cast_out,
        compiler_params=pltpu.CompilerParams(dimension_semantics=("parallel",), vmem_limit_bytes=VMEM_LIMIT),
        name="ffn",
    )(h, g, wg, wu, wd, gf, *[v for v, _ in casts])
    return outs[0], outs[1:]


def _conv_ffn_kernel(h_ref, g_ref, win_ref, cw_ref, wout_ref, zinit_ref, gf_ref, wg_ref, wu_ref, wd_ref,
                     out_ref, ztail_ref, zprev_scr, *, tm):
    @pl.when(pl.program_id(1) == 0)
    def _():
        zprev_scr[...] = zinit_ref[...]

    x = h_ref[...]
    a = _rms(x, g_ref[...]).astype(BF16)
    bcu = _dot(a, win_ref[...])
    gate_b = bcu[:, :D_MODEL]
    z = bcu[:, D_MODEL:2 * D_MODEL] * bcu[:, 2 * D_MODEL:]
    zp = zprev_scr[...]
    row = lax.broadcasted_iota(jnp.int32, (tm, D_MODEL), 0)
    last, last2 = zp[CONV_TAIL - 1:CONV_TAIL], zp[CONV_TAIL - 2:CONV_TAIL - 1]
    z1 = jnp.where(row == 0, last, pltpu.roll(z, 1, 0))
    z2 = jnp.where(row == 0, last2, jnp.where(row == 1, last, pltpu.roll(z, 2, 0)))
    cw = cw_ref[...]
    conv = cw[0:1] * z2 + cw[1:2] * z1 + cw[2:3] * z
    y = x + _dot((gate_b * conv).astype(BF16), wout_ref[...])
    out_ref[...] = _ffn_math(y, gf_ref, wg_ref, wu_ref, wd_ref)
    tail = z[tm - CONV_TAIL:]
    zprev_scr[...] = tail
    ztail_ref[...] = tail


def _conv_ffn_call(h, g, win, cw, wout, zinit, gf, wg, wu, wd, batch, tm, casts=()):
    n = h.shape[0]
    nt = n // batch // tm
    row = pl.BlockSpec((tm, D_MODEL), lambda b, i: (b * nt + i, 0))
    tail_shape = (CONV_TAIL, D_MODEL)
    cast_in, cast_out, cast_shapes = _cast_specs(casts, batch * nt, lambda b, i: b * nt + i)
    in_specs = [row, _const_spec((1, D_MODEL)), _const_spec(win.shape), _const_spec(cw.shape),
                _const_spec(wout.shape), _const_spec(tail_shape)] + _ffn_specs(wg, wu, wd)
    outs = pl.pallas_call(
        _hosting(functools.partial(_conv_ffn_kernel, tm=tm), len(in_specs), 2, len(casts)),
        grid=(batch, nt),
        out_shape=[jax.ShapeDtypeStruct((n, D_MODEL), F32), jax.ShapeDtypeStruct(tail_shape, F32)] + cast_shapes,
        in_specs=in_specs + cast_in,
        out_specs=[row, pl.BlockSpec(tail_shape, lambda b, i: (0, 0))] + cast_out,
        scratch_shapes=[pltpu.VMEM(tail_shape, F32)],
        compiler_params=pltpu.CompilerParams(dimension_semantics=("arbitrary", "arbitrary"),
                                             vmem_limit_bytes=VMEM_LIMIT),
        name="conv_ffn",
    )(h, g, win, cw, wout, zinit, gf, wg, wu, wd, *[v for v, _ in casts])
    return outs[0], outs[1], outs[2:]


def _pool_ffn_kernel(h_ref, g_ref, wp_ref, sc_ref, hinit_ref, gf_ref, wg_ref, wu_ref, wd_ref,
                     out_ref, atail_ref, hist_scr, *, tm, from_start):
    @pl.when(pl.program_id(1) == 0)
    def _():
        hist_scr[...] = hinit_ref[...]

    x = h_ref[...]
    a = _rms(x, g_ref[...])
    ext = jnp.concatenate([hist_scr[...], a], axis=0)
    outs = []
    for gi, win in enumerate(POOL_WINDOWS):
        cols = slice(gi * POOL_GROUP_DIM, (gi + 1) * POOL_GROUP_DIM)
        s = ext[:, cols]
        k = 1
        while k < win:
            s = s + pltpu.roll(s, k, 0)
            k *= 2
        s = s[POOL_HIST:]
        if from_start:
            pos = lax.broadcasted_iota(jnp.int32, (tm, POOL_GROUP_DIM), 0)
            mean = s / jnp.minimum(win, pos + 1).astype(F32)
        else:
            mean = s * (1.0 / win)
        mix = (mean - a[:, cols]).astype(BF16)
        outs.append(_dot(mix, wp_ref[gi]))
    y = x + jnp.concatenate(outs, axis=-1) * sc_ref[...]
    out_ref[...] = _ffn_math(y, gf_ref, wg_ref, wu_ref, wd_ref)
    tail = a[tm - POOL_HIST:]
    hist_scr[...] = tail
    atail_ref[...] = tail


def _pool_ffn_call(h, g, wp, sc, hinit, gf, wg, wu, wd, batch, tm, from_start):
    n = h.shape[0]
    nt = n // batch // tm
    row = pl.BlockSpec((tm, D_MODEL), lambda b, i: (b * nt + i, 0))
    hist_shape = (POOL_HIST, D_MODEL)
    return pl.pallas_call(
        functools.partial(_pool_ffn_kernel, tm=tm, from_start=from_start),
        grid=(batch, nt),
        out_shape=(jax.ShapeDtypeStruct((n, D_MODEL), F32), jax.ShapeDtypeStruct(hist_shape, F32)),
        in_specs=[row, _const_spec((1, D_MODEL)), _const_spec(wp.shape), _const_spec((1, D_MODEL)),
                  _const_spec(hist_shape)] + _ffn_specs(wg, wu, wd),
        out_specs=(row, pl.BlockSpec(hist_shape, lambda b, i: (0, 0))),
        scratch_shapes=[pltpu.VMEM(hist_shape, F32)],
        compiler_params=pltpu.CompilerParams(dimension_semantics=("arbitrary", "arbitrary"),
                                             vmem_limit_bytes=VMEM_LIMIT),
        name="pool_ffn",
    )(h, g, wp, sc, hinit, gf, wg, wu, wd)


ATTN_TM = 512
ATTN_LOOKAHEAD = 2
ROW_TM = 1024
CONV_TM = 512


def kernel(x, meta_tokens, rel_bias_table, norm_mix, norm_ffn, norm_final, attn_w_qkv, attn_b_qkv, attn_w_o,
           attn_b_o, attn_sinks, conv_w_in, conv_w, conv_w_out, pool_w, pool_scale, ffn_w_gate, ffn_w_up,
           ffn_w_down):
    batch, seq, _ = x.shape
    depth = norm_mix.shape[0]
    n_mixers = 3
    hr = x.reshape(batch * seq, D_MODEL)
    hm = meta_tokens.astype(F32)
    row = lambda v: v.reshape(1, -1)
    meta_codes = _meta_codes()

    def ffn_casts(layer):
        return [(w, layer) for w in (ffn_w_gate, ffn_w_up, ffn_w_down)]

    def next_layer_casts(layer):
        nxt = layer + 1
        if nxt >= depth or nxt % n_mixers == 0:
            return []
        conv = [(w, nxt // n_mixers) for w in (conv_w_in, conv_w_out)] if nxt % n_mixers == 1 else []
        return conv + ffn_casts(nxt)

    def unpack_next(layer, cast):
        nonlocal wg, wu, wd, conv_bf16
        if not cast:
            return
        if (layer + 1) % n_mixers == 1:
            conv_bf16, cast = cast[:2], cast[2:]
        wg, wu, wd = cast

    wg = wu = wd = conv_bf16 = None
    for i in range(depth):
        kind, j = i % n_mixers, i // n_mixers
        last = i == depth - 1
        gm, gf = row(norm_mix[i]), row(norm_ffn[i])
        if kind == 0:
            w, b = attn_w_qkv[j], attn_b_qkv[j]
            wqvT = jnp.concatenate([w[:, :Q_DIM], w[:, Q_DIM + KV_DIM:]], axis=1).T.astype(BF16)
            bqvT = jnp.broadcast_to(jnp.concatenate([b[:Q_DIM], b[Q_DIM + KV_DIM:]])[:, None], (Q_DIM + KV_DIM, LANES))
            wk, bk = w[:, Q_DIM:Q_DIM + KV_DIM].astype(BF16), row(b[Q_DIM:Q_DIM + KV_DIM])
            wo, bo = attn_w_o[j].astype(BF16), row(attn_b_o[j])
            tbl_meta = _meta_table(meta_codes, rel_bias_table, attn_sinks[j])
            tbl_real = _real_table(rel_bias_table, attn_sinks[j])
            hm, km, vmT = _attn_meta_call(hm, gm, wqvT, bqvT, wk, row(b), tbl_meta, wo, bo)
            hr, (wg, wu, wd) = _attn_call(hr, gm, wqvT, bqvT, wk, bk, km, vmT, tbl_real, wo, bo, batch, ATTN_TM,
                                          ffn_casts(i))
        elif kind == 1:
            win, wout = conv_bf16
            hm, ztail, _ = _conv_ffn_call(hm, gm, win, conv_w[j], wout, jnp.zeros((CONV_TAIL, D_MODEL), F32),
                                          gf, wg, wu, wd, 1, N_META)
            hr, _, cast = _conv_ffn_call(hr, gm, win, conv_w[j], wout, ztail, gf, wg, wu, wd, batch, CONV_TM,
                                         next_layer_casts(i))
            unpack_next(i, cast)
            continue
        else:
            wp, sc = pool_w[j].astype(BF16), row(pool_scale[j])
            hm, atail = _pool_ffn_call(hm, gm, wp, sc, jnp.zeros((POOL_HIST, D_MODEL), F32),
                                       gf, wg, wu, wd, 1, N_META, True)
            hr, _ = _pool_ffn_call(hr, gm, wp, sc, atail, gf, wg, wu, wd, batch, ROW_TM, False)
            wg = wu = wd = None
            continue
        if not last:
            hm, _ = _ffn_call(hm, gf, wg, wu, wd, row(norm_final), N_META, False)
        hr, cast = _ffn_call(hr, gf, wg, wu, wd, row(norm_final), ROW_TM, last, next_layer_casts(i))
        wg = wu = wd = None
        unpack_next(i, cast)
    return hr.reshape(batch, seq, D_MODEL)
```

```python
import functools
import math

import numpy as np
import jax
import jax.numpy as jnp
from jax import lax
from jax.experimental import pallas as pl
from jax.experimental.pallas import tpu as pltpu

F32 = jnp.float32
BF16 = jnp.bfloat16

D_MODEL = 1024
N_META = 16
RMS_EPS = 1e-6
N_HEADS = 16
N_KV = 4
HEAD_DIM = 64
GROUP = N_HEADS // N_KV
Q_DIM = N_HEADS * HEAD_DIM
KV_DIM = N_KV * HEAD_DIM
WINDOW = 128
N_BUCKETS = 32
MAX_DISTANCE = 128
POOL_WINDOWS = (2, 4, 8, 16)
POOL_GROUP_DIM = D_MODEL // len(POOL_WINDOWS)
POOL_HIST = 16
CONV_TAIL = 8
LANES = 128
FFN_LEAD_ROWS = 256
BF16_SUBLANES = 16
LOG2E = math.log2(math.e)

Q_BLK = 128
BAND_ROWS = WINDOW + Q_BLK
SINK_ROW = BAND_ROWS + N_META
KEY_ROWS = SINK_ROW + N_META
N_NEAR = WINDOW // Q_BLK
ONES_ROWS = 16

META_KEY_COLS = 128
CODE_SINK = N_BUCKETS
CODE_MASKED = N_BUCKETS + 1

VMEM_LIMIT = 56 * 1024 * 1024


def _rel_bucket_np(dist):
    max_exact = N_BUCKETS // 2
    d = np.maximum(dist, 0)
    df = np.maximum(d, 1).astype(np.float64)
    large = max_exact + (np.log(df / max_exact) / math.log(MAX_DISTANCE / max_exact)
                         * (N_BUCKETS - max_exact)).astype(np.int64)
    large = np.minimum(large, N_BUCKETS - 1)
    return np.where(d < max_exact, d, large).astype(np.int32)


def _meta_codes():
    i = np.arange(N_META)[:, None]
    m = np.arange(N_META)[None, :]
    code = np.full((N_META, META_KEY_COLS), CODE_MASKED, np.int32)
    code[:, :N_META] = np.where(i >= m, _rel_bucket_np(i - m), CODE_MASKED)
    code[:, N_META] = CODE_SINK
    return code[None]


def _band_codes():
    return np.stack([_rel_bucket_np(np.arange(WINDOW)), np.zeros(WINDOW, np.int32)])


def _near_meta_codes():
    m = np.arange(N_META)[:, None]
    i = np.arange(Q_BLK)[None, :]
    return np.stack([_rel_bucket_np(N_META + n * Q_BLK + i - m) for n in range(N_NEAR)])


def _const_spec(shape):
    zeros = (0,) * len(shape)
    return pl.BlockSpec(shape, lambda *_: zeros, pipeline_mode=pl.Buffered(1))


def _hosting(kernel_fn, n_in, n_out, n_cast):
    def wrapped(*refs):
        ins, rest = refs[:n_in], refs[n_in:]
        cast_in, rest = rest[:n_cast], rest[n_cast:]
        outs, rest = rest[:n_out], rest[n_out:]
        cast_out, scratch = rest[:n_cast], rest[n_cast:]
        for src, dst in zip(cast_in, cast_out):
            dst[...] = src[...].astype(BF16)
        kernel_fn(*ins, *outs, *scratch)
    return wrapped


def _cast_specs(casts, steps, flat_step):
    in_specs, out_specs, out_shapes = [], [], []
    for w, layer in casts:
        _, rows, cols = w.shape
        n_slices = max(n for n in range(1, steps + 1)
                       if steps % n == 0 and rows % n == 0 and (rows // n) % BF16_SUBLANES == 0)
        per = steps // n_slices
        blk = rows // n_slices
        in_specs.append(pl.BlockSpec((None, blk, cols), lambda *idx, l=layer, p=per: (l, flat_step(*idx) // p, 0)))
        out_specs.append(pl.BlockSpec((blk, cols), lambda *idx, p=per: (flat_step(*idx) // p, 0)))
        out_shapes.append(jax.ShapeDtypeStruct((rows, cols), BF16))
    return in_specs, out_specs, out_shapes


def _smem_spec():
    return pl.BlockSpec(memory_space=pltpu.SMEM)


def _rms(x, g):
    return x * lax.rsqrt(jnp.mean(x * x, axis=-1, keepdims=True) + RMS_EPS) * g


def _dot(a, b):
    return jnp.dot(a, b, preferred_element_type=F32)


def _dot_nt(a, b):
    return lax.dot_general(a, b, (((1,), (1,)), ((), ())), preferred_element_type=F32)


def _dot_tn(a, b):
    return lax.dot_general(a, b, (((0,), (0,)), ((), ())), preferred_element_type=F32)


def _select_by_code(code, tbl_ref, h, otherwise):
    val = otherwise
    for b in range(N_BUCKETS):
        val = jnp.where(code == b, tbl_ref[b, h], val)
    return val


def _meta_table_kernel(code_ref, tbl_ref, sink_ref, out_ref):
    n_var, rows, _ = code_ref.shape
    for v in range(n_var):
        code = code_ref[v]
        for h in range(N_HEADS):
            kh, g = divmod(h, GROUP)
            val = _select_by_code(code, tbl_ref, h, jnp.where(code == CODE_SINK, sink_ref[h], -jnp.inf))
            out_ref[v, kh, g * rows:(g + 1) * rows, :] = val


def _meta_table(codes, rel_table, sinks):
    n_var, rows, cols = codes.shape
    return pl.pallas_call(
        _meta_table_kernel,
        out_shape=jax.ShapeDtypeStruct((n_var, N_KV, GROUP * rows, cols), F32),
        in_specs=[pl.BlockSpec(memory_space=pltpu.VMEM), _smem_spec(), _smem_spec()],
        out_specs=pl.BlockSpec(memory_space=pltpu.VMEM),
        name="meta_bias_table",
    )(jnp.asarray(codes), rel_table, sinks)


def _real_table_kernel(band_code_ref, meta_code_ref, tbl_ref, sink_ref, out_ref):
    neg = -jnp.inf
    band_code = band_code_ref[0:1, :]
    t = lax.broadcasted_iota(jnp.int32, (BAND_ROWS, WINDOW), 0)
    i = lax.broadcasted_iota(jnp.int32, (BAND_ROWS, WINDOW), 1)
    in_window = (i < t) & (t <= i + WINDOW)
    row16 = lax.broadcasted_iota(jnp.int32, (N_META, Q_BLK), 0)
    for h in range(N_HEADS):
        kh, g = divmod(h, GROUP)
        lanes = slice(g * Q_BLK, (g + 1) * Q_BLK)
        by_dist = _select_by_code(band_code, tbl_ref, h, jnp.zeros((1, WINDOW), F32)) * LOG2E
        rolled = pltpu.roll(jnp.broadcast_to(by_dist, (WINDOW, WINDOW)), 0, 1, stride=1, stride_axis=0)
        rolled = jnp.concatenate([rolled, rolled[:BAND_ROWS - WINDOW]], axis=0)
        far = jnp.full((N_META, Q_BLK), tbl_ref[N_BUCKETS - 1, h], F32)
        sink_pad = jnp.where(row16 == 0, sink_ref[h] * LOG2E, neg)
        for v in range(1 + N_NEAR):
            if v == 0:
                valid, meta = in_window, far * LOG2E
            else:
                first_key_row = WINDOW - (v - 1) * Q_BLK
                valid = in_window & (t >= first_key_row)
                meta = _select_by_code(meta_code_ref[v - 1], tbl_ref, h, far) * LOG2E
            out_ref[v, kh, 0:BAND_ROWS, lanes] = jnp.where(valid, rolled, neg)[:, :Q_BLK]
            out_ref[v, kh, BAND_ROWS:SINK_ROW, lanes] = meta
            out_ref[v, kh, SINK_ROW:KEY_ROWS, lanes] = sink_pad


def _real_table(rel_table, sinks):
    vmem = pl.BlockSpec(memory_space=pltpu.VMEM)
    return pl.pallas_call(
        _real_table_kernel,
        out_shape=jax.ShapeDtypeStruct((1 + N_NEAR, N_KV, KEY_ROWS, GROUP * Q_BLK), F32),
        in_specs=[vmem, vmem, _smem_spec(), _smem_spec()],
        out_specs=vmem,
        name="real_bias_table",
    )(jnp.asarray(_band_codes()), jnp.asarray(_near_meta_codes()), rel_table, sinks)


def _attn_kernel(h_ref, g_ref, wqv_ref, bqv_ref, wk_ref, bk_ref, km_ref, vmT_ref, tbl_ref, wo_ref, bo_ref,
                 out_ref, qT_scr, k_scr, vT_scr, oT_scr, *, tm):
    first_tile = pl.program_id(1) == 0

    @pl.when(first_tile)
    def _():
        k_scr[0:WINDOW, :] = jnp.zeros((WINDOW, KV_DIM), BF16)
        vT_scr[:, 0:WINDOW] = jnp.zeros((KV_DIM, WINDOW), BF16)

    @pl.when(jnp.logical_not(first_tile))
    def _():
        k_scr[0:WINDOW, :] = k_scr[tm:tm + WINDOW, :]
        vT_scr[:, 0:WINDOW] = vT_scr[:, tm:tm + WINDOW]

    x = h_ref[...]
    a = _rms(x, g_ref[...]).astype(BF16)
    bias = jnp.concatenate([bqv_ref[...]] * (tm // LANES), axis=1)
    qvT = _dot_nt(wqv_ref[...], a) + bias
    qT_scr[...] = (qvT[:Q_DIM] * (HEAD_DIM ** -0.5 * LOG2E)).astype(BF16)
    vT_scr[:, WINDOW:] = qvT[Q_DIM:].astype(BF16)
    k_scr[WINDOW:, :] = (_dot(a, wk_ref[...]) + bk_ref[...]).astype(BF16)

    kext = k_scr[...]
    vText = vT_scr[...]
    km = km_ref[...]
    vmT = vmT_ref[...]
    zk = jnp.zeros((KEY_ROWS - SINK_ROW, HEAD_DIM), BF16)
    ones = jnp.ones((ONES_ROWS, KEY_ROWS), BF16)

    def scores(j, kh):
        r0 = j * Q_BLK
        cols = slice(kh * HEAD_DIM, (kh + 1) * HEAD_DIM)
        qsT = jnp.concatenate(
            [qT_scr[(kh * GROUP + g) * HEAD_DIM:(kh * GROUP + g + 1) * HEAD_DIM, r0:r0 + Q_BLK]
             for g in range(GROUP)], axis=1)
        keys = jnp.concatenate([kext[r0:r0 + BAND_ROWS, cols], km[:, cols], zk], axis=0)
        return _dot(keys, qsT)

    def finish(j, kh, sT, sT_next):
        r0 = j * Q_BLK
        cols = slice(kh * HEAD_DIM, (kh + 1) * HEAD_DIM)
        var = jnp.where(first_tile, 1 + j, 0) if j < N_NEAR else 0
        sT = sT + tbl_ref[var, kh]
        m = jnp.max(sT, axis=0, keepdims=True)
        pT = jnp.exp2(sT - m).astype(BF16)
        valsT = jnp.concatenate([vText[cols, r0:r0 + BAND_ROWS], vmT[cols, :]], axis=1)
        den = ones
        if sT_next is not None:
            bits = pltpu.bitcast(sT_next[KEY_ROWS - ONES_ROWS:, :LANES], jnp.uint32)
            zero = pltpu.bitcast((bits >> 16) >> 16, F32).astype(BF16)
            den = jnp.concatenate([ones[:, :LANES] + zero, ones[:, LANES:]], axis=1)
        oT = _dot(jnp.concatenate([valsT, den], axis=0), pT)
        oT = oT[:HEAD_DIM] / oT[HEAD_DIM:HEAD_DIM + 1]
        for g in range(GROUP):
            h0 = (kh * GROUP + g) * HEAD_DIM
            oT_scr[h0:h0 + HEAD_DIM, r0:r0 + Q_BLK] = oT[:, g * Q_BLK:(g + 1) * Q_BLK].astype(BF16)

    units = [(j, kh) for j in range(tm // Q_BLK) for kh in range(N_KV)]
    pending = [scores(*u) for u in units[:ATTN_LOOKAHEAD]]
    for idx, unit in enumerate(units):
        ahead = idx + ATTN_LOOKAHEAD
        if ahead < len(units):
            pending.append(scores(*units[ahead]))
        sT = pending.pop(0)
        finish(*unit, sT, pending[-1] if pending else None)
    out_ref[...] = x + _dot_tn(oT_scr[...], wo_ref[...]) + bo_ref[...]


def _attn_call(h, g, wqvT, bqvT, wk, bk, km, vmT, tbl, wo, bo, batch, tm, casts=()):
    n = h.shape[0]
    nt = n // batch // tm
    row = pl.BlockSpec((tm, D_MODEL), lambda b, i: (b * nt + i, 0))
    cast_in, cast_out, cast_shapes = _cast_specs(casts, batch * nt, lambda b, i: b * nt + i)
    in_specs = [row, _const_spec((1, D_MODEL)), _const_spec(wqvT.shape), _const_spec(bqvT.shape),
                _const_spec(wk.shape), _const_spec((1, KV_DIM)), _const_spec(km.shape), _const_spec(vmT.shape),
                _const_spec(tbl.shape), _const_spec(wo.shape), _const_spec((1, D_MODEL))]
    outs = pl.pallas_call(
        _hosting(functools.partial(_attn_kernel, tm=tm), len(in_specs), 1, len(casts)),
        grid=(batch, nt),
        out_shape=[jax.ShapeDtypeStruct((n, D_MODEL), F32)] + cast_shapes,
        in_specs=in_specs + cast_in,
        out_specs=[row] + cast_out,
        scratch_shapes=[pltpu.VMEM((Q_DIM, tm), BF16), pltpu.VMEM((WINDOW + tm, KV_DIM), BF16),
                        pltpu.VMEM((KV_DIM, WINDOW + tm), BF16), pltpu.VMEM((Q_DIM, tm), BF16)],
        compiler_params=pltpu.CompilerParams(dimension_semantics=("arbitrary", "arbitrary"),
                                             vmem_limit_bytes=VMEM_LIMIT),
        name="attn",
    )(h, g, wqvT, bqvT, wk, bk, km, vmT, tbl, wo, bo, *[v for v, _ in casts])
    return outs[0], outs[1:]


def _attn_meta_kernel(h_ref, g_ref, wqv_ref, bqv_ref, wk_ref, b_ref, tbl_ref, wo_ref, bo_ref,
                      out_ref, km_ref, vmT_ref):
    x = h_ref[...]
    a = _rms(x, g_ref[...]).astype(BF16)
    a2 = jnp.concatenate([a, jnp.zeros_like(a)], axis=0)
    lane = lax.broadcasted_iota(jnp.int32, (KV_DIM, 2 * N_META), 1)
    vT = _dot_nt(wqv_ref[Q_DIM:, :], a2) + jnp.where(lane < N_META, bqv_ref[Q_DIM:, :2 * N_META], 0.0)
    vmT_ref[...] = vT.astype(BF16)
    b = b_ref[...]
    q = ((_dot_nt(a, wqv_ref[:Q_DIM, :]) + b[:, :Q_DIM]) * (HEAD_DIM ** -0.5)).astype(BF16)
    k = (_dot(a, wk_ref[...]) + b[:, Q_DIM:Q_DIM + KV_DIM]).astype(BF16)
    v = (_dot_nt(a, wqv_ref[Q_DIM:, :]) + b[:, Q_DIM + KV_DIM:]).astype(BF16)
    km_ref[...] = k
    pad = META_KEY_COLS - N_META
    vcat = jnp.concatenate([v, jnp.zeros((pad, KV_DIM), BF16)], axis=0)
    pieces = []
    for kh in range(N_KV):
        cols = slice(kh * HEAD_DIM, (kh + 1) * HEAD_DIM)
        qs = jnp.concatenate([q[:, (kh * GROUP + g) * HEAD_DIM:(kh * GROUP + g + 1) * HEAD_DIM]
                              for g in range(GROUP)], axis=0)
        kcat = jnp.concatenate([k[:, cols], jnp.zeros((pad, HEAD_DIM), BF16)], axis=0)
        s = _dot_nt(qs, kcat) + tbl_ref[0, kh]
        p = jnp.exp(s - jnp.max(s, axis=-1, keepdims=True))
        l = jnp.sum(p, axis=-1, keepdims=True)
        o = _dot(p.astype(BF16), vcat)[:, cols] / l
        pieces += [o[g * N_META:(g + 1) * N_META] for g in range(GROUP)]
    o_all = jnp.concatenate(pieces, axis=-1).astype(BF16)
    out_ref[...] = x + _dot(o_all, wo_ref[...]) + bo_ref[...]


def _attn_meta_call(h, g, wqvT, bqvT, wk, bk, tbl, wo, bo):
    vmem = pl.BlockSpec(memory_space=pltpu.VMEM)
    return pl.pallas_call(
        _attn_meta_kernel,
        out_shape=(jax.ShapeDtypeStruct((N_META, D_MODEL), F32),
                   jax.ShapeDtypeStruct((N_META, KV_DIM), BF16),
                   jax.ShapeDtypeStruct((KV_DIM, 2 * N_META), BF16)),
        in_specs=[vmem] * 9,
        out_specs=(vmem, vmem, vmem),
        compiler_params=pltpu.CompilerParams(vmem_limit_bytes=VMEM_LIMIT),
        name="attn_meta",
    )(h, g, wqvT, bqvT, wk, bk, tbl, wo, bo)


def _ffn_math(x, g_ref, wg_ref, wu_ref, wd_ref):
    rows = x.shape[0]
    lead = FFN_LEAD_ROWS if rows > 2 * FFN_LEAD_ROWS else rows
    chunks = [x[:lead]] + ([x[lead:]] if lead < rows else [])
    hmids = []
    for xc in chunks:
        a = _rms(xc, g_ref[...]).astype(BF16)
        gate = _dot(a, wg_ref[...])
        up = _dot(a, wu_ref[...])
        hmids.append((gate * (1.0 / (1.0 + jnp.exp(-gate))) * up).astype(BF16))
    outs = [xc + _dot(hm, wd_ref[...]) for xc, hm in zip(chunks, hmids)]
    return outs[0] if len(outs) == 1 else jnp.concatenate(outs, axis=0)


def _fold_meta(run, h_ref, hm_ref, out_ref, hm_out_ref, first):
    @pl.when(first)
    def _():
        y = run(jnp.concatenate([hm_ref[...], h_ref[...]], axis=0), True)
        hm_out_ref[...] = y[:N_META]
        out_ref[...] = y[N_META:]

    @pl.when(jnp.logical_not(first))
    def _():
        out_ref[...] = run(h_ref[...], False)


def _meta_specs():
    return _const_spec((N_META, D_MODEL)), pl.BlockSpec((N_META, D_MODEL), lambda *_: (0, 0))


def _ffn_kernel(*refs, final_norm, with_meta):
    if with_meta:
        h_ref, hm_ref, g_ref, wg_ref, wu_ref, wd_ref, gf_ref, out_ref, hm_out_ref = refs
    else:
        h_ref, g_ref, wg_ref, wu_ref, wd_ref, gf_ref, out_ref = refs

    def run(x, joint):
        y = _ffn_math(x, g_ref, wg_ref, wu_ref, wd_ref)
        return _rms(y, gf_ref[...]) if final_norm else y

    if with_meta:
        _fold_meta(run, h_ref, hm_ref, out_ref, hm_out_ref, pl.program_id(0) == 0)
    else:
        out_ref[...] = run(h_ref[...], False)


def _ffn_specs(wg, wu, wd):
    return [_const_spec((1, D_MODEL)), _const_spec(wg.shape), _const_spec(wu.shape), _const_spec(wd.shape)]


def _ffn_call(h, hm, g, wg, wu, wd, gf, tm, final_norm, casts=()):
    n = h.shape[0]
    with_meta = hm is not None
    row = pl.BlockSpec((tm, D_MODEL), lambda i: (i, 0))
    meta_in, meta_out = _meta_specs()
    cast_in, cast_out, cast_shapes = _cast_specs(casts, n // tm, lambda i: i)
    in_specs = [row] + ([meta_in] if with_meta else []) + _ffn_specs(wg, wu, wd) + [_const_spec((1, D_MODEL))]
    out_specs = [row] + ([meta_out] if with_meta else [])
    out_shapes = [jax.ShapeDtypeStruct((n, D_MODEL), F32)]
    out_shapes += [jax.ShapeDtypeStruct((N_META, D_MODEL), F32)] if with_meta else []
    outs = pl.pallas_call(
        _hosting(functools.partial(_ffn_kernel, final_norm=final_norm, with_meta=with_meta),
                 len(in_specs), len(out_specs), len(casts)),
        grid=(n // tm,),
        out_shape=out_shapes + cast_shapes,
        in_specs=in_specs + cast_in,
        out_specs=out_specs + cast_out,
        compiler_params=pltpu.CompilerParams(dimension_semantics=("arbitrary",), vmem_limit_bytes=VMEM_LIMIT),
        name="ffn",
    )(h, *([hm] if with_meta else []), g, wg, wu, wd, gf, *[v for v, _ in casts])
    n_out = len(out_specs)
    return outs[0], (outs[1] if with_meta else None), outs[n_out:]


def _conv_ffn_kernel(h_ref, hm_ref, g_ref, win_ref, cw_ref, wout_ref, gf_ref, wg_ref, wu_ref, wd_ref,
                     out_ref, hm_out_ref, zprev_scr, zmeta_scr):
    seq_start = pl.program_id(1) == 0

    def run(x, joint):
        rows = x.shape[0]
        a = _rms(x, g_ref[...]).astype(BF16)
        bcu = _dot(a, win_ref[...])
        gate_b = bcu[:, :D_MODEL]
        z = bcu[:, D_MODEL:2 * D_MODEL] * bcu[:, 2 * D_MODEL:]
        if joint:
            zp = jnp.zeros((CONV_TAIL, D_MODEL), F32)
        else:
            zp = jnp.where(seq_start, zmeta_scr[...], zprev_scr[...])
        row = lax.broadcasted_iota(jnp.int32, (rows, D_MODEL), 0)
        last, last2 = zp[CONV_TAIL - 1:CONV_TAIL], zp[CONV_TAIL - 2:CONV_TAIL - 1]
        z1 = jnp.where(row == 0, last, pltpu.roll(z, 1, 0))
        z2 = jnp.where(row == 0, last2, jnp.where(row == 1, last, pltpu.roll(z, 2, 0)))
        cw = cw_ref[...]
        conv = cw[0:1] * z2 + cw[1:2] * z1 + cw[2:3] * z
        y = x + _dot((gate_b * conv).astype(BF16), wout_ref[...])
        if joint:
            zmeta_scr[...] = z[N_META - CONV_TAIL:N_META]
        zprev_scr[...] = z[rows - CONV_TAIL:]
        return _ffn_math(y, gf_ref, wg_ref, wu_ref, wd_ref)

    first = (pl.program_id(0) == 0) & seq_start
    _fold_meta(run, h_ref, hm_ref, out_ref, hm_out_ref, first)


def _conv_ffn_call(h, hm, g, win, cw, wout, gf, wg, wu, wd, batch, tm, casts=()):
    n = h.shape[0]
    nt = n // batch // tm
    row = pl.BlockSpec((tm, D_MODEL), lambda b, i: (b * nt + i, 0))
    meta_in, meta_out = _meta_specs()
    tail_shape = (CONV_TAIL, D_MODEL)
    cast_in, cast_out, cast_shapes = _cast_specs(casts, batch * nt, lambda b, i: b * nt + i)
    in_specs = [row, meta_in, _const_spec((1, D_MODEL)), _const_spec(win.shape), _const_spec(cw.shape),
                _const_spec(wout.shape)] + _ffn_specs(wg, wu, wd)
    outs = pl.pallas_call(
        _hosting(_conv_ffn_kernel, len(in_specs), 2, len(casts)),
        grid=(batch, nt),
        out_shape=[jax.ShapeDtypeStruct((n, D_MODEL), F32), jax.ShapeDtypeStruct((N_META, D_MODEL), F32)] + cast_shapes,
        in_specs=in_specs + cast_in,
        out_specs=[row, meta_out] + cast_out,
        scratch_shapes=[pltpu.VMEM(tail_shape, F32), pltpu.VMEM(tail_shape, F32)],
        compiler_params=pltpu.CompilerParams(dimension_semantics=("arbitrary", "arbitrary"),
                                             vmem_limit_bytes=VMEM_LIMIT),
        name="conv_ffn",
    )(h, hm, g, win, cw, wout, gf, wg, wu, wd, *[v for v, _ in casts])
    return outs[0], outs[1], outs[2:]


def _pool_ffn_kernel(h_ref, hm_ref, g_ref, wp_ref, sc_ref, gf_ref, wg_ref, wu_ref, wd_ref,
                     out_ref, hm_out_ref, hist_scr, ameta_scr):
    seq_start = pl.program_id(1) == 0

    def run(x, joint):
        rows = x.shape[0]
        a = _rms(x, g_ref[...])
        if joint:
            hist = jnp.zeros((POOL_HIST, D_MODEL), F32)
        else:
            hist = jnp.where(seq_start, ameta_scr[...], hist_scr[...])
        ext = jnp.concatenate([hist, a], axis=0)
        outs = []
        for gi, win in enumerate(POOL_WINDOWS):
            cols = slice(gi * POOL_GROUP_DIM, (gi + 1) * POOL_GROUP_DIM)
            s = ext[:, cols]
            k = 1
            while k < win:
                s = s + pltpu.roll(s, k, 0)
                k *= 2
            s = s[POOL_HIST:]
            if joint:
                pos = lax.broadcasted_iota(jnp.int32, (rows, POOL_GROUP_DIM), 0)
                mean = s / jnp.minimum(win, pos + 1).astype(F32)
            else:
                mean = s * (1.0 / win)
            mix = (mean - a[:, cols]).astype(BF16)
            outs.append(_dot(mix, wp_ref[gi]))
        y = x + jnp.concatenate(outs, axis=-1) * sc_ref[...]
        if joint:
            ameta_scr[...] = a[:N_META]
        hist_scr[...] = a[rows - POOL_HIST:]
        return _ffn_math(y, gf_ref, wg_ref, wu_ref, wd_ref)

    first = (pl.program_id(0) == 0) & seq_start
    _fold_meta(run, h_ref, hm_ref, out_ref, hm_out_ref, first)


def _pool_ffn_call(h, hm, g, wp, sc, gf, wg, wu, wd, batch, tm):
    n = h.shape[0]
    nt = n // batch // tm
    row = pl.BlockSpec((tm, D_MODEL), lambda b, i: (b * nt + i, 0))
    meta_in, meta_out = _meta_specs()
    return pl.pallas_call(
        _pool_ffn_kernel,
        grid=(batch, nt),
        out_shape=(jax.ShapeDtypeStruct((n, D_MODEL), F32), jax.ShapeDtypeStruct((N_META, D_MODEL), F32)),
        in_specs=[row, meta_in, _const_spec((1, D_MODEL)), _const_spec(wp.shape), _const_spec((1, D_MODEL))]
        + _ffn_specs(wg, wu, wd),
        out_specs=(row, meta_out),
        scratch_shapes=[pltpu.VMEM((POOL_HIST, D_MODEL), F32), pltpu.VMEM((N_META, D_MODEL), F32)],
        compiler_params=pltpu.CompilerParams(dimension_semantics=("arbitrary", "arbitrary"),
                                             vmem_limit_bytes=VMEM_LIMIT),
        name="pool_ffn",
    )(h, hm, g, wp, sc, gf, wg, wu, wd)


ATTN_TM = 512
ATTN_LOOKAHEAD = 2
ROW_TM = 1024
CONV_TM = 512


def kernel(x, meta_tokens, rel_bias_table, norm_mix, norm_ffn, norm_final, attn_w_qkv, attn_b_qkv, attn_w_o,
           attn_b_o, attn_sinks, conv_w_in, conv_w, conv_w_out, pool_w, pool_scale, ffn_w_gate, ffn_w_up,
           ffn_w_down):
    batch, seq, _ = x.shape
    depth = norm_mix.shape[0]
    n_mixers = 3
    hr = x.reshape(batch * seq, D_MODEL)
    hm = meta_tokens.astype(F32)
    row = lambda v: v.reshape(1, -1)
    meta_codes = _meta_codes()

    def ffn_casts(layer):
        return [(w, layer) for w in (ffn_w_gate, ffn_w_up, ffn_w_down)]

    def next_layer_casts(layer):
        nxt = layer + 1
        if nxt >= depth or nxt % n_mixers == 0:
            return []
        conv = [(w, nxt // n_mixers) for w in (conv_w_in, conv_w_out)] if nxt % n_mixers == 1 else []
        return conv + ffn_casts(nxt)

    def unpack_next(layer, cast):
        nonlocal wg, wu, wd, conv_bf16
        if not cast:
            return
        if (layer + 1) % n_mixers == 1:
            conv_bf16, cast = cast[:2], cast[2:]
        wg, wu, wd = cast

    wg = wu = wd = conv_bf16 = None
    for i in range(depth):
        kind, j = i % n_mixers, i // n_mixers
        last = i == depth - 1
        gm, gf = row(norm_mix[i]), row(norm_ffn[i])
        if kind == 0:
            w, b = attn_w_qkv[j], attn_b_qkv[j]
            wqvT = jnp.concatenate([w[:, :Q_DIM], w[:, Q_DIM + KV_DIM:]], axis=1).T.astype(BF16)
            bqvT = jnp.broadcast_to(jnp.concatenate([b[:Q_DIM], b[Q_DIM + KV_DIM:]])[:, None], (Q_DIM + KV_DIM, LANES))
            wk, bk = w[:, Q_DIM:Q_DIM + KV_DIM].astype(BF16), row(b[Q_DIM:Q_DIM + KV_DIM])
            wo, bo = attn_w_o[j].astype(BF16), row(attn_b_o[j])
            tbl_meta = _meta_table(meta_codes, rel_bias_table, attn_sinks[j])
            tbl_real = _real_table(rel_bias_table, attn_sinks[j])
            hm, km, vmT = _attn_meta_call(hm, gm, wqvT, bqvT, wk, row(b), tbl_meta, wo, bo)
            hr, (wg, wu, wd) = _attn_call(hr, gm, wqvT, bqvT, wk, bk, km, vmT, tbl_real, wo, bo, batch, ATTN_TM,
                                          ffn_casts(i))
        elif kind == 1:
            win, wout = conv_bf16
            hr, hm, cast = _conv_ffn_call(hr, hm, gm, win, conv_w[j], wout, gf, wg, wu, wd, batch, CONV_TM,
                                          next_layer_casts(i))
            unpack_next(i, cast)
            continue
        else:
            wp, sc = pool_w[j].astype(BF16), row(pool_scale[j])
            hr, hm = _pool_ffn_call(hr, hm, gm, wp, sc, gf, wg, wu, wd, batch, ROW_TM)
            continue
        hr, hm, cast = _ffn_call(hr, None if last else hm, gf, wg, wu, wd, row(norm_final), ROW_TM, last,
                                 next_layer_casts(i))
        unpack_next(i, cast)
    return hr.reshape(batch, seq, D_MODEL)
```

```python
import functools
import math

import numpy as np
import jax
import jax.numpy as jnp
from jax import lax
from jax.experimental import pallas as pl
from jax.experimental.pallas import tpu as pltpu

F32 = jnp.float32
BF16 = jnp.bfloat16

D_MODEL = 1024
N_META = 16
RMS_EPS = 1e-6
N_HEADS = 16
N_KV = 4
HEAD_DIM = 64
GROUP = N_HEADS // N_KV
Q_DIM = N_HEADS * HEAD_DIM
KV_DIM = N_KV * HEAD_DIM
WINDOW = 128
N_BUCKETS = 32
MAX_DISTANCE = 128
POOL_WINDOWS = (2, 4, 8, 16)
POOL_GROUP_DIM = D_MODEL // len(POOL_WINDOWS)
POOL_HIST = 16
CONV_TAIL = 8
LANES = 128
FFN_LEAD_ROWS = 256
BF16_SUBLANES = 16
LOG2E = math.log2(math.e)

Q_BLK = 128
BAND_ROWS = WINDOW + Q_BLK
SINK_ROW = BAND_ROWS + N_META
KEY_ROWS = SINK_ROW + N_META
N_NEAR = WINDOW // Q_BLK
ONES_ROWS = 16

META_KEY_COLS = 128
CODE_SINK = N_BUCKETS
CODE_MASKED = N_BUCKETS + 1

VMEM_LIMIT = 56 * 1024 * 1024


def _rel_bucket_np(dist):
    max_exact = N_BUCKETS // 2
    d = np.maximum(dist, 0)
    df = np.maximum(d, 1).astype(np.float64)
    large = max_exact + (np.log(df / max_exact) / math.log(MAX_DISTANCE / max_exact)
                         * (N_BUCKETS - max_exact)).astype(np.int64)
    large = np.minimum(large, N_BUCKETS - 1)
    return np.where(d < max_exact, d, large).astype(np.int32)


def _meta_codes():
    i = np.arange(N_META)[:, None]
    m = np.arange(N_META)[None, :]
    code = np.full((N_META, META_KEY_COLS), CODE_MASKED, np.int32)
    code[:, :N_META] = np.where(i >= m, _rel_bucket_np(i - m), CODE_MASKED)
    code[:, N_META] = CODE_SINK
    return code[None]


def _band_codes():
    return np.stack([_rel_bucket_np(np.arange(WINDOW)), np.zeros(WINDOW, np.int32)])


def _near_meta_codes():
    m = np.arange(N_META)[:, None]
    i = np.arange(Q_BLK)[None, :]
    return np.stack([_rel_bucket_np(N_META + n * Q_BLK + i - m) for n in range(N_NEAR)])


def _const_spec(shape):
    zeros = (0,) * len(shape)
    return pl.BlockSpec(shape, lambda *_: zeros, pipeline_mode=pl.Buffered(1))


def _hosting(kernel_fn, n_in, n_out, n_cast):
    def wrapped(*refs):
        ins, rest = refs[:n_in], refs[n_in:]
        cast_in, rest = rest[:n_cast], rest[n_cast:]
        outs, rest = rest[:n_out], rest[n_out:]
        cast_out, scratch = rest[:n_cast], rest[n_cast:]
        for src, dst in zip(cast_in, cast_out):
            dst[...] = src[...].astype(BF16)
        kernel_fn(*ins, *outs, *scratch)
    return wrapped


def _cast_specs(casts, steps, flat_step):
    in_specs, out_specs, out_shapes = [], [], []
    for w, layer in casts:
        _, rows, cols = w.shape
        n_slices = max(n for n in range(1, steps + 1)
                       if steps % n == 0 and rows % n == 0 and (rows // n) % BF16_SUBLANES == 0)
        per = steps // n_slices
        blk = rows // n_slices
        in_specs.append(pl.BlockSpec((None, blk, cols), lambda *idx, l=layer, p=per: (l, flat_step(*idx) // p, 0)))
        out_specs.append(pl.BlockSpec((blk, cols), lambda *idx, p=per: (flat_step(*idx) // p, 0)))
        out_shapes.append(jax.ShapeDtypeStruct((rows, cols), BF16))
    return in_specs, out_specs, out_shapes


def _smem_spec():
    return pl.BlockSpec(memory_space=pltpu.SMEM)


def _rms(x, g):
    return x * lax.rsqrt(jnp.mean(x * x, axis=-1, keepdims=True) + RMS_EPS) * g


def _dot(a, b):
    return jnp.dot(a, b, preferred_element_type=F32)


def _dot_nt(a, b):
    return lax.dot_general(a, b, (((1,), (1,)), ((), ())), preferred_element_type=F32)


def _dot_tn(a, b):
    return lax.dot_general(a, b, (((0,), (0,)), ((), ())), preferred_element_type=F32)


def _select_by_code(code, tbl_ref, h, otherwise):
    val = otherwise
    for b in range(N_BUCKETS):
        val = jnp.where(code == b, tbl_ref[b, h], val)
    return val


def _meta_table_kernel(code_ref, tbl_ref, sink_ref, out_ref):
    n_var, rows, _ = code_ref.shape
    for v in range(n_var):
        code = code_ref[v]
        for h in range(N_HEADS):
            kh, g = divmod(h, GROUP)
            val = _select_by_code(code, tbl_ref, h, jnp.where(code == CODE_SINK, sink_ref[h], -jnp.inf))
            out_ref[v, kh, g * rows:(g + 1) * rows, :] = val


def _meta_table(codes, rel_table, sinks):
    n_var, rows, cols = codes.shape
    return pl.pallas_call(
        _meta_table_kernel,
        out_shape=jax.ShapeDtypeStruct((n_var, N_KV, GROUP * rows, cols), F32),
        in_specs=[pl.BlockSpec(memory_space=pltpu.VMEM), _smem_spec(), _smem_spec()],
        out_specs=pl.BlockSpec(memory_space=pltpu.VMEM),
        name="meta_bias_table",
    )(jnp.asarray(codes), rel_table, sinks)


def _real_table_kernel(band_code_ref, meta_code_ref, tbl_ref, sink_ref, out_ref):
    neg = -jnp.inf
    band_code = band_code_ref[0:1, :]
    t = lax.broadcasted_iota(jnp.int32, (BAND_ROWS, WINDOW), 0)
    i = lax.broadcasted_iota(jnp.int32, (BAND_ROWS, WINDOW), 1)
    in_window = (i < t) & (t <= i + WINDOW)
    row16 = lax.broadcasted_iota(jnp.int32, (N_META, Q_BLK), 0)
    for h in range(N_HEADS):
        kh, g = divmod(h, GROUP)
        lanes = slice(g * Q_BLK, (g + 1) * Q_BLK)
        by_dist = _select_by_code(band_code, tbl_ref, h, jnp.zeros((1, WINDOW), F32)) * LOG2E
        rolled = pltpu.roll(jnp.broadcast_to(by_dist, (WINDOW, WINDOW)), 0, 1, stride=1, stride_axis=0)
        rolled = jnp.concatenate([rolled, rolled[:BAND_ROWS - WINDOW]], axis=0)
        far = jnp.full((N_META, Q_BLK), tbl_ref[N_BUCKETS - 1, h], F32)
        sink_pad = jnp.where(row16 == 0, sink_ref[h] * LOG2E, neg)
        for v in range(1 + N_NEAR):
            if v == 0:
                valid, meta = in_window, far * LOG2E
            else:
                first_key_row = WINDOW - (v - 1) * Q_BLK
                valid = in_window & (t >= first_key_row)
                meta = _select_by_code(meta_code_ref[v - 1], tbl_ref, h, far) * LOG2E
            out_ref[v, kh, 0:BAND_ROWS, lanes] = jnp.where(valid, rolled, neg)[:, :Q_BLK]
            out_ref[v, kh, BAND_ROWS:SINK_ROW, lanes] = meta
            out_ref[v, kh, SINK_ROW:KEY_ROWS, lanes] = sink_pad


def _real_table(rel_table, sinks):
    vmem = pl.BlockSpec(memory_space=pltpu.VMEM)
    return pl.pallas_call(
        _real_table_kernel,
        out_shape=jax.ShapeDtypeStruct((1 + N_NEAR, N_KV, KEY_ROWS, GROUP * Q_BLK), F32),
        in_specs=[vmem, vmem, _smem_spec(), _smem_spec()],
        out_specs=vmem,
        name="real_bias_table",
    )(jnp.asarray(_band_codes()), jnp.asarray(_near_meta_codes()), rel_table, sinks)


def _attn_kernel(h_ref, g_ref, wqv_ref, bqv_ref, wk_ref, bk_ref, km_ref, vmT_ref, tbl_ref, wo_ref, bo_ref,
                 out_ref, qT_scr, k_scr, vT_scr, oT_scr, *, tm):
    first_tile = pl.program_id(1) == 0

    @pl.when(first_tile)
    def _():
        k_scr[0:WINDOW, :] = jnp.zeros((WINDOW, KV_DIM), BF16)
        vT_scr[:, 0:WINDOW] = jnp.zeros((KV_DIM, WINDOW), BF16)

    @pl.when(jnp.logical_not(first_tile))
    def _():
        k_scr[0:WINDOW, :] = k_scr[tm:tm + WINDOW, :]
        vT_scr[:, 0:WINDOW] = vT_scr[:, tm:tm + WINDOW]

    x = h_ref[...]
    a = _rms(x, g_ref[...]).astype(BF16)
    bias = jnp.concatenate([bqv_ref[...]] * (tm // LANES), axis=1)
    qvT = _dot_nt(wqv_ref[...], a) + bias
    qT_scr[...] = (qvT[:Q_DIM] * (HEAD_DIM ** -0.5 * LOG2E)).astype(BF16)
    vT_scr[:, WINDOW:] = qvT[Q_DIM:].astype(BF16)
    k_scr[WINDOW:, :] = (_dot(a, wk_ref[...]) + bk_ref[...]).astype(BF16)

    kext = k_scr[...]
    vText = vT_scr[...]
    km = km_ref[...]
    vmT = vmT_ref[...]
    zk = jnp.zeros((KEY_ROWS - SINK_ROW, HEAD_DIM), BF16)
    ones = jnp.ones((ONES_ROWS, KEY_ROWS), BF16)

    def scores(j, kh):
        r0 = j * Q_BLK
        cols = slice(kh * HEAD_DIM, (kh + 1) * HEAD_DIM)
        qsT = jnp.concatenate(
            [qT_scr[(kh * GROUP + g) * HEAD_DIM:(kh * GROUP + g + 1) * HEAD_DIM, r0:r0 + Q_BLK]
             for g in range(GROUP)], axis=1)
        keys = jnp.concatenate([kext[r0:r0 + BAND_ROWS, cols], km[:, cols], zk], axis=0)
        return _dot(keys, qsT)

    def finish(j, kh, sT, sT_next):
        r0 = j * Q_BLK
        cols = slice(kh * HEAD_DIM, (kh + 1) * HEAD_DIM)
        var = jnp.where(first_tile, 1 + j, 0) if j < N_NEAR else 0
        sT = sT + tbl_ref[var, kh]
        m = jnp.max(sT, axis=0, keepdims=True)
        pT = jnp.exp2(sT - m).astype(BF16)
        valsT = jnp.concatenate([vText[cols, r0:r0 + BAND_ROWS], vmT[cols, :]], axis=1)
        den = ones
        if sT_next is not None:
            bits = pltpu.bitcast(sT_next[KEY_ROWS - ONES_ROWS:, :LANES], jnp.uint32)
            zero = pltpu.bitcast((bits >> 16) >> 16, F32).astype(BF16)
            den = jnp.concatenate([ones[:, :LANES] + zero, ones[:, LANES:]], axis=1)
        oT = _dot(jnp.concatenate([valsT, den], axis=0), pT)
        oT = oT[:HEAD_DIM] / oT[HEAD_DIM:HEAD_DIM + 1]
        for g in range(GROUP):
            h0 = (kh * GROUP + g) * HEAD_DIM
            oT_scr[h0:h0 + HEAD_DIM, r0:r0 + Q_BLK] = oT[:, g * Q_BLK:(g + 1) * Q_BLK].astype(BF16)

    units = [(j, kh) for j in range(tm // Q_BLK) for kh in range(N_KV)]
    pending = [scores(*u) for u in units[:ATTN_LOOKAHEAD]]
    for idx, unit in enumerate(units):
        ahead = idx + ATTN_LOOKAHEAD
        if ahead < len(units):
            pending.append(scores(*units[ahead]))
        sT = pending.pop(0)
        finish(*unit, sT, pending[-1] if pending else None)
    out_ref[...] = x + _dot_tn(oT_scr[...], wo_ref[...]) + bo_ref[...]


def _attn_call(h, g, wqvT, bqvT, wk, bk, km, vmT, tbl, wo, bo, batch, tm, casts=()):
    n = h.shape[0]
    nt = n // batch // tm
    row = pl.BlockSpec((tm, D_MODEL), lambda b, i: (b * nt + i, 0))
    cast_in, cast_out, cast_shapes = _cast_specs(casts, batch * nt, lambda b, i: b * nt + i)
    in_specs = [row, _const_spec((1, D_MODEL)), _const_spec(wqvT.shape), _const_spec(bqvT.shape),
                _const_spec(wk.shape), _const_spec((1, KV_DIM)), _const_spec(km.shape), _const_spec(vmT.shape),
                _const_spec(tbl.shape), _const_spec(wo.shape), _const_spec((1, D_MODEL))]
    outs = pl.pallas_call(
        _hosting(functools.partial(_attn_kernel, tm=tm), len(in_specs), 1, len(casts)),
        grid=(batch, nt),
        out_shape=[jax.ShapeDtypeStruct((n, D_MODEL), F32)] + cast_shapes,
        in_specs=in_specs + cast_in,
        out_specs=[row] + cast_out,
        scratch_shapes=[pltpu.VMEM((Q_DIM, tm), BF16), pltpu.VMEM((WINDOW + tm, KV_DIM), BF16),
                        pltpu.VMEM((KV_DIM, WINDOW + tm), BF16), pltpu.VMEM((Q_DIM, tm), BF16)],
        compiler_params=pltpu.CompilerParams(dimension_semantics=("arbitrary", "arbitrary"),
                                             vmem_limit_bytes=VMEM_LIMIT),
        name="attn",
    )(h, g, wqvT, bqvT, wk, bk, km, vmT, tbl, wo, bo, *[v for v, _ in casts])
    return outs[0], outs[1:]


def _attn_meta_kernel(h_ref, g_ref, wqv_ref, bqv_ref, wk_ref, b_ref, tbl_ref, wo_ref, bo_ref,
                      out_ref, km_ref, vmT_ref):
    x = h_ref[...]
    a = _rms(x, g_ref[...]).astype(BF16)
    a2 = jnp.concatenate([a, jnp.zeros_like(a)], axis=0)
    lane = lax.broadcasted_iota(jnp.int32, (KV_DIM, 2 * N_META), 1)
    vT = _dot_nt(wqv_ref[Q_DIM:, :], a2) + jnp.where(lane < N_META, bqv_ref[Q_DIM:, :2 * N_META], 0.0)
    vmT_ref[...] = vT.astype(BF16)
    b = b_ref[...]
    q = ((_dot_nt(a, wqv_ref[:Q_DIM, :]) + b[:, :Q_DIM]) * (HEAD_DIM ** -0.5)).astype(BF16)
    k = (_dot(a, wk_ref[...]) + b[:, Q_DIM:Q_DIM + KV_DIM]).astype(BF16)
    v = (_dot_nt(a, wqv_ref[Q_DIM:, :]) + b[:, Q_DIM + KV_DIM:]).astype(BF16)
    km_ref[...] = k
    pad = META_KEY_COLS - N_META
    vcat = jnp.concatenate([v, jnp.zeros((pad, KV_DIM), BF16)], axis=0)
    pieces = []
    for kh in range(N_KV):
        cols = slice(kh * HEAD_DIM, (kh + 1) * HEAD_DIM)
        qs = jnp.concatenate([q[:, (kh * GROUP + g) * HEAD_DIM:(kh * GROUP + g + 1) * HEAD_DIM]
                              for g in range(GROUP)], axis=0)
        kcat = jnp.concatenate([k[:, cols], jnp.zeros((pad, HEAD_DIM), BF16)], axis=0)
        s = _dot_nt(qs, kcat) + tbl_ref[0, kh]
        p = jnp.exp(s - jnp.max(s, axis=-1, keepdims=True))
        l = jnp.sum(p, axis=-1, keepdims=True)
        o = _dot(p.astype(BF16), vcat)[:, cols] / l
        pieces += [o[g * N_META:(g + 1) * N_META] for g in range(GROUP)]
    o_all = jnp.concatenate(pieces, axis=-1).astype(BF16)
    out_ref[...] = x + _dot(o_all, wo_ref[...]) + bo_ref[...]


def _attn_meta_call(h, g, wqvT, bqvT, wk, bk, tbl, wo, bo):
    vmem = pl.BlockSpec(memory_space=pltpu.VMEM)
    return pl.pallas_call(
        _attn_meta_kernel,
        out_shape=(jax.ShapeDtypeStruct((N_META, D_MODEL), F32),
                   jax.ShapeDtypeStruct((N_META, KV_DIM), BF16),
                   jax.ShapeDtypeStruct((KV_DIM, 2 * N_META), BF16)),
        in_specs=[vmem] * 9,
        out_specs=(vmem, vmem, vmem),
        compiler_params=pltpu.CompilerParams(vmem_limit_bytes=VMEM_LIMIT),
        name="attn_meta",
    )(h, g, wqvT, bqvT, wk, bk, tbl, wo, bo)


def _ffn_math(x, g_ref, wg_ref, wu_ref, wd_ref):
    rows = x.shape[0]
    lead = FFN_LEAD_ROWS if rows > 2 * FFN_LEAD_ROWS else rows
    chunks = [x[:lead]] + ([x[lead:]] if lead < rows else [])
    hmids = []
    for xc in chunks:
        a = _rms(xc, g_ref[...]).astype(BF16)
        gate = _dot(a, wg_ref[...])
        up = _dot(a, wu_ref[...])
        hmids.append((gate * (1.0 / (1.0 + jnp.exp(-gate))) * up).astype(BF16))
    outs = [xc + _dot(hm, wd_ref[...]) for xc, hm in zip(chunks, hmids)]
    return outs[0] if len(outs) == 1 else jnp.concatenate(outs, axis=0)


def _ffn_kernel(h_ref, g_ref, wg_ref, wu_ref, wd_ref, gf_ref, out_ref, *, final_norm):
    y = _ffn_math(h_ref[...], g_ref, wg_ref, wu_ref, wd_ref)
    out_ref[...] = _rms(y, gf_ref[...]) if final_norm else y


def _ffn_specs(wg, wu, wd):
    return [_const_spec((1, D_MODEL)), _const_spec(wg.shape), _const_spec(wu.shape), _const_spec(wd.shape)]


def _ffn_call(h, g, wg, wu, wd, gf, tm, final_norm, casts=()):
    n = h.shape[0]
    row = pl.BlockSpec((tm, D_MODEL), lambda i: (i, 0))
    cast_in, cast_out, cast_shapes = _cast_specs(casts, n // tm, lambda i: i)
    in_specs = [row] + _ffn_specs(wg, wu, wd) + [_const_spec((1, D_MODEL))]
    outs = pl.pallas_call(
        _hosting(functools.partial(_ffn_kernel, final_norm=final_norm), len(in_specs), 1, len(casts)),
        grid=(n // tm,),
        out_shape=[jax.ShapeDtypeStruct((n, D_MODEL), F32)] + cast_shapes,
        in_specs=in_specs + cast_in,
        out_specs=[row] + cast_out,
        compiler_params=pltpu.CompilerParams(dimension_semantics=("parallel",), vmem_limit_bytes=VMEM_LIMIT),
        name="ffn",
    )(h, g, wg, wu, wd, gf, *[v for v, _ in casts])
    return outs[0], outs[1:]


def _conv_ffn_kernel(h_ref, g_ref, win_ref, cw_ref, wout_ref, zinit_ref, gf_ref, wg_ref, wu_ref, wd_ref,
                     out_ref, ztail_ref, zprev_scr, *, tm):
    @pl.when(pl.program_id(1) == 0)
    def _():
        zprev_scr[...] = zinit_ref[...]

    x = h_ref[...]
    a = _rms(x, g_ref[...]).astype(BF16)
    bcu = _dot(a, win_ref[...])
    gate_b = bcu[:, :D_MODEL]
    z = bcu[:, D_MODEL:2 * D_MODEL] * bcu[:, 2 * D_MODEL:]
    zp = zprev_scr[...]
    row = lax.broadcasted_iota(jnp.int32, (tm, D_MODEL), 0)
    last, last2 = zp[CONV_TAIL - 1:CONV_TAIL], zp[CONV_TAIL - 2:CONV_TAIL - 1]
    z1 = jnp.where(row == 0, last, pltpu.roll(z, 1, 0))
    z2 = jnp.where(row == 0, last2, jnp.where(row == 1, last, pltpu.roll(z, 2, 0)))
    cw = cw_ref[...]
    conv = cw[0:1] * z2 + cw[1:2] * z1 + cw[2:3] * z
    y = x + _dot((gate_b * conv).astype(BF16), wout_ref[...])
    out_ref[...] = _ffn_math(y, gf_ref, wg_ref, wu_ref, wd_ref)
    tail = z[tm - CONV_TAIL:]
    zprev_scr[...] = tail
    ztail_ref[...] = tail


def _conv_ffn_call(h, g, win, cw, wout, zinit, gf, wg, wu, wd, batch, tm, casts=()):
    n = h.shape[0]
    nt = n // batch // tm
    row = pl.BlockSpec((tm, D_MODEL), lambda b, i: (b * nt + i, 0))
    tail_shape = (CONV_TAIL, D_MODEL)
    cast_in, cast_out, cast_shapes = _cast_specs(casts, batch * nt, lambda b, i: b * nt + i)
    in_specs = [row, _const_spec((1, D_MODEL)), _const_spec(win.shape), _const_spec(cw.shape),
                _const_spec(wout.shape), _const_spec(tail_shape)] + _ffn_specs(wg, wu, wd)
    outs = pl.pallas_call(
        _hosting(functools.partial(_conv_ffn_kernel, tm=tm), len(in_specs), 2, len(casts)),
        grid=(batch, nt),
        out_shape=[jax.ShapeDtypeStruct((n, D_MODEL), F32), jax.ShapeDtypeStruct(tail_shape, F32)] + cast_shapes,
        in_specs=in_specs + cast_in,
        out_specs=[row, pl.BlockSpec(tail_shape, lambda b, i: (0, 0))] + cast_out,
        scratch_shapes=[pltpu.VMEM(tail_shape, F32)],
        compiler_params=pltpu.CompilerParams(dimension_semantics=("arbitrary", "arbitrary"),
                                             vmem_limit_bytes=VMEM_LIMIT),
        name="conv_ffn",
    )(h, g, win, cw, wout, zinit, gf, wg, wu, wd, *[v for v, _ in casts])
    return outs[0], outs[1], outs[2:]


def _pool_ffn_kernel(h_ref, g_ref, wp_ref, sc_ref, hinit_ref, gf_ref, wg_ref, wu_ref, wd_ref,
                     out_ref, atail_ref, hist_scr, *, tm, from_start):
    @pl.when(pl.program_id(1) == 0)
    def _():
        hist_scr[...] = hinit_ref[...]

    x = h_ref[...]
    a = _rms(x, g_ref[...])
    ext = jnp.concatenate([hist_scr[...], a], axis=0)
    outs = []
    for gi, win in enumerate(POOL_WINDOWS):
        cols = slice(gi * POOL_GROUP_DIM, (gi + 1) * POOL_GROUP_DIM)
        s = ext[:, cols]
        k = 1
        while k < win:
            s = s + pltpu.roll(s, k, 0)
            k *= 2
        s = s[POOL_HIST:]
        if from_start:
            pos = lax.broadcasted_iota(jnp.int32, (tm, POOL_GROUP_DIM), 0)
            mean = s / jnp.minimum(win, pos + 1).astype(F32)
        else:
            mean = s * (1.0 / win)
        mix = (mean - a[:, cols]).astype(BF16)
        outs.append(_dot(mix, wp_ref[gi]))
    y = x + jnp.concatenate(outs, axis=-1) * sc_ref[...]
    out_ref[...] = _ffn_math(y, gf_ref, wg_ref, wu_ref, wd_ref)
    tail = a[tm - POOL_HIST:]
    hist_scr[...] = tail
    atail_ref[...] = tail


def _pool_ffn_call(h, g, wp, sc, hinit, gf, wg, wu, wd, batch, tm, from_start):
    n = h.shape[0]
    nt = n // batch // tm
    row = pl.BlockSpec((tm, D_MODEL), lambda b, i: (b * nt + i, 0))
    hist_shape = (POOL_HIST, D_MODEL)
    return pl.pallas_call(
        functools.partial(_pool_ffn_kernel, tm=tm, from_start=from_start),
        grid=(batch, nt),
        out_shape=(jax.ShapeDtypeStruct((n, D_MODEL), F32), jax.ShapeDtypeStruct(hist_shape, F32)),
        in_specs=[row, _const_spec((1, D_MODEL)), _const_spec(wp.shape), _const_spec((1, D_MODEL)),
                  _const_spec(hist_shape)] + _ffn_specs(wg, wu, wd),
        out_specs=(row, pl.BlockSpec(hist_shape, lambda b, i: (0, 0))),
        scratch_shapes=[pltpu.VMEM(hist_shape, F32)],
        compiler_params=pltpu.CompilerParams(dimension_semantics=("arbitrary", "arbitrary"),
                                             vmem_limit_bytes=VMEM_LIMIT),
        name="pool_ffn",
    )(h, g, wp, sc, hinit, gf, wg, wu, wd)


ATTN_TM = 512
ATTN_LOOKAHEAD = 2
ROW_TM = 1024
CONV_TM = 1024


def kernel(x, meta_tokens, rel_bias_table, norm_mix, norm_ffn, norm_final, attn_w_qkv, attn_b_qkv, attn_w_o,
           attn_b_o, attn_sinks, conv_w_in, conv_w, conv_w_out, pool_w, pool_scale, ffn_w_gate, ffn_w_up,
           ffn_w_down):
    batch, seq, _ = x.shape
    depth = norm_mix.shape[0]
    n_mixers = 3
    hr = x.reshape(batch * seq, D_MODEL)
    hm = meta_tokens.astype(F32)
    row = lambda v: v.reshape(1, -1)
    meta_codes = _meta_codes()

    ffn_bf16, conv_bf16 = {}, {}

    def ffn_casts(layers):
        return [(w, l) for l in layers for w in (ffn_w_gate, ffn_w_up, ffn_w_down)]

    for i in range(depth):
        kind, j = i % n_mixers, i // n_mixers
        last = i == depth - 1
        gm, gf = row(norm_mix[i]), row(norm_ffn[i])
        if kind != 0:
            wg, wu, wd = ffn_bf16[i]
        if kind == 0:
            w, b = attn_w_qkv[j], attn_b_qkv[j]
            wqvT = jnp.concatenate([w[:, :Q_DIM], w[:, Q_DIM + KV_DIM:]], axis=1).T.astype(BF16)
            bqvT = jnp.broadcast_to(jnp.concatenate([b[:Q_DIM], b[Q_DIM + KV_DIM:]])[:, None], (Q_DIM + KV_DIM, LANES))
            wk, bk = w[:, Q_DIM:Q_DIM + KV_DIM].astype(BF16), row(b[Q_DIM:Q_DIM + KV_DIM])
            wo, bo = attn_w_o[j].astype(BF16), row(attn_b_o[j])
            tbl_meta = _meta_table(meta_codes, rel_bias_table, attn_sinks[j])
            tbl_real = _real_table(rel_bias_table, attn_sinks[j])
            hm, km, vmT = _attn_meta_call(hm, gm, wqvT, bqvT, wk, row(b), tbl_meta, wo, bo)
            hosted = [l for l in (i, i + 2) if l < depth and (l == i or l % n_mixers == 2)]
            hr, cast = _attn_call(hr, gm, wqvT, bqvT, wk, bk, km, vmT, tbl_real, wo, bo, batch, ATTN_TM,
                                  ffn_casts(hosted))
            for n, l in enumerate(hosted):
                ffn_bf16[l] = cast[3 * n:3 * n + 3]
            wg, wu, wd = ffn_bf16[i]
            if not last:
                hm, _ = _ffn_call(hm, gf, wg, wu, wd, row(norm_final), N_META, False)
            conv_next = i + 1 < depth and (i + 1) % n_mixers == 1
            casts = ([(conv_w_in, (i + 1) // n_mixers), (conv_w_out, (i + 1) // n_mixers)] + ffn_casts([i + 1])
                     if conv_next else [])
            hr, cast = _ffn_call(hr, gf, wg, wu, wd, row(norm_final), ROW_TM, last, casts)
            if conv_next:
                conv_bf16[i + 1], ffn_bf16[i + 1] = cast[:2], cast[2:]
        elif kind == 1:
            win, wout = conv_bf16[i]
            hm, ztail, _ = _conv_ffn_call(hm, gm, win, conv_w[j], wout, jnp.zeros((CONV_TAIL, D_MODEL), F32),
                                          gf, wg, wu, wd, 1, N_META)
            hr, _, _ = _conv_ffn_call(hr, gm, win, conv_w[j], wout, ztail, gf, wg, wu, wd, batch, CONV_TM)
        else:
            wp, sc = pool_w[j].astype(BF16), row(pool_scale[j])
            hm, atail = _pool_ffn_call(hm, gm, wp, sc, jnp.zeros((POOL_HIST, D_MODEL), F32),
                                       gf, wg, wu, wd, 1, N_META, True)
            hr, _ = _pool_ffn_call(hr, gm, wp, sc, atail, gf, wg, wu, wd, batch, ROW_TM, False)
    return hr.reshape(batch, seq, D_MODEL)
```

```python
import functools
import math

import numpy as np
import jax
import jax.numpy as jnp
from jax import lax
from jax.experimental import pallas as pl
from jax.experimental.pallas import tpu as pltpu

F32 = jnp.float32
BF16 = jnp.bfloat16

D_MODEL = 1024
N_META = 16
RMS_EPS = 1e-6
N_HEADS = 16
N_KV = 4
HEAD_DIM = 64
GROUP = N_HEADS // N_KV
Q_DIM = N_HEADS * HEAD_DIM
KV_DIM = N_KV * HEAD_DIM
WINDOW = 128
N_BUCKETS = 32
MAX_DISTANCE = 128
POOL_WINDOWS = (2, 4, 8, 16)
POOL_GROUP_DIM = D_MODEL // len(POOL_WINDOWS)
POOL_HIST = 16
CONV_TAIL = 8
LANES = 128
FFN_LEAD_ROWS = 256
BF16_SUBLANES = 16
LOG2E = math.log2(math.e)

Q_BLK = 128
BAND_ROWS = WINDOW + Q_BLK
SINK_ROW = BAND_ROWS + N_META
KEY_ROWS = SINK_ROW + N_META
N_NEAR = WINDOW // Q_BLK
ONES_ROWS = 16

META_KEY_COLS = 128
CODE_SINK = N_BUCKETS
CODE_MASKED = N_BUCKETS + 1

VMEM_LIMIT = 56 * 1024 * 1024


def _rel_bucket_np(dist):
    max_exact = N_BUCKETS // 2
    d = np.maximum(dist, 0)
    df = np.maximum(d, 1).astype(np.float64)
    large = max_exact + (np.log(df / max_exact) / math.log(MAX_DISTANCE / max_exact)
                         * (N_BUCKETS - max_exact)).astype(np.int64)
    large = np.minimum(large, N_BUCKETS - 1)
    return np.where(d < max_exact, d, large).astype(np.int32)


def _meta_codes():
    i = np.arange(N_META)[:, None]
    m = np.arange(N_META)[None, :]
    code = np.full((N_META, META_KEY_COLS), CODE_MASKED, np.int32)
    code[:, :N_META] = np.where(i >= m, _rel_bucket_np(i - m), CODE_MASKED)
    code[:, N_META] = CODE_SINK
    return code[None]


def _band_codes():
    return np.stack([_rel_bucket_np(np.arange(WINDOW)), np.zeros(WINDOW, np.int32)])


def _near_meta_codes():
    m = np.arange(N_META)[:, None]
    i = np.arange(Q_BLK)[None, :]
    return np.stack([_rel_bucket_np(N_META + n * Q_BLK + i - m) for n in range(N_NEAR)])


def _const_spec(shape):
    zeros = (0,) * len(shape)
    return pl.BlockSpec(shape, lambda *_: zeros, pipeline_mode=pl.Buffered(1))


def _hosting(kernel_fn, n_in, n_out, n_cast):
    def wrapped(*refs):
        ins, rest = refs[:n_in], refs[n_in:]
        cast_in, rest = rest[:n_cast], rest[n_cast:]
        outs, rest = rest[:n_out], rest[n_out:]
        cast_out, scratch = rest[:n_cast], rest[n_cast:]
        for src, dst in zip(cast_in, cast_out):
            dst[...] = src[...].astype(BF16)
        kernel_fn(*ins, *outs, *scratch)
    return wrapped


def _cast_specs(casts, steps, flat_step):
    in_specs, out_specs, out_shapes = [], [], []
    for w, layer in casts:
        _, rows, cols = w.shape
        n_slices = max(n for n in range(1, steps + 1)
                       if steps % n == 0 and rows % n == 0 and (rows // n) % BF16_SUBLANES == 0)
        per = steps // n_slices
        blk = rows // n_slices
        in_specs.append(pl.BlockSpec((None, blk, cols), lambda *idx, l=layer, p=per: (l, flat_step(*idx) // p, 0)))
        out_specs.append(pl.BlockSpec((blk, cols), lambda *idx, p=per: (flat_step(*idx) // p, 0)))
        out_shapes.append(jax.ShapeDtypeStruct((rows, cols), BF16))
    return in_specs, out_specs, out_shapes


def _smem_spec():
    return pl.BlockSpec(memory_space=pltpu.SMEM)


def _rms(x, g):
    return x * lax.rsqrt(jnp.mean(x * x, axis=-1, keepdims=True) + RMS_EPS) * g


def _dot(a, b):
    return jnp.dot(a, b, preferred_element_type=F32)


def _dot_nt(a, b):
    return lax.dot_general(a, b, (((1,), (1,)), ((), ())), preferred_element_type=F32)


def _dot_tn(a, b):
    return lax.dot_general(a, b, (((0,), (0,)), ((), ())), preferred_element_type=F32)


def _select_by_code(code, tbl_ref, h, otherwise):
    val = otherwise
    for b in range(N_BUCKETS):
        val = jnp.where(code == b, tbl_ref[b, h], val)
    return val


def _meta_table_kernel(code_ref, tbl_ref, sink_ref, out_ref):
    n_var, rows, _ = code_ref.shape
    for v in range(n_var):
        code = code_ref[v]
        for h in range(N_HEADS):
            kh, g = divmod(h, GROUP)
            val = _select_by_code(code, tbl_ref, h, jnp.where(code == CODE_SINK, sink_ref[h], -jnp.inf))
            out_ref[v, kh, g * rows:(g + 1) * rows, :] = val


def _meta_table(codes, rel_table, sinks):
    n_var, rows, cols = codes.shape
    return pl.pallas_call(
        _meta_table_kernel,
        out_shape=jax.ShapeDtypeStruct((n_var, N_KV, GROUP * rows, cols), F32),
        in_specs=[pl.BlockSpec(memory_space=pltpu.VMEM), _smem_spec(), _smem_spec()],
        out_specs=pl.BlockSpec(memory_space=pltpu.VMEM),
        name="meta_bias_table",
    )(jnp.asarray(codes), rel_table, sinks)


def _real_table_kernel(band_code_ref, meta_code_ref, tbl_ref, sink_ref, out_ref):
    neg = -jnp.inf
    band_code = band_code_ref[0:1, :]
    t = lax.broadcasted_iota(jnp.int32, (BAND_ROWS, WINDOW), 0)
    i = lax.broadcasted_iota(jnp.int32, (BAND_ROWS, WINDOW), 1)
    in_window = (i < t) & (t <= i + WINDOW)
    row16 = lax.broadcasted_iota(jnp.int32, (N_META, Q_BLK), 0)
    for h in range(N_HEADS):
        kh, g = divmod(h, GROUP)
        lanes = slice(g * Q_BLK, (g + 1) * Q_BLK)
        by_dist = _select_by_code(band_code, tbl_ref, h, jnp.zeros((1, WINDOW), F32)) * LOG2E
        rolled = pltpu.roll(jnp.broadcast_to(by_dist, (WINDOW, WINDOW)), 0, 1, stride=1, stride_axis=0)
        rolled = jnp.concatenate([rolled, rolled[:BAND_ROWS - WINDOW]], axis=0)
        far = jnp.full((N_META, Q_BLK), tbl_ref[N_BUCKETS - 1, h], F32)
        sink_pad = jnp.where(row16 == 0, sink_ref[h] * LOG2E, neg)
        for v in range(1 + N_NEAR):
            if v == 0:
                valid, meta = in_window, far * LOG2E
            else:
                first_key_row = WINDOW - (v - 1) * Q_BLK
                valid = in_window & (t >= first_key_row)
                meta = _select_by_code(meta_code_ref[v - 1], tbl_ref, h, far) * LOG2E
            out_ref[v, kh, 0:BAND_ROWS, lanes] = jnp.where(valid, rolled, neg)[:, :Q_BLK]
            out_ref[v, kh, BAND_ROWS:SINK_ROW, lanes] = meta
            out_ref[v, kh, SINK_ROW:KEY_ROWS, lanes] = sink_pad


def _real_table(rel_table, sinks):
    vmem = pl.BlockSpec(memory_space=pltpu.VMEM)
    return pl.pallas_call(
        _real_table_kernel,
        out_shape=jax.ShapeDtypeStruct((1 + N_NEAR, N_KV, KEY_ROWS, GROUP * Q_BLK), F32),
        in_specs=[vmem, vmem, _smem_spec(), _smem_spec()],
        out_specs=vmem,
        name="real_bias_table",
    )(jnp.asarray(_band_codes()), jnp.asarray(_near_meta_codes()), rel_table, sinks)


def _attn_kernel(h_ref, g_ref, wqv_ref, bqv_ref, wk_ref, bk_ref, km_ref, vmT_ref, tbl_ref, wo_ref, bo_ref,
                 out_ref, qT_scr, k_scr, vT_scr, oT_scr, *, tm):
    first_tile = pl.program_id(1) == 0

    @pl.when(first_tile)
    def _():
        k_scr[0:WINDOW, :] = jnp.zeros((WINDOW, KV_DIM), BF16)
        vT_scr[:, 0:WINDOW] = jnp.zeros((KV_DIM, WINDOW), BF16)

    @pl.when(jnp.logical_not(first_tile))
    def _():
        k_scr[0:WINDOW, :] = k_scr[tm:tm + WINDOW, :]
        vT_scr[:, 0:WINDOW] = vT_scr[:, tm:tm + WINDOW]

    x = h_ref[...]
    a = _rms(x, g_ref[...]).astype(BF16)
    bias = jnp.concatenate([bqv_ref[...]] * (tm // LANES), axis=1)
    qvT = _dot_nt(wqv_ref[...], a) + bias
    qT_scr[...] = (qvT[:Q_DIM] * (HEAD_DIM ** -0.5 * LOG2E)).astype(BF16)
    vT_scr[:, WINDOW:] = qvT[Q_DIM:].astype(BF16)
    k_scr[WINDOW:, :] = (_dot(a, wk_ref[...]) + bk_ref[...]).astype(BF16)

    kext = k_scr[...]
    vText = vT_scr[...]
    km = km_ref[...]
    vmT = vmT_ref[...]
    zk = jnp.zeros((KEY_ROWS - SINK_ROW, HEAD_DIM), BF16)
    ones = jnp.ones((ONES_ROWS, KEY_ROWS), BF16)

    def scores(j, kh):
        r0 = j * Q_BLK
        cols = slice(kh * HEAD_DIM, (kh + 1) * HEAD_DIM)
        qsT = jnp.concatenate(
            [qT_scr[(kh * GROUP + g) * HEAD_DIM:(kh * GROUP + g + 1) * HEAD_DIM, r0:r0 + Q_BLK]
             for g in range(GROUP)], axis=1)
        keys = jnp.concatenate([kext[r0:r0 + BAND_ROWS, cols], km[:, cols], zk], axis=0)
        return _dot(keys, qsT)

    def finish(j, kh, sT, sT_next):
        r0 = j * Q_BLK
        cols = slice(kh * HEAD_DIM, (kh + 1) * HEAD_DIM)
        var = jnp.where(first_tile, 1 + j, 0) if j < N_NEAR else 0
        sT = sT + tbl_ref[var, kh]
        m = jnp.max(sT, axis=0, keepdims=True)
        pT = jnp.exp2(sT - m).astype(BF16)
        valsT = jnp.concatenate([vText[cols, r0:r0 + BAND_ROWS], vmT[cols, :]], axis=1)
        den = ones
        if sT_next is not None:
            bits = pltpu.bitcast(sT_next[KEY_ROWS - ONES_ROWS:, :LANES], jnp.uint32)
            zero = pltpu.bitcast((bits >> 16) >> 16, F32).astype(BF16)
            den = jnp.concatenate([ones[:, :LANES] + zero, ones[:, LANES:]], axis=1)
        oT = _dot(jnp.concatenate([valsT, den], axis=0), pT)
        oT = oT[:HEAD_DIM] / oT[HEAD_DIM:HEAD_DIM + 1]
        for g in range(GROUP):
            h0 = (kh * GROUP + g) * HEAD_DIM
            oT_scr[h0:h0 + HEAD_DIM, r0:r0 + Q_BLK] = oT[:, g * Q_BLK:(g + 1) * Q_BLK].astype(BF16)

    units = [(j, kh) for j in range(tm // Q_BLK) for kh in range(N_KV)]
    pending = [scores(*u) for u in units[:ATTN_LOOKAHEAD]]
    for idx, unit in enumerate(units):
        ahead = idx + ATTN_LOOKAHEAD
        if ahead < len(units):
            pending.append(scores(*units[ahead]))
        sT = pending.pop(0)
        finish(*unit, sT, pending[-1] if pending else None)
    out_ref[...] = x + _dot_tn(oT_scr[...], wo_ref[...]) + bo_ref[...]


def _attn_call(h, g, wqvT, bqvT, wk, bk, km, vmT, tbl, wo, bo, batch, tm, casts=()):
    n = h.shape[0]
    nt = n // batch // tm
    row = pl.BlockSpec((tm, D_MODEL), lambda b, i: (b * nt + i, 0))
    cast_in, cast_out, cast_shapes = _cast_specs(casts, batch * nt, lambda b, i: b * nt + i)
    in_specs = [row, _const_spec((1, D_MODEL)), _const_spec(wqvT.shape), _const_spec(bqvT.shape),
                _const_spec(wk.shape), _const_spec((1, KV_DIM)), _const_spec(km.shape), _const_spec(vmT.shape),
                _const_spec(tbl.shape), _const_spec(wo.shape), _const_spec((1, D_MODEL))]
    outs = pl.pallas_call(
        _hosting(functools.partial(_attn_kernel, tm=tm), len(in_specs), 1, len(casts)),
        grid=(batch, nt),
        out_shape=[jax.ShapeDtypeStruct((n, D_MODEL), F32)] + cast_shapes,
        in_specs=in_specs + cast_in,
        out_specs=[row] + cast_out,
        scratch_shapes=[pltpu.VMEM((Q_DIM, tm), BF16), pltpu.VMEM((WINDOW + tm, KV_DIM), BF16),
                        pltpu.VMEM((KV_DIM, WINDOW + tm), BF16), pltpu.VMEM((Q_DIM, tm), BF16)],
        compiler_params=pltpu.CompilerParams(dimension_semantics=("arbitrary", "arbitrary"),
                                             vmem_limit_bytes=VMEM_LIMIT),
        name="attn",
    )(h, g, wqvT, bqvT, wk, bk, km, vmT, tbl, wo, bo, *[v for v, _ in casts])
    return outs[0], outs[1:]


def _attn_meta_kernel(h_ref, g_ref, w_ref, b_ref, tbl_ref, wo_ref, bo_ref,
                      out_ref, km_ref, vmT_ref, wqvT_ref, bqvT_ref, wk_ref, wo16_ref):
    w = w_ref[...]
    b = b_ref[...]
    wq, wk, wv = w[:, :Q_DIM], w[:, Q_DIM:Q_DIM + KV_DIM], w[:, Q_DIM + KV_DIM:]
    wqvT = jnp.concatenate([wq.T, wv.T], axis=0).astype(BF16)
    b_qv = jnp.concatenate([b[:, :Q_DIM], b[:, Q_DIM + KV_DIM:]], axis=1)
    bqvT = jnp.broadcast_to(b_qv, (LANES, Q_DIM + KV_DIM)).T
    wk16 = wk.astype(BF16)
    wo16 = wo_ref[...].astype(BF16)
    wqvT_ref[...] = wqvT
    bqvT_ref[...] = bqvT
    wk_ref[...] = wk16
    wo16_ref[...] = wo16

    x = h_ref[...]
    a = _rms(x, g_ref[...]).astype(BF16)
    a2 = jnp.concatenate([a, jnp.zeros_like(a)], axis=0)
    lane = lax.broadcasted_iota(jnp.int32, (KV_DIM, 2 * N_META), 1)
    vT = _dot_nt(wqvT[Q_DIM:], a2) + jnp.where(lane < N_META, bqvT[Q_DIM:, :2 * N_META], 0.0)
    vmT_ref[...] = vT.astype(BF16)
    q = ((_dot_nt(a, wqvT[:Q_DIM]) + b[:, :Q_DIM]) * (HEAD_DIM ** -0.5)).astype(BF16)
    k = (_dot(a, wk16) + b[:, Q_DIM:Q_DIM + KV_DIM]).astype(BF16)
    v = (_dot_nt(a, wqvT[Q_DIM:]) + b[:, Q_DIM + KV_DIM:]).astype(BF16)
    km_ref[...] = k
    pad = META_KEY_COLS - N_META
    vcat = jnp.concatenate([v, jnp.zeros((pad, KV_DIM), BF16)], axis=0)
    pieces = []
    for kh in range(N_KV):
        cols = slice(kh * HEAD_DIM, (kh + 1) * HEAD_DIM)
        qs = jnp.concatenate([q[:, (kh * GROUP + g) * HEAD_DIM:(kh * GROUP + g + 1) * HEAD_DIM]
                              for g in range(GROUP)], axis=0)
        kcat = jnp.concatenate([k[:, cols], jnp.zeros((pad, HEAD_DIM), BF16)], axis=0)
        s = _dot_nt(qs, kcat) + tbl_ref[0, kh]
        p = jnp.exp(s - jnp.max(s, axis=-1, keepdims=True))
        l = jnp.sum(p, axis=-1, keepdims=True)
        o = _dot(p.astype(BF16), vcat)[:, cols] / l
        pieces += [o[g * N_META:(g + 1) * N_META] for g in range(GROUP)]
    o_all = jnp.concatenate(pieces, axis=-1).astype(BF16)
    out_ref[...] = x + _dot(o_all, wo16) + bo_ref[...]


def _attn_meta_call(h, g, w_stack, layer, b, tbl, wo_stack, bo):
    whole = lambda shape: pl.BlockSpec(shape, lambda i: (0,) * len(shape))
    of_layer = lambda stack: pl.BlockSpec((None,) + stack.shape[1:], lambda i: (layer, 0, 0))
    out_shapes = ((N_META, D_MODEL, F32), (N_META, KV_DIM, BF16), (KV_DIM, 2 * N_META, BF16),
                  (Q_DIM + KV_DIM, D_MODEL, BF16), (Q_DIM + KV_DIM, LANES, F32), (D_MODEL, KV_DIM, BF16),
                  (Q_DIM, D_MODEL, BF16))
    return pl.pallas_call(
        _attn_meta_kernel,
        grid=(1,),
        out_shape=tuple(jax.ShapeDtypeStruct((r, c), dt) for r, c, dt in out_shapes),
        in_specs=[whole(h.shape), whole(g.shape), of_layer(w_stack), whole(b.shape), whole(tbl.shape),
                  of_layer(wo_stack), whole(bo.shape)],
        out_specs=tuple(whole((r, c)) for r, c, _ in out_shapes),
        compiler_params=pltpu.CompilerParams(dimension_semantics=("arbitrary",), vmem_limit_bytes=VMEM_LIMIT),
        name="attn_meta",
    )(h, g, w_stack, b, tbl, wo_stack, bo)


def _ffn_math(x, g_ref, wg_ref, wu_ref, wd_ref):
    rows = x.shape[0]
    lead = FFN_LEAD_ROWS if rows > 2 * FFN_LEAD_ROWS else rows
    chunks = [x[:lead]] + ([x[lead:]] if lead < rows else [])
    hmids = []
    for xc in chunks:
        a = _rms(xc, g_ref[...]).astype(BF16)
        gate = _dot(a, wg_ref[...])
        up = _dot(a, wu_ref[...])
        hmids.append((gate * (1.0 / (1.0 + jnp.exp(-gate))) * up).astype(BF16))
    outs = [xc + _dot(hm, wd_ref[...]) for xc, hm in zip(chunks, hmids)]
    return outs[0] if len(outs) == 1 else jnp.concatenate(outs, axis=0)


def _ffn_kernel(h_ref, g_ref, wg_ref, wu_ref, wd_ref, gf_ref, out_ref, *, final_norm):
    y = _ffn_math(h_ref[...], g_ref, wg_ref, wu_ref, wd_ref)
    out_ref[...] = _rms(y, gf_ref[...]) if final_norm else y


def _ffn_specs(wg, wu, wd):
    return [_const_spec((1, D_MODEL)), _const_spec(wg.shape), _const_spec(wu.shape), _const_spec(wd.shape)]


def _ffn_call(h, g, wg, wu, wd, gf, tm, final_norm, casts=()):
    n = h.shape[0]
    row = pl.BlockSpec((tm, D_MODEL), lambda i: (i, 0))
    cast_in, cast_out, cast_shapes = _cast_specs(casts, n // tm, lambda i: i)
    in_specs = [row] + _ffn_specs(wg, wu, wd) + [_const_spec((1, D_MODEL))]
    outs = pl.pallas_call(
        _hosting(functools.partial(_ffn_kernel, final_norm=final_norm), len(in_specs), 1, len(casts)),
        grid=(n // tm,),
        out_shape=[jax.ShapeDtypeStruct((n, D_MODEL), F32)] + cast_shapes,
        in_specs=in_specs + cast_in,
        out_specs=[row] + cast_out,
        compiler_params=pltpu.CompilerParams(dimension_semantics=("parallel",), vmem_limit_bytes=VMEM_LIMIT),
        name="ffn",
    )(h, g, wg, wu, wd, gf, *[v for v, _ in casts])
    return outs[0], outs[1:]


def _conv_ffn_kernel(h_ref, g_ref, win_ref, cw_ref, wout_ref, zinit_ref, gf_ref, wg_ref, wu_ref, wd_ref,
                     out_ref, ztail_ref, zprev_scr, *, tm):
    @pl.when(pl.program_id(1) == 0)
    def _():
        zprev_scr[...] = zinit_ref[...]

    x = h_ref[...]
    a = _rms(x, g_ref[...]).astype(BF16)
    bcu = _dot(a, win_ref[...])
    gate_b = bcu[:, :D_MODEL]
    z = bcu[:, D_MODEL:2 * D_MODEL] * bcu[:, 2 * D_MODEL:]
    zp = zprev_scr[...]
    row = lax.broadcasted_iota(jnp.int32, (tm, D_MODEL), 0)
    last, last2 = zp[CONV_TAIL - 1:CONV_TAIL], zp[CONV_TAIL - 2:CONV_TAIL - 1]
    z1 = jnp.where(row == 0, last, pltpu.roll(z, 1, 0))
    z2 = jnp.where(row == 0, last2, jnp.where(row == 1, last, pltpu.roll(z, 2, 0)))
    cw = cw_ref[...]
    conv = cw[0:1] * z2 + cw[1:2] * z1 + cw[2:3] * z
    y = x + _dot((gate_b * conv).astype(BF16), wout_ref[...])
    out_ref[...] = _ffn_math(y, gf_ref, wg_ref, wu_ref, wd_ref)
    tail = z[tm - CONV_TAIL:]
    zprev_scr[...] = tail
    ztail_ref[...] = tail


def _conv_ffn_call(h, g, win, cw, wout, zinit, gf, wg, wu, wd, batch, tm, casts=()):
    n = h.shape[0]
    nt = n // batch // tm
    row = pl.BlockSpec((tm, D_MODEL), lambda b, i: (b * nt + i, 0))
    tail_shape = (CONV_TAIL, D_MODEL)
    cast_in, cast_out, cast_shapes = _cast_specs(casts, batch * nt, lambda b, i: b * nt + i)
    in_specs = [row, _const_spec((1, D_MODEL)), _const_spec(win.shape), _const_spec(cw.shape),
                _const_spec(wout.shape), _const_spec(tail_shape)] + _ffn_specs(wg, wu, wd)
    outs = pl.pallas_call(
        _hosting(functools.partial(_conv_ffn_kernel, tm=tm), len(in_specs), 2, len(casts)),
        grid=(batch, nt),
        out_shape=[jax.ShapeDtypeStruct((n, D_MODEL), F32), jax.ShapeDtypeStruct(tail_shape, F32)] + cast_shapes,
        in_specs=in_specs + cast_in,
        out_specs=[row, pl.BlockSpec(tail_shape, lambda b, i: (0, 0))] + cast_out,
        scratch_shapes=[pltpu.VMEM(tail_shape, F32)],
        compiler_params=pltpu.CompilerParams(dimension_semantics=("arbitrary", "arbitrary"),
                                             vmem_limit_bytes=VMEM_LIMIT),
        name="conv_ffn",
    )(h, g, win, cw, wout, zinit, gf, wg, wu, wd, *[v for v, _ in casts])
    return outs[0], outs[1], outs[2:]


def _pool_ffn_kernel(h_ref, g_ref, wp_ref, sc_ref, hinit_ref, gf_ref, wg_ref, wu_ref, wd_ref,
                     out_ref, atail_ref, hist_scr, *, tm, from_start):
    @pl.when(pl.program_id(1) == 0)
    def _():
        hist_scr[...] = hinit_ref[...]

    x = h_ref[...]
    a = _rms(x, g_ref[...])
    ext = jnp.concatenate([hist_scr[...], a], axis=0)
    outs = []
    for gi, win in enumerate(POOL_WINDOWS):
        cols = slice(gi * POOL_GROUP_DIM, (gi + 1) * POOL_GROUP_DIM)
        s = ext[:, cols]
        k = 1
        while k < win:
            s = s + pltpu.roll(s, k, 0)
            k *= 2
        s = s[POOL_HIST:]
        if from_start:
            pos = lax.broadcasted_iota(jnp.int32, (tm, POOL_GROUP_DIM), 0)
            mean = s / jnp.minimum(win, pos + 1).astype(F32)
        else:
            mean = s * (1.0 / win)
        mix = (mean - a[:, cols]).astype(BF16)
        outs.append(_dot(mix, wp_ref[gi]))
    y = x + jnp.concatenate(outs, axis=-1) * sc_ref[...]
    out_ref[...] = _ffn_math(y, gf_ref, wg_ref, wu_ref, wd_ref)
    tail = a[tm - POOL_HIST:]
    hist_scr[...] = tail
    atail_ref[...] = tail


def _pool_ffn_call(h, g, wp, sc, hinit, gf, wg, wu, wd, batch, tm, from_start):
    n = h.shape[0]
    nt = n // batch // tm
    row = pl.BlockSpec((tm, D_MODEL), lambda b, i: (b * nt + i, 0))
    hist_shape = (POOL_HIST, D_MODEL)
    return pl.pallas_call(
        functools.partial(_pool_ffn_kernel, tm=tm, from_start=from_start),
        grid=(batch, nt),
        out_shape=(jax.ShapeDtypeStruct((n, D_MODEL), F32), jax.ShapeDtypeStruct(hist_shape, F32)),
        in_specs=[row, _const_spec((1, D_MODEL)), _const_spec(wp.shape), _const_spec((1, D_MODEL)),
                  _const_spec(hist_shape)] + _ffn_specs(wg, wu, wd),
        out_specs=(row, pl.BlockSpec(hist_shape, lambda b, i: (0, 0))),
        scratch_shapes=[pltpu.VMEM(hist_shape, F32)],
        compiler_params=pltpu.CompilerParams(dimension_semantics=("arbitrary", "arbitrary"),
                                             vmem_limit_bytes=VMEM_LIMIT),
        name="pool_ffn",
    )(h, g, wp, sc, hinit, gf, wg, wu, wd)


ATTN_TM = 512
ATTN_LOOKAHEAD = 2
ROW_TM = 1024
CONV_TM = 1024


def kernel(x, meta_tokens, rel_bias_table, norm_mix, norm_ffn, norm_final, attn_w_qkv, attn_b_qkv, attn_w_o,
           attn_b_o, attn_sinks, conv_w_in, conv_w, conv_w_out, pool_w, pool_scale, ffn_w_gate, ffn_w_up,
           ffn_w_down):
    batch, seq, _ = x.shape
    depth = norm_mix.shape[0]
    n_mixers = 3
    hr = x.reshape(batch * seq, D_MODEL)
    hm = meta_tokens.astype(F32)
    row = lambda v: v.reshape(1, -1)
    meta_codes = _meta_codes()

    ffn_bf16, conv_bf16 = {}, {}

    def ffn_casts(layers):
        return [(w, l) for l in layers for w in (ffn_w_gate, ffn_w_up, ffn_w_down)]

    for i in range(depth):
        kind, j = i % n_mixers, i // n_mixers
        last = i == depth - 1
        gm, gf = row(norm_mix[i]), row(norm_ffn[i])
        if kind != 0:
            wg, wu, wd = ffn_bf16[i]
        if kind == 0:
            b, bo = attn_b_qkv[j], row(attn_b_o[j])
            bk = row(b[Q_DIM:Q_DIM + KV_DIM])
            tbl_meta = _meta_table(meta_codes, rel_bias_table, attn_sinks[j])
            tbl_real = _real_table(rel_bias_table, attn_sinks[j])
            hm, km, vmT, wqvT, bqvT, wk, wo = _attn_meta_call(hm, gm, attn_w_qkv, j, row(b), tbl_meta,
                                                              attn_w_o, bo)
            hosted = [l for l in (i, i + 2) if l < depth and (l == i or l % n_mixers == 2)]
            hr, cast = _attn_call(hr, gm, wqvT, bqvT, wk, bk, km, vmT, tbl_real, wo, bo, batch, ATTN_TM,
                                  ffn_casts(hosted))
            for n, l in enumerate(hosted):
                ffn_bf16[l] = cast[3 * n:3 * n + 3]
            wg, wu, wd = ffn_bf16[i]
            if not last:
                hm, _ = _ffn_call(hm, gf, wg, wu, wd, row(norm_final), N_META, False)
            conv_next = i + 1 < depth and (i + 1) % n_mixers == 1
            casts = ([(conv_w_in, (i + 1) // n_mixers), (conv_w_out, (i + 1) // n_mixers)] + ffn_casts([i + 1])
                     if conv_next else [])
            hr, cast = _ffn_call(hr, gf, wg, wu, wd, row(norm_final), ROW_TM, last, casts)
            if conv_next:
                conv_bf16[i + 1], ffn_bf16[i + 1] = cast[:2], cast[2:]
        elif kind == 1:
            win, wout = conv_bf16[i]
            hm, ztail, _ = _conv_ffn_call(hm, gm, win, conv_w[j], wout, jnp.zeros((CONV_TAIL, D_MODEL), F32),
                                          gf, wg, wu, wd, 1, N_META)
            hr, _, _ = _conv_ffn_call(hr, gm, win, conv_w[j], wout, ztail, gf, wg, wu, wd, batch, CONV_TM)
        else:
            wp, sc = pool_w[j].astype(BF16), row(pool_scale[j])
            hm, atail = _pool_ffn_call(hm, gm, wp, sc, jnp.zeros((POOL_HIST, D_MODEL), F32),
                                       gf, wg, wu, wd, 1, N_META, True)
            hr, _ = _pool_ffn_call(hr, gm, wp, sc, atail, gf, wg, wu, wd, batch, ROW_TM, False)
    return hr.reshape(batch, seq, D_MODEL)
```

```python
import functools
import math

import numpy as np
import jax
import jax.numpy as jnp
from jax import lax
from jax.experimental import pallas as pl
from jax.experimental.pallas import tpu as pltpu

F32 = jnp.float32
BF16 = jnp.bfloat16

D_MODEL = 1024
N_META = 16
RMS_EPS = 1e-6
N_HEADS = 16
N_KV = 4
HEAD_DIM = 64
GROUP = N_HEADS // N_KV
Q_DIM = N_HEADS * HEAD_DIM
KV_DIM = N_KV * HEAD_DIM
WINDOW = 128
N_BUCKETS = 32
MAX_DISTANCE = 128
POOL_WINDOWS = (2, 4, 8, 16)
POOL_GROUP_DIM = D_MODEL // len(POOL_WINDOWS)
POOL_HIST = 16
CONV_TAIL = 8
LANES = 128
FFN_LEAD_ROWS = 256
BF16_SUBLANES = 16
LOG2E = math.log2(math.e)

Q_BLK = 128
KV_PAIR = 2
UNIT_LANES = KV_PAIR * GROUP * Q_BLK
BAND_ROWS = WINDOW + Q_BLK
SINK_ROW = BAND_ROWS + N_META
KEY_ROWS = SINK_ROW + N_META
N_NEAR = WINDOW // Q_BLK
ONES_ROWS = 16

META_KEY_COLS = 128
CODE_SINK = N_BUCKETS
CODE_MASKED = N_BUCKETS + 1

VMEM_LIMIT = 56 * 1024 * 1024


def _rel_bucket_np(dist):
    max_exact = N_BUCKETS // 2
    d = np.maximum(dist, 0)
    df = np.maximum(d, 1).astype(np.float64)
    large = max_exact + (np.log(df / max_exact) / math.log(MAX_DISTANCE / max_exact)
                         * (N_BUCKETS - max_exact)).astype(np.int64)
    large = np.minimum(large, N_BUCKETS - 1)
    return np.where(d < max_exact, d, large).astype(np.int32)


def _meta_codes():
    i = np.arange(N_META)[:, None]
    m = np.arange(N_META)[None, :]
    code = np.full((N_META, META_KEY_COLS), CODE_MASKED, np.int32)
    code[:, :N_META] = np.where(i >= m, _rel_bucket_np(i - m), CODE_MASKED)
    code[:, N_META] = CODE_SINK
    return code[None]


def _band_codes():
    return np.stack([_rel_bucket_np(np.arange(WINDOW)), np.zeros(WINDOW, np.int32)])


def _near_meta_codes():
    m = np.arange(N_META)[:, None]
    i = np.arange(Q_BLK)[None, :]
    return np.stack([_rel_bucket_np(N_META + n * Q_BLK + i - m) for n in range(N_NEAR)])


def _const_spec(shape):
    zeros = (0,) * len(shape)
    return pl.BlockSpec(shape, lambda *_: zeros, pipeline_mode=pl.Buffered(1))


def _hosting(kernel_fn, n_in, n_out, n_cast):
    def wrapped(*refs):
        ins, rest = refs[:n_in], refs[n_in:]
        cast_in, rest = rest[:n_cast], rest[n_cast:]
        outs, rest = rest[:n_out], rest[n_out:]
        cast_out, scratch = rest[:n_cast], rest[n_cast:]
        for src, dst in zip(cast_in, cast_out):
            dst[...] = src[...].astype(BF16)
        kernel_fn(*ins, *outs, *scratch)
    return wrapped


def _cast_specs(casts, steps, flat_step):
    in_specs, out_specs, out_shapes = [], [], []
    for w, layer in casts:
        _, rows, cols = w.shape
        n_slices = max(n for n in range(1, steps + 1)
                       if steps % n == 0 and rows % n == 0 and (rows // n) % BF16_SUBLANES == 0)
        per = steps // n_slices
        blk = rows // n_slices
        in_specs.append(pl.BlockSpec((None, blk, cols), lambda *idx, l=layer, p=per: (l, flat_step(*idx) // p, 0)))
        out_specs.append(pl.BlockSpec((blk, cols), lambda *idx, p=per: (flat_step(*idx) // p, 0)))
        out_shapes.append(jax.ShapeDtypeStruct((rows, cols), BF16))
    return in_specs, out_specs, out_shapes


def _smem_spec():
    return pl.BlockSpec(memory_space=pltpu.SMEM)


def _rms(x, g):
    return x * lax.rsqrt(jnp.mean(x * x, axis=-1, keepdims=True) + RMS_EPS) * g


def _dot(a, b):
    return jnp.dot(a, b, preferred_element_type=F32)


def _dot_nt(a, b):
    return lax.dot_general(a, b, (((1,), (1,)), ((), ())), preferred_element_type=F32)


def _dot_tn(a, b):
    return lax.dot_general(a, b, (((0,), (0,)), ((), ())), preferred_element_type=F32)


def _select_by_code(code, tbl_ref, h, otherwise):
    val = otherwise
    for b in range(N_BUCKETS):
        val = jnp.where(code == b, tbl_ref[b, h], val)
    return val


def _meta_table_kernel(code_ref, tbl_ref, sink_ref, out_ref):
    n_var, rows, _ = code_ref.shape
    for v in range(n_var):
        code = code_ref[v]
        for h in range(N_HEADS):
            kh, g = divmod(h, GROUP)
            val = _select_by_code(code, tbl_ref, h, jnp.where(code == CODE_SINK, sink_ref[h], -jnp.inf))
            out_ref[v, kh, g * rows:(g + 1) * rows, :] = val


def _meta_table(codes, rel_table, sinks):
    n_var, rows, cols = codes.shape
    return pl.pallas_call(
        _meta_table_kernel,
        out_shape=jax.ShapeDtypeStruct((n_var, N_KV, GROUP * rows, cols), F32),
        in_specs=[pl.BlockSpec(memory_space=pltpu.VMEM), _smem_spec(), _smem_spec()],
        out_specs=pl.BlockSpec(memory_space=pltpu.VMEM),
        name="meta_bias_table",
    )(jnp.asarray(codes), rel_table, sinks)


def _real_table_kernel(band_code_ref, meta_code_ref, tbl_ref, sink_ref, out_ref):
    neg = -jnp.inf
    band_code = band_code_ref[0:1, :]
    t = lax.broadcasted_iota(jnp.int32, (BAND_ROWS, WINDOW), 0)
    i = lax.broadcasted_iota(jnp.int32, (BAND_ROWS, WINDOW), 1)
    in_window = (i < t) & (t <= i + WINDOW)
    row16 = lax.broadcasted_iota(jnp.int32, (N_META, Q_BLK), 0)
    for h in range(N_HEADS):
        pair, lane_blk = divmod(h, KV_PAIR * GROUP)
        lanes = slice(lane_blk * Q_BLK, (lane_blk + 1) * Q_BLK)
        by_dist = _select_by_code(band_code, tbl_ref, h, jnp.zeros((1, WINDOW), F32)) * LOG2E
        rolled = pltpu.roll(jnp.broadcast_to(by_dist, (WINDOW, WINDOW)), 0, 1, stride=1, stride_axis=0)
        rolled = jnp.concatenate([rolled, rolled[:BAND_ROWS - WINDOW]], axis=0)
        far = jnp.full((N_META, Q_BLK), tbl_ref[N_BUCKETS - 1, h], F32)
        sink_pad = jnp.where(row16 == 0, sink_ref[h] * LOG2E, neg)
        for v in range(1 + N_NEAR):
            if v == 0:
                valid, meta = in_window, far * LOG2E
            else:
                first_key_row = WINDOW - (v - 1) * Q_BLK
                valid = in_window & (t >= first_key_row)
                meta = _select_by_code(meta_code_ref[v - 1], tbl_ref, h, far) * LOG2E
            out_ref[v, pair, 0:BAND_ROWS, lanes] = jnp.where(valid, rolled, neg)[:, :Q_BLK]
            out_ref[v, pair, BAND_ROWS:SINK_ROW, lanes] = meta
            out_ref[v, pair, SINK_ROW:KEY_ROWS, lanes] = sink_pad


def _real_table(rel_table, sinks):
    vmem = pl.BlockSpec(memory_space=pltpu.VMEM)
    return pl.pallas_call(
        _real_table_kernel,
        out_shape=jax.ShapeDtypeStruct((1 + N_NEAR, N_KV // KV_PAIR, KEY_ROWS, UNIT_LANES), F32),
        in_specs=[vmem, vmem, _smem_spec(), _smem_spec()],
        out_specs=vmem,
        name="real_bias_table",
    )(jnp.asarray(_band_codes()), jnp.asarray(_near_meta_codes()), rel_table, sinks)


def _attn_kernel(h_ref, g_ref, wqv_ref, bqv_ref, wk_ref, bk_ref, km_ref, vmT_ref, tbl_ref, wo_ref, bo_ref,
                 out_ref, qT_scr, k_scr, vT_scr, oT_scr, *, tm):
    first_tile = pl.program_id(1) == 0

    @pl.when(first_tile)
    def _():
        k_scr[0:WINDOW, :] = jnp.zeros((WINDOW, KV_DIM), BF16)
        vT_scr[:, 0:WINDOW] = jnp.zeros((KV_DIM, WINDOW), BF16)

    @pl.when(jnp.logical_not(first_tile))
    def _():
        k_scr[0:WINDOW, :] = k_scr[tm:tm + WINDOW, :]
        vT_scr[:, 0:WINDOW] = vT_scr[:, tm:tm + WINDOW]

    x = h_ref[...]
    a = _rms(x, g_ref[...]).astype(BF16)
    bias = jnp.concatenate([bqv_ref[...]] * (tm // LANES), axis=1)
    qvT = _dot_nt(wqv_ref[...], a) + bias
    qT_scr[...] = (qvT[:Q_DIM] * (HEAD_DIM ** -0.5 * LOG2E)).astype(BF16)
    vT_scr[:, WINDOW:] = qvT[Q_DIM:].astype(BF16)
    k_scr[WINDOW:, :] = (_dot(a, wk_ref[...]) + bk_ref[...]).astype(BF16)

    kext = k_scr[...]
    vText = vT_scr[...]
    km = km_ref[...]
    vmT = vmT_ref[...]
    pair_dim = KV_PAIR * HEAD_DIM
    kv_lanes = GROUP * Q_BLK
    zk = jnp.zeros((KEY_ROWS - SINK_ROW, pair_dim), BF16)
    zq = jnp.zeros((HEAD_DIM, kv_lanes), BF16)
    ones = jnp.ones((ONES_ROWS, KEY_ROWS), BF16)

    def scores(j, pair):
        r0 = j * Q_BLK
        cols = slice(pair * pair_dim, (pair + 1) * pair_dim)
        blocks = []
        for c in range(KV_PAIR):
            kh = pair * KV_PAIR + c
            qsT = jnp.concatenate(
                [qT_scr[(kh * GROUP + g) * HEAD_DIM:(kh * GROUP + g + 1) * HEAD_DIM, r0:r0 + Q_BLK]
                 for g in range(GROUP)], axis=1)
            blocks.append(jnp.concatenate([zq] * c + [qsT] + [zq] * (KV_PAIR - 1 - c), axis=1))
        qsT = jnp.concatenate(blocks, axis=0)
        keys = jnp.concatenate([kext[r0:r0 + BAND_ROWS, cols], km[:, cols], zk], axis=0)
        return _dot(keys, qsT)

    def finish(j, pair, sT, sT_next):
        r0 = j * Q_BLK
        cols = slice(pair * pair_dim, (pair + 1) * pair_dim)
        var = jnp.where(first_tile, 1 + j, 0) if j < N_NEAR else 0
        sT = sT + tbl_ref[var, pair]
        m = jnp.max(sT, axis=0, keepdims=True)
        pT = jnp.exp2(sT - m).astype(BF16)
        valsT = jnp.concatenate([vText[cols, r0:r0 + BAND_ROWS], vmT[cols, :]], axis=1)
        den = ones
        if sT_next is not None:
            bits = pltpu.bitcast(sT_next[KEY_ROWS - ONES_ROWS:, :LANES], jnp.uint32)
            zero = pltpu.bitcast((bits >> 16) >> 16, F32).astype(BF16)
            den = jnp.concatenate([ones[:, :LANES] + zero, ones[:, LANES:]], axis=1)
        oT = _dot(jnp.concatenate([valsT, den], axis=0), pT)
        for c in range(KV_PAIR):
            kh = pair * KV_PAIR + c
            lanes = slice(c * kv_lanes, (c + 1) * kv_lanes)
            o = oT[c * HEAD_DIM:(c + 1) * HEAD_DIM, lanes] / oT[pair_dim:pair_dim + 1, lanes]
            for g in range(GROUP):
                h0 = (kh * GROUP + g) * HEAD_DIM
                oT_scr[h0:h0 + HEAD_DIM, r0:r0 + Q_BLK] = o[:, g * Q_BLK:(g + 1) * Q_BLK].astype(BF16)

    units = [(j, pair) for j in range(tm // Q_BLK) for pair in range(N_KV // KV_PAIR)]
    pending = [scores(*u) for u in units[:ATTN_LOOKAHEAD]]
    for idx, unit in enumerate(units):
        ahead = idx + ATTN_LOOKAHEAD
        if ahead < len(units):
            pending.append(scores(*units[ahead]))
        sT = pending.pop(0)
        finish(*unit, sT, pending[-1] if pending else None)
    out_ref[...] = x + _dot_tn(oT_scr[...], wo_ref[...]) + bo_ref[...]


def _attn_call(h, g, wqvT, bqvT, wk, bk, km, vmT, tbl, wo, bo, batch, tm, casts=()):
    n = h.shape[0]
    nt = n // batch // tm
    row = pl.BlockSpec((tm, D_MODEL), lambda b, i: (b * nt + i, 0))
    cast_in, cast_out, cast_shapes = _cast_specs(casts, batch * nt, lambda b, i: b * nt + i)
    in_specs = [row, _const_spec((1, D_MODEL)), _const_spec(wqvT.shape), _const_spec(bqvT.shape),
                _const_spec(wk.shape), _const_spec((1, KV_DIM)), _const_spec(km.shape), _const_spec(vmT.shape),
                _const_spec(tbl.shape), _const_spec(wo.shape), _const_spec((1, D_MODEL))]
    outs = pl.pallas_call(
        _hosting(functools.partial(_attn_kernel, tm=tm), len(in_specs), 1, len(casts)),
        grid=(batch, nt),
        out_shape=[jax.ShapeDtypeStruct((n, D_MODEL), F32)] + cast_shapes,
        in_specs=in_specs + cast_in,
        out_specs=[row] + cast_out,
        scratch_shapes=[pltpu.VMEM((Q_DIM, tm), BF16), pltpu.VMEM((WINDOW + tm, KV_DIM), BF16),
                        pltpu.VMEM((KV_DIM, WINDOW + tm), BF16), pltpu.VMEM((Q_DIM, tm), BF16)],
        compiler_params=pltpu.CompilerParams(dimension_semantics=("arbitrary", "arbitrary"),
                                             vmem_limit_bytes=VMEM_LIMIT),
        name="attn",
    )(h, g, wqvT, bqvT, wk, bk, km, vmT, tbl, wo, bo, *[v for v, _ in casts])
    return outs[0], outs[1:]


def _attn_meta_kernel(h_ref, g_ref, w_ref, b_ref, tbl_ref, wo_ref, bo_ref,
                      out_ref, km_ref, vmT_ref, wqvT_ref, bqvT_ref, wk_ref, wo16_ref):
    w = w_ref[...]
    b = b_ref[...]
    wq, wk, wv = w[:, :Q_DIM], w[:, Q_DIM:Q_DIM + KV_DIM], w[:, Q_DIM + KV_DIM:]
    wqvT = jnp.concatenate([wq.T, wv.T], axis=0).astype(BF16)
    b_qv = jnp.concatenate([b[:, :Q_DIM], b[:, Q_DIM + KV_DIM:]], axis=1)
    bqvT = jnp.broadcast_to(b_qv, (LANES, Q_DIM + KV_DIM)).T
    wk16 = wk.astype(BF16)
    wo16 = wo_ref[...].astype(BF16)
    wqvT_ref[...] = wqvT
    bqvT_ref[...] = bqvT
    wk_ref[...] = wk16
    wo16_ref[...] = wo16

    x = h_ref[...]
    a = _rms(x, g_ref[...]).astype(BF16)
    a2 = jnp.concatenate([a, jnp.zeros_like(a)], axis=0)
    lane = lax.broadcasted_iota(jnp.int32, (KV_DIM, 2 * N_META), 1)
    vT = _dot_nt(wqvT[Q_DIM:], a2) + jnp.where(lane < N_META, bqvT[Q_DIM:, :2 * N_META], 0.0)
    vmT_ref[...] = vT.astype(BF16)
    q = ((_dot_nt(a, wqvT[:Q_DIM]) + b[:, :Q_DIM]) * (HEAD_DIM ** -0.5)).astype(BF16)
    k = (_dot(a, wk16) + b[:, Q_DIM:Q_DIM + KV_DIM]).astype(BF16)
    v = (_dot_nt(a, wqvT[Q_DIM:]) + b[:, Q_DIM + KV_DIM:]).astype(BF16)
    km_ref[...] = k
    pad = META_KEY_COLS - N_META
    vcat = jnp.concatenate([v, jnp.zeros((pad, KV_DIM), BF16)], axis=0)
    pieces = []
    for kh in range(N_KV):
        cols = slice(kh * HEAD_DIM, (kh + 1) * HEAD_DIM)
        qs = jnp.concatenate([q[:, (kh * GROUP + g) * HEAD_DIM:(kh * GROUP + g + 1) * HEAD_DIM]
                              for g in range(GROUP)], axis=0)
        kcat = jnp.concatenate([k[:, cols], jnp.zeros((pad, HEAD_DIM), BF16)], axis=0)
        s = _dot_nt(qs, kcat) + tbl_ref[0, kh]
        p = jnp.exp(s - jnp.max(s, axis=-1, keepdims=True))
        l = jnp.sum(p, axis=-1, keepdims=True)
        o = _dot(p.astype(BF16), vcat)[:, cols] / l
        pieces += [o[g * N_META:(g + 1) * N_META] for g in range(GROUP)]
    o_all = jnp.concatenate(pieces, axis=-1).astype(BF16)
    out_ref[...] = x + _dot(o_all, wo16) + bo_ref[...]


def _attn_meta_call(h, g, w_stack, layer, b, tbl, wo_stack, bo):
    whole = lambda shape: pl.BlockSpec(shape, lambda i: (0,) * len(shape))
    of_layer = lambda stack: pl.BlockSpec((None,) + stack.shape[1:], lambda i: (layer, 0, 0))
    out_shapes = ((N_META, D_MODEL, F32), (N_META, KV_DIM, BF16), (KV_DIM, 2 * N_META, BF16),
                  (Q_DIM + KV_DIM, D_MODEL, BF16), (Q_DIM + KV_DIM, LANES, F32), (D_MODEL, KV_DIM, BF16),
                  (Q_DIM, D_MODEL, BF16))
    return pl.pallas_call(
        _attn_meta_kernel,
        grid=(1,),
        out_shape=tuple(jax.ShapeDtypeStruct((r, c), dt) for r, c, dt in out_shapes),
        in_specs=[whole(h.shape), whole(g.shape), of_layer(w_stack), whole(b.shape), whole(tbl.shape),
                  of_layer(wo_stack), whole(bo.shape)],
        out_specs=tuple(whole((r, c)) for r, c, _ in out_shapes),
        compiler_params=pltpu.CompilerParams(dimension_semantics=("arbitrary",), vmem_limit_bytes=VMEM_LIMIT),
        name="attn_meta",
    )(h, g, w_stack, b, tbl, wo_stack, bo)


def _ffn_math(x, g_ref, wg_ref, wu_ref, wd_ref):
    rows = x.shape[0]
    lead = FFN_LEAD_ROWS if rows > 2 * FFN_LEAD_ROWS else rows
    chunks = [x[:lead]] + ([x[lead:]] if lead < rows else [])
    hmids = []
    for xc in chunks:
        a = _rms(xc, g_ref[...]).astype(BF16)
        gate = _dot(a, wg_ref[...])
        up = _dot(a, wu_ref[...])
        hmids.append((gate * (1.0 / (1.0 + jnp.exp(-gate))) * up).astype(BF16))
    outs = [xc + _dot(hm, wd_ref[...]) for xc, hm in zip(chunks, hmids)]
    return outs[0] if len(outs) == 1 else jnp.concatenate(outs, axis=0)


def _ffn_kernel(h_ref, g_ref, wg_ref, wu_ref, wd_ref, gf_ref, out_ref, *, final_norm):
    y = _ffn_math(h_ref[...], g_ref, wg_ref, wu_ref, wd_ref)
    out_ref[...] = _rms(y, gf_ref[...]) if final_norm else y


def _ffn_specs(wg, wu, wd):
    return [_const_spec((1, D_MODEL)), _const_spec(wg.shape), _const_spec(wu.shape), _const_spec(wd.shape)]


def _ffn_call(h, g, wg, wu, wd, gf, tm, final_norm, casts=()):
    n = h.shape[0]
    row = pl.BlockSpec((tm, D_MODEL), lambda i: (i, 0))
    cast_in, cast_out, cast_shapes = _cast_specs(casts, n // tm, lambda i: i)
    in_specs = [row] + _ffn_specs(wg, wu, wd) + [_const_spec((1, D_MODEL))]
    outs = pl.pallas_call(
        _hosting(functools.partial(_ffn_kernel, final_norm=final_norm), len(in_specs), 1, len(casts)),
        grid=(n // tm,),
        out_shape=[jax.ShapeDtypeStruct((n, D_MODEL), F32)] + cast_shapes,
        in_specs=in_specs + cast_in,
        out_specs=[row] + cast_out,
        compiler_params=pltpu.CompilerParams(dimension_semantics=("parallel",), vmem_limit_bytes=VMEM_LIMIT),
        name="ffn",
    )(h, g, wg, wu, wd, gf, *[v for v, _ in casts])
    return outs[0], outs[1:]


def _conv_ffn_kernel(h_ref, g_ref, win_ref, cw_ref, wout_ref, zinit_ref, gf_ref, wg_ref, wu_ref, wd_ref,
                     out_ref, ztail_ref, zprev_scr, *, tm):
    @pl.when(pl.program_id(1) == 0)
    def _():
        zprev_scr[...] = zinit_ref[...]

    x = h_ref[...]
    a = _rms(x, g_ref[...]).astype(BF16)
    bcu = _dot(a, win_ref[...])
    gate_b = bcu[:, :D_MODEL]
    z = bcu[:, D_MODEL:2 * D_MODEL] * bcu[:, 2 * D_MODEL:]
    zp = zprev_scr[...]
    row = lax.broadcasted_iota(jnp.int32, (tm, D_MODEL), 0)
    last, last2 = zp[CONV_TAIL - 1:CONV_TAIL], zp[CONV_TAIL - 2:CONV_TAIL - 1]
    z1 = jnp.where(row == 0, last, pltpu.roll(z, 1, 0))
    z2 = jnp.where(row == 0, last2, jnp.where(row == 1, last, pltpu.roll(z, 2, 0)))
    cw = cw_ref[...]
    conv = cw[0:1] * z2 + cw[1:2] * z1 + cw[2:3] * z
    y = x + _dot((gate_b * conv).astype(BF16), wout_ref[...])
    out_ref[...] = _ffn_math(y, gf_ref, wg_ref, wu_ref, wd_ref)
    tail = z[tm - CONV_TAIL:]
    zprev_scr[...] = tail
    ztail_ref[...] = tail


def _conv_ffn_call(h, g, win, cw, wout, zinit, gf, wg, wu, wd, batch, tm, casts=()):
    n = h.shape[0]
    nt = n // batch // tm
    row = pl.BlockSpec((tm, D_MODEL), lambda b, i: (b * nt + i, 0))
    tail_shape = (CONV_TAIL, D_MODEL)
    cast_in, cast_out, cast_shapes = _cast_specs(casts, batch * nt, lambda b, i: b * nt + i)
    in_specs = [row, _const_spec((1, D_MODEL)), _const_spec(win.shape), _const_spec(cw.shape),
                _const_spec(wout.shape), _const_spec(tail_shape)] + _ffn_specs(wg, wu, wd)
    outs = pl.pallas_call(
        _hosting(functools.partial(_conv_ffn_kernel, tm=tm), len(in_specs), 2, len(casts)),
        grid=(batch, nt),
        out_shape=[jax.ShapeDtypeStruct((n, D_MODEL), F32), jax.ShapeDtypeStruct(tail_shape, F32)] + cast_shapes,
        in_specs=in_specs + cast_in,
        out_specs=[row, pl.BlockSpec(tail_shape, lambda b, i: (0, 0))] + cast_out,
        scratch_shapes=[pltpu.VMEM(tail_shape, F32)],
        compiler_params=pltpu.CompilerParams(dimension_semantics=("arbitrary", "arbitrary"),
                                             vmem_limit_bytes=VMEM_LIMIT),
        name="conv_ffn",
    )(h, g, win, cw, wout, zinit, gf, wg, wu, wd, *[v for v, _ in casts])
    return outs[0], outs[1], outs[2:]


def _pool_ffn_kernel(h_ref, g_ref, wp_ref, sc_ref, hinit_ref, gf_ref, wg_ref, wu_ref, wd_ref,
                     out_ref, atail_ref, hist_scr, *, tm, from_start):
    @pl.when(pl.program_id(1) == 0)
    def _():
        hist_scr[...] = hinit_ref[...]

    x = h_ref[...]
    a = _rms(x, g_ref[...])
    ext = jnp.concatenate([hist_scr[...], a], axis=0)
    outs = []
    for gi, win in enumerate(POOL_WINDOWS):
        cols = slice(gi * POOL_GROUP_DIM, (gi + 1) * POOL_GROUP_DIM)
        s = ext[:, cols]
        k = 1
        while k < win:
            s = s + pltpu.roll(s, k, 0)
            k *= 2
        s = s[POOL_HIST:]
        if from_start:
            pos = lax.broadcasted_iota(jnp.int32, (tm, POOL_GROUP_DIM), 0)
            mean = s / jnp.minimum(win, pos + 1).astype(F32)
        else:
            mean = s * (1.0 / win)
        mix = (mean - a[:, cols]).astype(BF16)
        outs.append(_dot(mix, wp_ref[gi]))
    y = x + jnp.concatenate(outs, axis=-1) * sc_ref[...]
    out_ref[...] = _ffn_math(y, gf_ref, wg_ref, wu_ref, wd_ref)
    tail = a[tm - POOL_HIST:]
    hist_scr[...] = tail
    atail_ref[...] = tail


def _pool_ffn_call(h, g, wp, sc, hinit, gf, wg, wu, wd, batch, tm, from_start):
    n = h.shape[0]
    nt = n // batch // tm
    row = pl.BlockSpec((tm, D_MODEL), lambda b, i: (b * nt + i, 0))
    hist_shape = (POOL_HIST, D_MODEL)
    return pl.pallas_call(
        functools.partial(_pool_ffn_kernel, tm=tm, from_start=from_start),
        grid=(batch, nt),
        out_shape=(jax.ShapeDtypeStruct((n, D_MODEL), F32), jax.ShapeDtypeStruct(hist_shape, F32)),
        in_specs=[row, _const_spec((1, D_MODEL)), _const_spec(wp.shape), _const_spec((1, D_MODEL)),
                  _const_spec(hist_shape)] + _ffn_specs(wg, wu, wd),
        out_specs=(row, pl.BlockSpec(hist_shape, lambda b, i: (0, 0))),
        scratch_shapes=[pltpu.VMEM(hist_shape, F32)],
        compiler_params=pltpu.CompilerParams(dimension_semantics=("arbitrary", "arbitrary"),
                                             vmem_limit_bytes=VMEM_LIMIT),
        name="pool_ffn",
    )(h, g, wp, sc, hinit, gf, wg, wu, wd)


ATTN_TM = 512
ATTN_LOOKAHEAD = 2
ROW_TM = 1024
CONV_TM = 1024


def kernel(x, meta_tokens, rel_bias_table, norm_mix, norm_ffn, norm_final, attn_w_qkv, attn_b_qkv, attn_w_o,
           attn_b_o, attn_sinks, conv_w_in, conv_w, conv_w_out, pool_w, pool_scale, ffn_w_gate, ffn_w_up,
           ffn_w_down):
    batch, seq, _ = x.shape
    depth = norm_mix.shape[0]
    n_mixers = 3
    hr = x.reshape(batch * seq, D_MODEL)
    hm = meta_tokens.astype(F32)
    row = lambda v: v.reshape(1, -1)
    meta_codes = _meta_codes()

    ffn_bf16, conv_bf16 = {}, {}

    def ffn_casts(layers):
        return [(w, l) for l in layers for w in (ffn_w_gate, ffn_w_up, ffn_w_down)]

    for i in range(depth):
        kind, j = i % n_mixers, i // n_mixers
        last = i == depth - 1
        gm, gf = row(norm_mix[i]), row(norm_ffn[i])
        if kind != 0:
            wg, wu, wd = ffn_bf16[i]
        if kind == 0:
            b, bo = attn_b_qkv[j], row(attn_b_o[j])
            bk = row(b[Q_DIM:Q_DIM + KV_DIM])
            tbl_meta = _meta_table(meta_codes, rel_bias_table, attn_sinks[j])
            tbl_real = _real_table(rel_bias_table, attn_sinks[j])
            hm, km, vmT, wqvT, bqvT, wk, wo = _attn_meta_call(hm, gm, attn_w_qkv, j, row(b), tbl_meta,
                                                              attn_w_o, bo)
            hosted = [l for l in (i, i + 2) if l < depth and (l == i or l % n_mixers == 2)]
            hr, cast = _attn_call(hr, gm, wqvT, bqvT, wk, bk, km, vmT, tbl_real, wo, bo, batch, ATTN_TM,
                                  ffn_casts(hosted))
            for n, l in enumerate(hosted):
                ffn_bf16[l] = cast[3 * n:3 * n + 3]
            wg, wu, wd = ffn_bf16[i]
            if not last:
                hm, _ = _ffn_call(hm, gf, wg, wu, wd, row(norm_final), N_META, False)
            conv_next = i + 1 < depth and (i + 1) % n_mixers == 1
            casts = ([(conv_w_in, (i + 1) // n_mixers), (conv_w_out, (i + 1) // n_mixers)] + ffn_casts([i + 1])
                     if conv_next else [])
            hr, cast = _ffn_call(hr, gf, wg, wu, wd, row(norm_final), ROW_TM, last, casts)
            if conv_next:
                conv_bf16[i + 1], ffn_bf16[i + 1] = cast[:2], cast[2:]
        elif kind == 1:
            win, wout = conv_bf16[i]
            hm, ztail, _ = _conv_ffn_call(hm, gm, win, conv_w[j], wout, jnp.zeros((CONV_TAIL, D_MODEL), F32),
                                          gf, wg, wu, wd, 1, N_META)
            hr, _, _ = _conv_ffn_call(hr, gm, win, conv_w[j], wout, ztail, gf, wg, wu, wd, batch, CONV_TM)
        else:
            wp, sc = pool_w[j].astype(BF16), row(pool_scale[j])
            hm, atail = _pool_ffn_call(hm, gm, wp, sc, jnp.zeros((POOL_HIST, D_MODEL), F32),
                                       gf, wg, wu, wd, 1, N_META, True)
            hr, _ = _pool_ffn_call(hr, gm, wp, sc, atail, gf, wg, wu, wd, batch, ROW_TM, False)
    return hr.reshape(batch, seq, D_MODEL)
```

```python
import functools
import math

import numpy as np
import jax
import jax.numpy as jnp
from jax import lax
from jax.experimental import pallas as pl
from jax.experimental.pallas import tpu as pltpu

F32 = jnp.float32
BF16 = jnp.bfloat16

D_MODEL = 1024
N_META = 16
RMS_EPS = 1e-6
N_HEADS = 16
N_KV = 4
HEAD_DIM = 64
GROUP = N_HEADS // N_KV
Q_DIM = N_HEADS * HEAD_DIM
KV_DIM = N_KV * HEAD_DIM
WINDOW = 128
N_BUCKETS = 32
MAX_DISTANCE = 128
POOL_WINDOWS = (2, 4, 8, 16)
POOL_GROUP_DIM = D_MODEL // len(POOL_WINDOWS)
POOL_HIST = 16
CONV_TAIL = 8
LANES = 128
FFN_LEAD_ROWS = 256
BF16_SUBLANES = 16
LOG2E = math.log2(math.e)

Q_BLK = 128
KV_PAIR = 2
UNIT_LANES = KV_PAIR * GROUP * Q_BLK
BAND_ROWS = WINDOW + Q_BLK
SINK_ROW = BAND_ROWS + N_META
KEY_ROWS = SINK_ROW + N_META
N_NEAR = WINDOW // Q_BLK
ONES_ROWS = 16

META_KEY_COLS = 128
CODE_SINK = N_BUCKETS
CODE_MASKED = N_BUCKETS + 1

VMEM_LIMIT = 56 * 1024 * 1024


def _rel_bucket_np(dist):
    max_exact = N_BUCKETS // 2
    d = np.maximum(dist, 0)
    df = np.maximum(d, 1).astype(np.float64)
    large = max_exact + (np.log(df / max_exact) / math.log(MAX_DISTANCE / max_exact)
                         * (N_BUCKETS - max_exact)).astype(np.int64)
    large = np.minimum(large, N_BUCKETS - 1)
    return np.where(d < max_exact, d, large).astype(np.int32)


def _meta_codes():
    i = np.arange(N_META)[:, None]
    m = np.arange(N_META)[None, :]
    code = np.full((N_META, META_KEY_COLS), CODE_MASKED, np.int32)
    code[:, :N_META] = np.where(i >= m, _rel_bucket_np(i - m), CODE_MASKED)
    code[:, N_META] = CODE_SINK
    return code[None]


def _band_codes():
    return np.stack([_rel_bucket_np(np.arange(WINDOW)), np.zeros(WINDOW, np.int32)])


def _near_meta_codes():
    m = np.arange(N_META)[:, None]
    i = np.arange(Q_BLK)[None, :]
    return np.stack([_rel_bucket_np(N_META + n * Q_BLK + i - m) for n in range(N_NEAR)])


def _const_spec(shape):
    zeros = (0,) * len(shape)
    return pl.BlockSpec(shape, lambda *_: zeros, pipeline_mode=pl.Buffered(1))


def _hosting(kernel_fn, n_in, n_out, n_cast):
    def wrapped(*refs):
        ins, rest = refs[:n_in], refs[n_in:]
        cast_in, rest = rest[:n_cast], rest[n_cast:]
        outs, rest = rest[:n_out], rest[n_out:]
        cast_out, scratch = rest[:n_cast], rest[n_cast:]
        for src, dst in zip(cast_in, cast_out):
            dst[...] = src[...].astype(BF16)
        kernel_fn(*ins, *outs, *scratch)
    return wrapped


def _cast_specs(casts, steps, flat_step):
    in_specs, out_specs, out_shapes = [], [], []
    for w, layer in casts:
        _, rows, cols = w.shape
        n_slices = max(n for n in range(1, steps + 1)
                       if steps % n == 0 and rows % n == 0 and (rows // n) % BF16_SUBLANES == 0)
        per = steps // n_slices
        blk = rows // n_slices
        in_specs.append(pl.BlockSpec((None, blk, cols), lambda *idx, l=layer, p=per: (l, flat_step(*idx) // p, 0)))
        out_specs.append(pl.BlockSpec((blk, cols), lambda *idx, p=per: (flat_step(*idx) // p, 0)))
        out_shapes.append(jax.ShapeDtypeStruct((rows, cols), BF16))
    return in_specs, out_specs, out_shapes


def _smem_spec():
    return pl.BlockSpec(memory_space=pltpu.SMEM)


def _rms(x, g):
    return x * lax.rsqrt(jnp.mean(x * x, axis=-1, keepdims=True) + RMS_EPS) * g


def _dot(a, b):
    return jnp.dot(a, b, preferred_element_type=F32)


def _dot_nt(a, b):
    return lax.dot_general(a, b, (((1,), (1,)), ((), ())), preferred_element_type=F32)


def _dot_tn(a, b):
    return lax.dot_general(a, b, (((0,), (0,)), ((), ())), preferred_element_type=F32)


def _select_by_code(code, tbl_ref, h, otherwise):
    val = otherwise
    for b in range(N_BUCKETS):
        val = jnp.where(code == b, tbl_ref[b, h], val)
    return val


def _meta_table_kernel(code_ref, tbl_ref, sink_ref, out_ref):
    n_var, rows, _ = code_ref.shape
    for v in range(n_var):
        code = code_ref[v]
        for h in range(N_HEADS):
            kh, g = divmod(h, GROUP)
            val = _select_by_code(code, tbl_ref, h, jnp.where(code == CODE_SINK, sink_ref[h], -jnp.inf))
            out_ref[v, kh, g * rows:(g + 1) * rows, :] = val


def _meta_table(codes, rel_table, sinks):
    n_var, rows, cols = codes.shape
    return pl.pallas_call(
        _meta_table_kernel,
        out_shape=jax.ShapeDtypeStruct((n_var, N_KV, GROUP * rows, cols), F32),
        in_specs=[pl.BlockSpec(memory_space=pltpu.VMEM), _smem_spec(), _smem_spec()],
        out_specs=pl.BlockSpec(memory_space=pltpu.VMEM),
        name="meta_bias_table",
    )(jnp.asarray(codes), rel_table, sinks)


def _real_table_kernel(band_code_ref, meta_code_ref, tbl_ref, sink_ref, out_ref):
    neg = -jnp.inf
    band_code = band_code_ref[0:1, :]
    t = lax.broadcasted_iota(jnp.int32, (BAND_ROWS, WINDOW), 0)
    i = lax.broadcasted_iota(jnp.int32, (BAND_ROWS, WINDOW), 1)
    in_window = (i < t) & (t <= i + WINDOW)
    row16 = lax.broadcasted_iota(jnp.int32, (N_META, Q_BLK), 0)
    for h in range(N_HEADS):
        pair, lane_blk = divmod(h, KV_PAIR * GROUP)
        lanes = slice(lane_blk * Q_BLK, (lane_blk + 1) * Q_BLK)
        by_dist = _select_by_code(band_code, tbl_ref, h, jnp.zeros((1, WINDOW), F32)) * LOG2E
        rolled = pltpu.roll(jnp.broadcast_to(by_dist, (WINDOW, WINDOW)), 0, 1, stride=1, stride_axis=0)
        rolled = jnp.concatenate([rolled, rolled[:BAND_ROWS - WINDOW]], axis=0)
        far = jnp.full((N_META, Q_BLK), tbl_ref[N_BUCKETS - 1, h], F32)
        sink_pad = jnp.where(row16 == 0, sink_ref[h] * LOG2E, neg)
        for v in range(1 + N_NEAR):
            if v == 0:
                valid, meta = in_window, far * LOG2E
            else:
                first_key_row = WINDOW - (v - 1) * Q_BLK
                valid = in_window & (t >= first_key_row)
                meta = _select_by_code(meta_code_ref[v - 1], tbl_ref, h, far) * LOG2E
            out_ref[v, pair, 0:BAND_ROWS, lanes] = jnp.where(valid, rolled, neg)[:, :Q_BLK]
            out_ref[v, pair, BAND_ROWS:SINK_ROW, lanes] = meta
            out_ref[v, pair, SINK_ROW:KEY_ROWS, lanes] = sink_pad


def _real_table(rel_table, sinks):
    vmem = pl.BlockSpec(memory_space=pltpu.VMEM)
    return pl.pallas_call(
        _real_table_kernel,
        out_shape=jax.ShapeDtypeStruct((1 + N_NEAR, N_KV // KV_PAIR, KEY_ROWS, UNIT_LANES), F32),
        in_specs=[vmem, vmem, _smem_spec(), _smem_spec()],
        out_specs=vmem,
        name="real_bias_table",
    )(jnp.asarray(_band_codes()), jnp.asarray(_near_meta_codes()), rel_table, sinks)


def _attn_kernel(h_ref, g_ref, wqv_ref, bqv_ref, wk_ref, bk_ref, km_ref, vmT_ref, tbl_ref, wo_ref, bo_ref,
                 out_ref, qT_scr, k_scr, vT_scr, oT_scr, *, tm):
    first_tile = pl.program_id(1) == 0

    @pl.when(first_tile)
    def _():
        k_scr[0:WINDOW, :] = jnp.zeros((WINDOW, KV_DIM), BF16)
        vT_scr[:, 0:WINDOW] = jnp.zeros((KV_DIM, WINDOW), BF16)

    @pl.when(jnp.logical_not(first_tile))
    def _():
        k_scr[0:WINDOW, :] = k_scr[tm:tm + WINDOW, :]
        vT_scr[:, 0:WINDOW] = vT_scr[:, tm:tm + WINDOW]

    x = h_ref[...]
    a = _rms(x, g_ref[...]).astype(BF16)
    bias = jnp.concatenate([bqv_ref[...]] * (tm // LANES), axis=1)
    qvT = _dot_nt(wqv_ref[...], a) + bias
    qT_scr[...] = (qvT[:Q_DIM] * (HEAD_DIM ** -0.5 * LOG2E)).astype(BF16)
    vT_scr[:, WINDOW:] = qvT[Q_DIM:].astype(BF16)
    k_scr[WINDOW:, :] = (_dot(a, wk_ref[...]) + bk_ref[...]).astype(BF16)

    kext = k_scr[...]
    vText = vT_scr[...]
    km = km_ref[...]
    vmT = vmT_ref[...]
    pair_dim = KV_PAIR * HEAD_DIM
    kv_lanes = GROUP * Q_BLK
    zk = jnp.zeros((KEY_ROWS - SINK_ROW, pair_dim), BF16)
    zq = jnp.zeros((HEAD_DIM, kv_lanes), BF16)
    ones = jnp.ones((ONES_ROWS, KEY_ROWS), BF16)

    def scores(j, pair):
        r0 = j * Q_BLK
        cols = slice(pair * pair_dim, (pair + 1) * pair_dim)
        blocks = []
        for c in range(KV_PAIR):
            kh = pair * KV_PAIR + c
            qsT = jnp.concatenate(
                [qT_scr[(kh * GROUP + g) * HEAD_DIM:(kh * GROUP + g + 1) * HEAD_DIM, r0:r0 + Q_BLK]
                 for g in range(GROUP)], axis=1)
            blocks.append(jnp.concatenate([zq] * c + [qsT] + [zq] * (KV_PAIR - 1 - c), axis=1))
        qsT = jnp.concatenate(blocks, axis=0)
        keys = jnp.concatenate([kext[r0:r0 + BAND_ROWS, cols], km[:, cols], zk], axis=0)
        return _dot(keys, qsT)

    def finish(j, pair, sT, sT_next):
        r0 = j * Q_BLK
        cols = slice(pair * pair_dim, (pair + 1) * pair_dim)
        var = jnp.where(first_tile, 1 + j, 0) if j < N_NEAR else 0
        sT = sT + tbl_ref[var, pair]
        m = jnp.max(sT, axis=0, keepdims=True)
        pT = jnp.exp2(sT - m).astype(BF16)
        valsT = jnp.concatenate([vText[cols, r0:r0 + BAND_ROWS], vmT[cols, :]], axis=1)
        den = ones
        if sT_next is not None:
            bits = pltpu.bitcast(sT_next[KEY_ROWS - ONES_ROWS:, :LANES], jnp.uint32)
            zero = pltpu.bitcast((bits >> 16) >> 16, F32).astype(BF16)
            den = jnp.concatenate([ones[:, :LANES] + zero, ones[:, LANES:]], axis=1)
        for c in range(KV_PAIR):
            kh = pair * KV_PAIR + c
            lanes = slice(c * kv_lanes, (c + 1) * kv_lanes)
            oT = _dot(jnp.concatenate([valsT[c * HEAD_DIM:(c + 1) * HEAD_DIM], den], axis=0), pT[:, lanes])
            o = oT[:HEAD_DIM] / oT[HEAD_DIM:HEAD_DIM + 1]
            for g in range(GROUP):
                h0 = (kh * GROUP + g) * HEAD_DIM
                oT_scr[h0:h0 + HEAD_DIM, r0:r0 + Q_BLK] = o[:, g * Q_BLK:(g + 1) * Q_BLK].astype(BF16)

    units = [(j, pair) for j in range(tm // Q_BLK) for pair in range(N_KV // KV_PAIR)]
    pending = [scores(*u) for u in units[:ATTN_LOOKAHEAD]]
    for idx, unit in enumerate(units):
        ahead = idx + ATTN_LOOKAHEAD
        if ahead < len(units):
            pending.append(scores(*units[ahead]))
        sT = pending.pop(0)
        finish(*unit, sT, pending[-1] if pending else None)
    out_ref[...] = x + _dot_tn(oT_scr[...], wo_ref[...]) + bo_ref[...]


def _attn_call(h, g, wqvT, bqvT, wk, bk, km, vmT, tbl, wo, bo, batch, tm, casts=()):
    n = h.shape[0]
    nt = n // batch // tm
    row = pl.BlockSpec((tm, D_MODEL), lambda b, i: (b * nt + i, 0))
    cast_in, cast_out, cast_shapes = _cast_specs(casts, batch * nt, lambda b, i: b * nt + i)
    in_specs = [row, _const_spec((1, D_MODEL)), _const_spec(wqvT.shape), _const_spec(bqvT.shape),
                _const_spec(wk.shape), _const_spec((1, KV_DIM)), _const_spec(km.shape), _const_spec(vmT.shape),
                _const_spec(tbl.shape), _const_spec(wo.shape), _const_spec((1, D_MODEL))]
    outs = pl.pallas_call(
        _hosting(functools.partial(_attn_kernel, tm=tm), len(in_specs), 1, len(casts)),
        grid=(batch, nt),
        out_shape=[jax.ShapeDtypeStruct((n, D_MODEL), F32)] + cast_shapes,
        in_specs=in_specs + cast_in,
        out_specs=[row] + cast_out,
        scratch_shapes=[pltpu.VMEM((Q_DIM, tm), BF16), pltpu.VMEM((WINDOW + tm, KV_DIM), BF16),
                        pltpu.VMEM((KV_DIM, WINDOW + tm), BF16), pltpu.VMEM((Q_DIM, tm), BF16)],
        compiler_params=pltpu.CompilerParams(dimension_semantics=("arbitrary", "arbitrary"),
                                             vmem_limit_bytes=VMEM_LIMIT),
        name="attn",
    )(h, g, wqvT, bqvT, wk, bk, km, vmT, tbl, wo, bo, *[v for v, _ in casts])
    return outs[0], outs[1:]


def _attn_meta_kernel(h_ref, g_ref, w_ref, b_ref, tbl_ref, wo_ref, bo_ref,
                      out_ref, km_ref, vmT_ref, wqvT_ref, bqvT_ref, wk_ref, wo16_ref):
    w = w_ref[...]
    b = b_ref[...]
    wq, wk, wv = w[:, :Q_DIM], w[:, Q_DIM:Q_DIM + KV_DIM], w[:, Q_DIM + KV_DIM:]
    wqvT = jnp.concatenate([wq.T, wv.T], axis=0).astype(BF16)
    b_qv = jnp.concatenate([b[:, :Q_DIM], b[:, Q_DIM + KV_DIM:]], axis=1)
    bqvT = jnp.broadcast_to(b_qv, (LANES, Q_DIM + KV_DIM)).T
    wk16 = wk.astype(BF16)
    wo16 = wo_ref[...].astype(BF16)
    wqvT_ref[...] = wqvT
    bqvT_ref[...] = bqvT
    wk_ref[...] = wk16
    wo16_ref[...] = wo16

    x = h_ref[...]
    a = _rms(x, g_ref[...]).astype(BF16)
    a2 = jnp.concatenate([a, jnp.zeros_like(a)], axis=0)
    lane = lax.broadcasted_iota(jnp.int32, (KV_DIM, 2 * N_META), 1)
    vT = _dot_nt(wqvT[Q_DIM:], a2) + jnp.where(lane < N_META, bqvT[Q_DIM:, :2 * N_META], 0.0)
    vmT_ref[...] = vT.astype(BF16)
    q = ((_dot_nt(a, wqvT[:Q_DIM]) + b[:, :Q_DIM]) * (HEAD_DIM ** -0.5)).astype(BF16)
    k = (_dot(a, wk16) + b[:, Q_DIM:Q_DIM + KV_DIM]).astype(BF16)
    v = (_dot_nt(a, wqvT[Q_DIM:]) + b[:, Q_DIM + KV_DIM:]).astype(BF16)
    km_ref[...] = k
    pad = META_KEY_COLS - N_META
    vcat = jnp.concatenate([v, jnp.zeros((pad, KV_DIM), BF16)], axis=0)
    pieces = []
    for kh in range(N_KV):
        cols = slice(kh * HEAD_DIM, (kh + 1) * HEAD_DIM)
        qs = jnp.concatenate([q[:, (kh * GROUP + g) * HEAD_DIM:(kh * GROUP + g + 1) * HEAD_DIM]
                              for g in range(GROUP)], axis=0)
        kcat = jnp.concatenate([k[:, cols], jnp.zeros((pad, HEAD_DIM), BF16)], axis=0)
        s = _dot_nt(qs, kcat) + tbl_ref[0, kh]
        p = jnp.exp(s - jnp.max(s, axis=-1, keepdims=True))
        l = jnp.sum(p, axis=-1, keepdims=True)
        o = _dot(p.astype(BF16), vcat)[:, cols] / l
        pieces += [o[g * N_META:(g + 1) * N_META] for g in range(GROUP)]
    o_all = jnp.concatenate(pieces, axis=-1).astype(BF16)
    out_ref[...] = x + _dot(o_all, wo16) + bo_ref[...]


def _attn_meta_call(h, g, w_stack, layer, b, tbl, wo_stack, bo):
    whole = lambda shape: pl.BlockSpec(shape, lambda i: (0,) * len(shape))
    of_layer = lambda stack: pl.BlockSpec((None,) + stack.shape[1:], lambda i: (layer, 0, 0))
    out_shapes = ((N_META, D_MODEL, F32), (N_META, KV_DIM, BF16), (KV_DIM, 2 * N_META, BF16),
                  (Q_DIM + KV_DIM, D_MODEL, BF16), (Q_DIM + KV_DIM, LANES, F32), (D_MODEL, KV_DIM, BF16),
                  (Q_DIM, D_MODEL, BF16))
    return pl.pallas_call(
        _attn_meta_kernel,
        grid=(1,),
        out_shape=tuple(jax.ShapeDtypeStruct((r, c), dt) for r, c, dt in out_shapes),
        in_specs=[whole(h.shape), whole(g.shape), of_layer(w_stack), whole(b.shape), whole(tbl.shape),
                  of_layer(wo_stack), whole(bo.shape)],
        out_specs=tuple(whole((r, c)) for r, c, _ in out_shapes),
        compiler_params=pltpu.CompilerParams(dimension_semantics=("arbitrary",), vmem_limit_bytes=VMEM_LIMIT),
        name="attn_meta",
    )(h, g, w_stack, b, tbl, wo_stack, bo)


def _ffn_math(x, g_ref, wg_ref, wu_ref, wd_ref):
    rows = x.shape[0]
    lead = FFN_LEAD_ROWS if rows > 2 * FFN_LEAD_ROWS else rows
    chunks = [x[:lead]] + ([x[lead:]] if lead < rows else [])
    hmids = []
    for xc in chunks:
        a = _rms(xc, g_ref[...]).astype(BF16)
        gate = _dot(a, wg_ref[...])
        up = _dot(a, wu_ref[...])
        hmids.append((gate * (1.0 / (1.0 + jnp.exp(-gate))) * up).astype(BF16))
    outs = [xc + _dot(hm, wd_ref[...]) for xc, hm in zip(chunks, hmids)]
    return outs[0] if len(outs) == 1 else jnp.concatenate(outs, axis=0)


def _ffn_kernel(h_ref, g_ref, wg_ref, wu_ref, wd_ref, gf_ref, out_ref, *, final_norm):
    y = _ffn_math(h_ref[...], g_ref, wg_ref, wu_ref, wd_ref)
    out_ref[...] = _rms(y, gf_ref[...]) if final_norm else y


def _ffn_specs(wg, wu, wd):
    return [_const_spec((1, D_MODEL)), _const_spec(wg.shape), _const_spec(wu.shape), _const_spec(wd.shape)]


def _ffn_call(h, g, wg, wu, wd, gf, tm, final_norm, casts=()):
    n = h.shape[0]
    row = pl.BlockSpec((tm, D_MODEL), lambda i: (i, 0))
    cast_in, cast_out, cast_shapes = _cast_specs(casts, n // tm, lambda i: i)
    in_specs = [row] + _ffn_specs(wg, wu, wd) + [_const_spec((1, D_MODEL))]
    outs = pl.pallas_call(
        _hosting(functools.partial(_ffn_kernel, final_norm=final_norm), len(in_specs), 1, len(casts)),
        grid=(n // tm,),
        out_shape=[jax.ShapeDtypeStruct((n, D_MODEL), F32)] + cast_shapes,
        in_specs=in_specs + cast_in,
        out_specs=[row] + cast_out,
        compiler_params=pltpu.CompilerParams(dimension_semantics=("parallel",), vmem_limit_bytes=VMEM_LIMIT),
        name="ffn",
    )(h, g, wg, wu, wd, gf, *[v for v, _ in casts])
    return outs[0], outs[1:]


def _conv_ffn_kernel(h_ref, g_ref, win_ref, cw_ref, wout_ref, zinit_ref, gf_ref, wg_ref, wu_ref, wd_ref,
                     out_ref, ztail_ref, zprev_scr, *, tm):
    @pl.when(pl.program_id(1) == 0)
    def _():
        zprev_scr[...] = zinit_ref[...]

    x = h_ref[...]
    a = _rms(x, g_ref[...]).astype(BF16)
    bcu = _dot(a, win_ref[...])
    gate_b = bcu[:, :D_MODEL]
    z = bcu[:, D_MODEL:2 * D_MODEL] * bcu[:, 2 * D_MODEL:]
    zp = zprev_scr[...]
    row = lax.broadcasted_iota(jnp.int32, (tm, D_MODEL), 0)
    last, last2 = zp[CONV_TAIL - 1:CONV_TAIL], zp[CONV_TAIL - 2:CONV_TAIL - 1]
    z1 = jnp.where(row == 0, last, pltpu.roll(z, 1, 0))
    z2 = jnp.where(row == 0, last2, jnp.where(row == 1, last, pltpu.roll(z, 2, 0)))
    cw = cw_ref[...]
    conv = cw[0:1] * z2 + cw[1:2] * z1 + cw[2:3] * z
    y = x + _dot((gate_b * conv).astype(BF16), wout_ref[...])
    out_ref[...] = _ffn_math(y, gf_ref, wg_ref, wu_ref, wd_ref)
    tail = z[tm - CONV_TAIL:]
    zprev_scr[...] = tail
    ztail_ref[...] = tail


def _conv_ffn_call(h, g, win, cw, wout, zinit, gf, wg, wu, wd, batch, tm, casts=()):
    n = h.shape[0]
    nt = n // batch // tm
    row = pl.BlockSpec((tm, D_MODEL), lambda b, i: (b * nt + i, 0))
    tail_shape = (CONV_TAIL, D_MODEL)
    cast_in, cast_out, cast_shapes = _cast_specs(casts, batch * nt, lambda b, i: b * nt + i)
    in_specs = [row, _const_spec((1, D_MODEL)), _const_spec(win.shape), _const_spec(cw.shape),
                _const_spec(wout.shape), _const_spec(tail_shape)] + _ffn_specs(wg, wu, wd)
    outs = pl.pallas_call(
        _hosting(functools.partial(_conv_ffn_kernel, tm=tm), len(in_specs), 2, len(casts)),
        grid=(batch, nt),
        out_shape=[jax.ShapeDtypeStruct((n, D_MODEL), F32), jax.ShapeDtypeStruct(tail_shape, F32)] + cast_shapes,
        in_specs=in_specs + cast_in,
        out_specs=[row, pl.BlockSpec(tail_shape, lambda b, i: (0, 0))] + cast_out,
        scratch_shapes=[pltpu.VMEM(tail_shape, F32)],
        compiler_params=pltpu.CompilerParams(dimension_semantics=("arbitrary", "arbitrary"),
                                             vmem_limit_bytes=VMEM_LIMIT),
        name="conv_ffn",
    )(h, g, win, cw, wout, zinit, gf, wg, wu, wd, *[v for v, _ in casts])
    return outs[0], outs[1], outs[2:]


def _pool_ffn_kernel(h_ref, g_ref, wp_ref, sc_ref, hinit_ref, gf_ref, wg_ref, wu_ref, wd_ref,
                     out_ref, atail_ref, hist_scr, *, tm, from_start):
    @pl.when(pl.program_id(1) == 0)
    def _():
        hist_scr[...] = hinit_ref[...]

    x = h_ref[...]
    a = _rms(x, g_ref[...])
    ext = jnp.concatenate([hist_scr[...], a], axis=0)
    outs = []
    for gi, win in enumerate(POOL_WINDOWS):
        cols = slice(gi * POOL_GROUP_DIM, (gi + 1) * POOL_GROUP_DIM)
        s = ext[:, cols]
        k = 1
        while k < win:
            s = s + pltpu.roll(s, k, 0)
            k *= 2
        s = s[POOL_HIST:]
        if from_start:
            pos = lax.broadcasted_iota(jnp.int32, (tm, POOL_GROUP_DIM), 0)
            mean = s / jnp.minimum(win, pos + 1).astype(F32)
        else:
            mean = s * (1.0 / win)
        mix = (mean - a[:, cols]).astype(BF16)
        outs.append(_dot(mix, wp_ref[gi]))
    y = x + jnp.concatenate(outs, axis=-1) * sc_ref[...]
    out_ref[...] = _ffn_math(y, gf_ref, wg_ref, wu_ref, wd_ref)
    tail = a[tm - POOL_HIST:]
    hist_scr[...] = tail
    atail_ref[...] = tail


def _pool_ffn_call(h, g, wp, sc, hinit, gf, wg, wu, wd, batch, tm, from_start):
    n = h.shape[0]
    nt = n // batch // tm
    row = pl.BlockSpec((tm, D_MODEL), lambda b, i: (b * nt + i, 0))
    hist_shape = (POOL_HIST, D_MODEL)
    return pl.pallas_call(
        functools.partial(_pool_ffn_kernel, tm=tm, from_start=from_start),
        grid=(batch, nt),
        out_shape=(jax.ShapeDtypeStruct((n, D_MODEL), F32), jax.ShapeDtypeStruct(hist_shape, F32)),
        in_specs=[row, _const_spec((1, D_MODEL)), _const_spec(wp.shape), _const_spec((1, D_MODEL)),
                  _const_spec(hist_shape)] + _ffn_specs(wg, wu, wd),
        out_specs=(row, pl.BlockSpec(hist_shape, lambda b, i: (0, 0))),
        scratch_shapes=[pltpu.VMEM(hist_shape, F32)],
        compiler_params=pltpu.CompilerParams(dimension_semantics=("arbitrary", "arbitrary"),
                                             vmem_limit_bytes=VMEM_LIMIT),
        name="pool_ffn",
    )(h, g, wp, sc, hinit, gf, wg, wu, wd)


ATTN_TM = 512
ATTN_LOOKAHEAD = 1
ROW_TM = 1024
CONV_TM = 1024


def kernel(x, meta_tokens, rel_bias_table, norm_mix, norm_ffn, norm_final, attn_w_qkv, attn_b_qkv, attn_w_o,
           attn_b_o, attn_sinks, conv_w_in, conv_w, conv_w_out, pool_w, pool_scale, ffn_w_gate, ffn_w_up,
           ffn_w_down):
    batch, seq, _ = x.shape
    depth = norm_mix.shape[0]
    n_mixers = 3
    hr = x.reshape(batch * seq, D_MODEL)
    hm = meta_tokens.astype(F32)
    row = lambda v: v.reshape(1, -1)
    meta_codes = _meta_codes()

    ffn_bf16, conv_bf16 = {}, {}

    def ffn_casts(layers):
        return [(w, l) for l in layers for w in (ffn_w_gate, ffn_w_up, ffn_w_down)]

    for i in range(depth):
        kind, j = i % n_mixers, i // n_mixers
        last = i == depth - 1
        gm, gf = row(norm_mix[i]), row(norm_ffn[i])
        if kind != 0:
            wg, wu, wd = ffn_bf16[i]
        if kind == 0:
            b, bo = attn_b_qkv[j], row(attn_b_o[j])
            bk = row(b[Q_DIM:Q_DIM + KV_DIM])
            tbl_meta = _meta_table(meta_codes, rel_bias_table, attn_sinks[j])
            tbl_real = _real_table(rel_bias_table, attn_sinks[j])
            hm, km, vmT, wqvT, bqvT, wk, wo = _attn_meta_call(hm, gm, attn_w_qkv, j, row(b), tbl_meta,
                                                              attn_w_o, bo)
            hosted = [l for l in (i, i + 2) if l < depth and (l == i or l % n_mixers == 2)]
            hr, cast = _attn_call(hr, gm, wqvT, bqvT, wk, bk, km, vmT, tbl_real, wo, bo, batch, ATTN_TM,
                                  ffn_casts(hosted))
            for n, l in enumerate(hosted):
                ffn_bf16[l] = cast[3 * n:3 * n + 3]
            wg, wu, wd = ffn_bf16[i]
            if not last:
                hm, _ = _ffn_call(hm, gf, wg, wu, wd, row(norm_final), N_META, False)
            conv_next = i + 1 < depth and (i + 1) % n_mixers == 1
            casts = ([(conv_w_in, (i + 1) // n_mixers), (conv_w_out, (i + 1) // n_mixers)] + ffn_casts([i + 1])
                     if conv_next else [])
            hr, cast = _ffn_call(hr, gf, wg, wu, wd, row(norm_final), ROW_TM, last, casts)
            if conv_next:
                conv_bf16[i + 1], ffn_bf16[i + 1] = cast[:2], cast[2:]
        elif kind == 1:
            win, wout = conv_bf16[i]
            hm, ztail, _ = _conv_ffn_call(hm, gm, win, conv_w[j], wout, jnp.zeros((CONV_TAIL, D_MODEL), F32),
                                          gf, wg, wu, wd, 1, N_META)
            hr, _, _ = _conv_ffn_call(hr, gm, win, conv_w[j], wout, ztail, gf, wg, wu, wd, batch, CONV_TM)
        else:
            wp, sc = pool_w[j].astype(BF16), row(pool_scale[j])
            hm, atail = _pool_ffn_call(hm, gm, wp, sc, jnp.zeros((POOL_HIST, D_MODEL), F32),
                                       gf, wg, wu, wd, 1, N_META, True)
            hr, _ = _pool_ffn_call(hr, gm, wp, sc, atail, gf, wg, wu, wd, batch, ROW_TM, False)
    return hr.reshape(batch, seq, D_MODEL)
```

```python
import functools
import math

import numpy as np
import jax
import jax.numpy as jnp
from jax import lax
from jax.experimental import pallas as pl
from jax.experimental.pallas import tpu as pltpu

F32 = jnp.float32
BF16 = jnp.bfloat16

D_MODEL = 1024
N_META = 16
RMS_EPS = 1e-6
N_HEADS = 16
N_KV = 4
HEAD_DIM = 64
GROUP = N_HEADS // N_KV
Q_DIM = N_HEADS * HEAD_DIM
KV_DIM = N_KV * HEAD_DIM
WINDOW = 128
N_BUCKETS = 32
MAX_DISTANCE = 128
POOL_WINDOWS = (2, 4, 8, 16)
POOL_GROUP_DIM = D_MODEL // len(POOL_WINDOWS)
POOL_HIST = 16
CONV_TAIL = 8
LANES = 128
FFN_LEAD_ROWS = 256
BF16_SUBLANES = 16
LOG2E = math.log2(math.e)

Q_BLK = 128
KV_PAIR = 4
UNIT_LANES = KV_PAIR * GROUP * Q_BLK
BAND_ROWS = WINDOW + Q_BLK
SINK_ROW = BAND_ROWS + N_META
KEY_ROWS = SINK_ROW + N_META
N_NEAR = WINDOW // Q_BLK
ONES_ROWS = 16

META_KEY_COLS = 128
CODE_SINK = N_BUCKETS
CODE_MASKED = N_BUCKETS + 1

VMEM_LIMIT = 56 * 1024 * 1024


def _rel_bucket_np(dist):
    max_exact = N_BUCKETS // 2
    d = np.maximum(dist, 0)
    df = np.maximum(d, 1).astype(np.float64)
    large = max_exact + (np.log(df / max_exact) / math.log(MAX_DISTANCE / max_exact)
                         * (N_BUCKETS - max_exact)).astype(np.int64)
    large = np.minimum(large, N_BUCKETS - 1)
    return np.where(d < max_exact, d, large).astype(np.int32)


def _meta_codes():
    i = np.arange(N_META)[:, None]
    m = np.arange(N_META)[None, :]
    code = np.full((N_META, META_KEY_COLS), CODE_MASKED, np.int32)
    code[:, :N_META] = np.where(i >= m, _rel_bucket_np(i - m), CODE_MASKED)
    code[:, N_META] = CODE_SINK
    return code[None]


def _band_codes():
    return np.stack([_rel_bucket_np(np.arange(WINDOW)), np.zeros(WINDOW, np.int32)])


def _near_meta_codes():
    m = np.arange(N_META)[:, None]
    i = np.arange(Q_BLK)[None, :]
    return np.stack([_rel_bucket_np(N_META + n * Q_BLK + i - m) for n in range(N_NEAR)])


def _const_spec(shape):
    zeros = (0,) * len(shape)
    return pl.BlockSpec(shape, lambda *_: zeros, pipeline_mode=pl.Buffered(1))


def _hosting(kernel_fn, n_in, n_out, n_cast):
    def wrapped(*refs):
        ins, rest = refs[:n_in], refs[n_in:]
        cast_in, rest = rest[:n_cast], rest[n_cast:]
        outs, rest = rest[:n_out], rest[n_out:]
        cast_out, scratch = rest[:n_cast], rest[n_cast:]
        for src, dst in zip(cast_in, cast_out):
            dst[...] = src[...].astype(BF16)
        kernel_fn(*ins, *outs, *scratch)
    return wrapped


def _cast_specs(casts, steps, flat_step):
    in_specs, out_specs, out_shapes = [], [], []
    for w, layer in casts:
        _, rows, cols = w.shape
        n_slices = max(n for n in range(1, steps + 1)
                       if steps % n == 0 and rows % n == 0 and (rows // n) % BF16_SUBLANES == 0)
        per = steps // n_slices
        blk = rows // n_slices
        in_specs.append(pl.BlockSpec((None, blk, cols), lambda *idx, l=layer, p=per: (l, flat_step(*idx) // p, 0)))
        out_specs.append(pl.BlockSpec((blk, cols), lambda *idx, p=per: (flat_step(*idx) // p, 0)))
        out_shapes.append(jax.ShapeDtypeStruct((rows, cols), BF16))
    return in_specs, out_specs, out_shapes


def _smem_spec():
    return pl.BlockSpec(memory_space=pltpu.SMEM)


def _rms(x, g):
    return x * lax.rsqrt(jnp.mean(x * x, axis=-1, keepdims=True) + RMS_EPS) * g


def _dot(a, b):
    return jnp.dot(a, b, preferred_element_type=F32)


def _dot_nt(a, b):
    return lax.dot_general(a, b, (((1,), (1,)), ((), ())), preferred_element_type=F32)


def _dot_tn(a, b):
    return lax.dot_general(a, b, (((0,), (0,)), ((), ())), preferred_element_type=F32)


def _select_by_code(code, tbl_ref, h, otherwise):
    val = otherwise
    for b in range(N_BUCKETS):
        val = jnp.where(code == b, tbl_ref[b, h], val)
    return val


def _meta_table_kernel(code_ref, tbl_ref, sink_ref, out_ref):
    n_var, rows, _ = code_ref.shape
    for v in range(n_var):
        code = code_ref[v]
        for h in range(N_HEADS):
            kh, g = divmod(h, GROUP)
            val = _select_by_code(code, tbl_ref, h, jnp.where(code == CODE_SINK, sink_ref[h], -jnp.inf))
            out_ref[v, kh, g * rows:(g + 1) * rows, :] = val


def _meta_table(codes, rel_table, sinks):
    n_var, rows, cols = codes.shape
    return pl.pallas_call(
        _meta_table_kernel,
        out_shape=jax.ShapeDtypeStruct((n_var, N_KV, GROUP * rows, cols), F32),
        in_specs=[pl.BlockSpec(memory_space=pltpu.VMEM), _smem_spec(), _smem_spec()],
        out_specs=pl.BlockSpec(memory_space=pltpu.VMEM),
        name="meta_bias_table",
    )(jnp.asarray(codes), rel_table, sinks)


def _real_table_kernel(band_code_ref, meta_code_ref, tbl_ref, sink_ref, out_ref):
    neg = -jnp.inf
    band_code = band_code_ref[0:1, :]
    t = lax.broadcasted_iota(jnp.int32, (BAND_ROWS, WINDOW), 0)
    i = lax.broadcasted_iota(jnp.int32, (BAND_ROWS, WINDOW), 1)
    in_window = (i < t) & (t <= i + WINDOW)
    row16 = lax.broadcasted_iota(jnp.int32, (N_META, Q_BLK), 0)
    for h in range(N_HEADS):
        pair, lane_blk = divmod(h, KV_PAIR * GROUP)
        lanes = slice(lane_blk * Q_BLK, (lane_blk + 1) * Q_BLK)
        by_dist = _select_by_code(band_code, tbl_ref, h, jnp.zeros((1, WINDOW), F32)) * LOG2E
        rolled = pltpu.roll(jnp.broadcast_to(by_dist, (WINDOW, WINDOW)), 0, 1, stride=1, stride_axis=0)
        rolled = jnp.concatenate([rolled, rolled[:BAND_ROWS - WINDOW]], axis=0)
        far = jnp.full((N_META, Q_BLK), tbl_ref[N_BUCKETS - 1, h], F32)
        sink_pad = jnp.where(row16 == 0, sink_ref[h] * LOG2E, neg)
        for v in range(1 + N_NEAR):
            if v == 0:
                valid, meta = in_window, far * LOG2E
            else:
                first_key_row = WINDOW - (v - 1) * Q_BLK
                valid = in_window & (t >= first_key_row)
                meta = _select_by_code(meta_code_ref[v - 1], tbl_ref, h, far) * LOG2E
            out_ref[v, pair, 0:BAND_ROWS, lanes] = jnp.where(valid, rolled, neg)[:, :Q_BLK]
            out_ref[v, pair, BAND_ROWS:SINK_ROW, lanes] = meta
            out_ref[v, pair, SINK_ROW:KEY_ROWS, lanes] = sink_pad


def _real_table(rel_table, sinks):
    vmem = pl.BlockSpec(memory_space=pltpu.VMEM)
    return pl.pallas_call(
        _real_table_kernel,
        out_shape=jax.ShapeDtypeStruct((1 + N_NEAR, N_KV // KV_PAIR, KEY_ROWS, UNIT_LANES), F32),
        in_specs=[vmem, vmem, _smem_spec(), _smem_spec()],
        out_specs=vmem,
        name="real_bias_table",
    )(jnp.asarray(_band_codes()), jnp.asarray(_near_meta_codes()), rel_table, sinks)


def _attn_kernel(h_ref, g_ref, wqv_ref, bqv_ref, wk_ref, bk_ref, km_ref, vmT_ref, tbl_ref, wo_ref, bo_ref,
                 out_ref, qT_scr, k_scr, vT_scr, oT_scr, *, tm):
    first_tile = pl.program_id(1) == 0

    @pl.when(first_tile)
    def _():
        k_scr[0:WINDOW, :] = jnp.zeros((WINDOW, KV_DIM), BF16)
        vT_scr[:, 0:WINDOW] = jnp.zeros((KV_DIM, WINDOW), BF16)

    @pl.when(jnp.logical_not(first_tile))
    def _():
        k_scr[0:WINDOW, :] = k_scr[tm:tm + WINDOW, :]
        vT_scr[:, 0:WINDOW] = vT_scr[:, tm:tm + WINDOW]

    x = h_ref[...]
    a = _rms(x, g_ref[...]).astype(BF16)
    bias = jnp.concatenate([bqv_ref[...]] * (tm // LANES), axis=1)
    qvT = _dot_nt(wqv_ref[...], a) + bias
    qT_scr[...] = (qvT[:Q_DIM] * (HEAD_DIM ** -0.5 * LOG2E)).astype(BF16)
    vT_scr[:, WINDOW:] = qvT[Q_DIM:].astype(BF16)
    k_scr[WINDOW:, :] = (_dot(a, wk_ref[...]) + bk_ref[...]).astype(BF16)

    kext = k_scr[...]
    vText = vT_scr[...]
    km = km_ref[...]
    vmT = vmT_ref[...]
    pair_dim = KV_PAIR * HEAD_DIM
    kv_lanes = GROUP * Q_BLK
    zk = jnp.zeros((KEY_ROWS - SINK_ROW, pair_dim), BF16)
    zq = jnp.zeros((HEAD_DIM, kv_lanes), BF16)
    ones = jnp.ones((ONES_ROWS, KEY_ROWS), BF16)

    def scores(j, pair):
        r0 = j * Q_BLK
        cols = slice(pair * pair_dim, (pair + 1) * pair_dim)
        blocks = []
        for c in range(KV_PAIR):
            kh = pair * KV_PAIR + c
            qsT = jnp.concatenate(
                [qT_scr[(kh * GROUP + g) * HEAD_DIM:(kh * GROUP + g + 1) * HEAD_DIM, r0:r0 + Q_BLK]
                 for g in range(GROUP)], axis=1)
            blocks.append(jnp.concatenate([zq] * c + [qsT] + [zq] * (KV_PAIR - 1 - c), axis=1))
        qsT = jnp.concatenate(blocks, axis=0)
        keys = jnp.concatenate([kext[r0:r0 + BAND_ROWS, cols], km[:, cols], zk], axis=0)
        return _dot(keys, qsT)

    def finish(j, pair, sT, sT_next):
        r0 = j * Q_BLK
        cols = slice(pair * pair_dim, (pair + 1) * pair_dim)
        var = jnp.where(first_tile, 1 + j, 0) if j < N_NEAR else 0
        sT = sT + tbl_ref[var, pair]
        m = jnp.max(sT, axis=0, keepdims=True)
        pT = jnp.exp2(sT - m).astype(BF16)
        valsT = jnp.concatenate([vText[cols, r0:r0 + BAND_ROWS], vmT[cols, :]], axis=1)
        den = ones
        if sT_next is not None:
            bits = pltpu.bitcast(sT_next[KEY_ROWS - ONES_ROWS:, :LANES], jnp.uint32)
            zero = pltpu.bitcast((bits >> 16) >> 16, F32).astype(BF16)
            den = jnp.concatenate([ones[:, :LANES] + zero, ones[:, LANES:]], axis=1)
        for c in range(KV_PAIR):
            kh = pair * KV_PAIR + c
            lanes = slice(c * kv_lanes, (c + 1) * kv_lanes)
            oT = _dot(jnp.concatenate([valsT[c * HEAD_DIM:(c + 1) * HEAD_DIM], den], axis=0), pT[:, lanes])
            o = oT[:HEAD_DIM] / oT[HEAD_DIM:HEAD_DIM + 1]
            for g in range(GROUP):
                h0 = (kh * GROUP + g) * HEAD_DIM
                oT_scr[h0:h0 + HEAD_DIM, r0:r0 + Q_BLK] = o[:, g * Q_BLK:(g + 1) * Q_BLK].astype(BF16)

    units = [(j, pair) for j in range(tm // Q_BLK) for pair in range(N_KV // KV_PAIR)]
    pending = [scores(*u) for u in units[:ATTN_LOOKAHEAD]]
    for idx, unit in enumerate(units):
        ahead = idx + ATTN_LOOKAHEAD
        if ahead < len(units):
            pending.append(scores(*units[ahead]))
        sT = pending.pop(0)
        finish(*unit, sT, pending[-1] if pending else None)
    out_ref[...] = x + _dot_tn(oT_scr[...], wo_ref[...]) + bo_ref[...]


def _attn_call(h, g, wqvT, bqvT, wk, bk, km, vmT, tbl, wo, bo, batch, tm, casts=()):
    n = h.shape[0]
    nt = n // batch // tm
    row = pl.BlockSpec((tm, D_MODEL), lambda b, i: (b * nt + i, 0))
    cast_in, cast_out, cast_shapes = _cast_specs(casts, batch * nt, lambda b, i: b * nt + i)
    in_specs = [row, _const_spec((1, D_MODEL)), _const_spec(wqvT.shape), _const_spec(bqvT.shape),
                _const_spec(wk.shape), _const_spec((1, KV_DIM)), _const_spec(km.shape), _const_spec(vmT.shape),
                _const_spec(tbl.shape), _const_spec(wo.shape), _const_spec((1, D_MODEL))]
    outs = pl.pallas_call(
        _hosting(functools.partial(_attn_kernel, tm=tm), len(in_specs), 1, len(casts)),
        grid=(batch, nt),
        out_shape=[jax.ShapeDtypeStruct((n, D_MODEL), F32)] + cast_shapes,
        in_specs=in_specs + cast_in,
        out_specs=[row] + cast_out,
        scratch_shapes=[pltpu.VMEM((Q_DIM, tm), BF16), pltpu.VMEM((WINDOW + tm, KV_DIM), BF16),
                        pltpu.VMEM((KV_DIM, WINDOW + tm), BF16), pltpu.VMEM((Q_DIM, tm), BF16)],
        compiler_params=pltpu.CompilerParams(dimension_semantics=("arbitrary", "arbitrary"),
                                             vmem_limit_bytes=VMEM_LIMIT),
        name="attn",
    )(h, g, wqvT, bqvT, wk, bk, km, vmT, tbl, wo, bo, *[v for v, _ in casts])
    return outs[0], outs[1:]


def _attn_meta_kernel(h_ref, g_ref, w_ref, b_ref, tbl_ref, wo_ref, bo_ref,
                      out_ref, km_ref, vmT_ref, wqvT_ref, bqvT_ref, wk_ref, wo16_ref):
    w = w_ref[...]
    b = b_ref[...]
    wq, wk, wv = w[:, :Q_DIM], w[:, Q_DIM:Q_DIM + KV_DIM], w[:, Q_DIM + KV_DIM:]
    wqvT = jnp.concatenate([wq.T, wv.T], axis=0).astype(BF16)
    b_qv = jnp.concatenate([b[:, :Q_DIM], b[:, Q_DIM + KV_DIM:]], axis=1)
    bqvT = jnp.broadcast_to(b_qv, (LANES, Q_DIM + KV_DIM)).T
    wk16 = wk.astype(BF16)
    wo16 = wo_ref[...].astype(BF16)
    wqvT_ref[...] = wqvT
    bqvT_ref[...] = bqvT
    wk_ref[...] = wk16
    wo16_ref[...] = wo16

    x = h_ref[...]
    a = _rms(x, g_ref[...]).astype(BF16)
    a2 = jnp.concatenate([a, jnp.zeros_like(a)], axis=0)
    lane = lax.broadcasted_iota(jnp.int32, (KV_DIM, 2 * N_META), 1)
    vT = _dot_nt(wqvT[Q_DIM:], a2) + jnp.where(lane < N_META, bqvT[Q_DIM:, :2 * N_META], 0.0)
    vmT_ref[...] = vT.astype(BF16)
    q = ((_dot_nt(a, wqvT[:Q_DIM]) + b[:, :Q_DIM]) * (HEAD_DIM ** -0.5)).astype(BF16)
    k = (_dot(a, wk16) + b[:, Q_DIM:Q_DIM + KV_DIM]).astype(BF16)
    v = (_dot_nt(a, wqvT[Q_DIM:]) + b[:, Q_DIM + KV_DIM:]).astype(BF16)
    km_ref[...] = k
    pad = META_KEY_COLS - N_META
    vcat = jnp.concatenate([v, jnp.zeros((pad, KV_DIM), BF16)], axis=0)
    pieces = []
    for kh in range(N_KV):
        cols = slice(kh * HEAD_DIM, (kh + 1) * HEAD_DIM)
        qs = jnp.concatenate([q[:, (kh * GROUP + g) * HEAD_DIM:(kh * GROUP + g + 1) * HEAD_DIM]
                              for g in range(GROUP)], axis=0)
        kcat = jnp.concatenate([k[:, cols], jnp.zeros((pad, HEAD_DIM), BF16)], axis=0)
        s = _dot_nt(qs, kcat) + tbl_ref[0, kh]
        p = jnp.exp(s - jnp.max(s, axis=-1, keepdims=True))
        l = jnp.sum(p, axis=-1, keepdims=True)
        o = _dot(p.astype(BF16), vcat)[:, cols] / l
        pieces += [o[g * N_META:(g + 1) * N_META] for g in range(GROUP)]
    o_all = jnp.concatenate(pieces, axis=-1).astype(BF16)
    out_ref[...] = x + _dot(o_all, wo16) + bo_ref[...]


def _attn_meta_call(h, g, w_stack, layer, b, tbl, wo_stack, bo):
    whole = lambda shape: pl.BlockSpec(shape, lambda i: (0,) * len(shape))
    of_layer = lambda stack: pl.BlockSpec((None,) + stack.shape[1:], lambda i: (layer, 0, 0))
    out_shapes = ((N_META, D_MODEL, F32), (N_META, KV_DIM, BF16), (KV_DIM, 2 * N_META, BF16),
                  (Q_DIM + KV_DIM, D_MODEL, BF16), (Q_DIM + KV_DIM, LANES, F32), (D_MODEL, KV_DIM, BF16),
                  (Q_DIM, D_MODEL, BF16))
    return pl.pallas_call(
        _attn_meta_kernel,
        grid=(1,),
        out_shape=tuple(jax.ShapeDtypeStruct((r, c), dt) for r, c, dt in out_shapes),
        in_specs=[whole(h.shape), whole(g.shape), of_layer(w_stack), whole(b.shape), whole(tbl.shape),
                  of_layer(wo_stack), whole(bo.shape)],
        out_specs=tuple(whole((r, c)) for r, c, _ in out_shapes),
        compiler_params=pltpu.CompilerParams(dimension_semantics=("arbitrary",), vmem_limit_bytes=VMEM_LIMIT),
        name="attn_meta",
    )(h, g, w_stack, b, tbl, wo_stack, bo)


def _ffn_math(x, g_ref, wg_ref, wu_ref, wd_ref):
    rows = x.shape[0]
    lead = FFN_LEAD_ROWS if rows > 2 * FFN_LEAD_ROWS else rows
    chunks = [x[:lead]] + ([x[lead:]] if lead < rows else [])
    hmids = []
    for xc in chunks:
        a = _rms(xc, g_ref[...]).astype(BF16)
        gate = _dot(a, wg_ref[...])
        up = _dot(a, wu_ref[...])
        hmids.append((gate * (1.0 / (1.0 + jnp.exp(-gate))) * up).astype(BF16))
    outs = [xc + _dot(hm, wd_ref[...]) for xc, hm in zip(chunks, hmids)]
    return outs[0] if len(outs) == 1 else jnp.concatenate(outs, axis=0)


def _ffn_kernel(h_ref, g_ref, wg_ref, wu_ref, wd_ref, gf_ref, out_ref, *, final_norm):
    y = _ffn_math(h_ref[...], g_ref, wg_ref, wu_ref, wd_ref)
    out_ref[...] = _rms(y, gf_ref[...]) if final_norm else y


def _ffn_specs(wg, wu, wd):
    return [_const_spec((1, D_MODEL)), _const_spec(wg.shape), _const_spec(wu.shape), _const_spec(wd.shape)]


def _ffn_call(h, g, wg, wu, wd, gf, tm, final_norm, casts=()):
    n = h.shape[0]
    row = pl.BlockSpec((tm, D_MODEL), lambda i: (i, 0))
    cast_in, cast_out, cast_shapes = _cast_specs(casts, n // tm, lambda i: i)
    in_specs = [row] + _ffn_specs(wg, wu, wd) + [_const_spec((1, D_MODEL))]
    outs = pl.pallas_call(
        _hosting(functools.partial(_ffn_kernel, final_norm=final_norm), len(in_specs), 1, len(casts)),
        grid=(n // tm,),
        out_shape=[jax.ShapeDtypeStruct((n, D_MODEL), F32)] + cast_shapes,
        in_specs=in_specs + cast_in,
        out_specs=[row] + cast_out,
        compiler_params=pltpu.CompilerParams(dimension_semantics=("parallel",), vmem_limit_bytes=VMEM_LIMIT),
        name="ffn",
    )(h, g, wg, wu, wd, gf, *[v for v, _ in casts])
    return outs[0], outs[1:]


def _conv_ffn_kernel(h_ref, g_ref, win_ref, cw_ref, wout_ref, zinit_ref, gf_ref, wg_ref, wu_ref, wd_ref,
                     out_ref, ztail_ref, zprev_scr, *, tm):
    @pl.when(pl.program_id(1) == 0)
    def _():
        zprev_scr[...] = zinit_ref[...]

    x = h_ref[...]
    a = _rms(x, g_ref[...]).astype(BF16)
    bcu = _dot(a, win_ref[...])
    gate_b = bcu[:, :D_MODEL]
    z = bcu[:, D_MODEL:2 * D_MODEL] * bcu[:, 2 * D_MODEL:]
    zp = zprev_scr[...]
    row = lax.broadcasted_iota(jnp.int32, (tm, D_MODEL), 0)
    last, last2 = zp[CONV_TAIL - 1:CONV_TAIL], zp[CONV_TAIL - 2:CONV_TAIL - 1]
    z1 = jnp.where(row == 0, last, pltpu.roll(z, 1, 0))
    z2 = jnp.where(row == 0, last2, jnp.where(row == 1, last, pltpu.roll(z, 2, 0)))
    cw = cw_ref[...]
    conv = cw[0:1] * z2 + cw[1:2] * z1 + cw[2:3] * z
    y = x + _dot((gate_b * conv).astype(BF16), wout_ref[...])
    out_ref[...] = _ffn_math(y, gf_ref, wg_ref, wu_ref, wd_ref)
    tail = z[tm - CONV_TAIL:]
    zprev_scr[...] = tail
    ztail_ref[...] = tail


def _conv_ffn_call(h, g, win, cw, wout, zinit, gf, wg, wu, wd, batch, tm, casts=()):
    n = h.shape[0]
    nt = n // batch // tm
    row = pl.BlockSpec((tm, D_MODEL), lambda b, i: (b * nt + i, 0))
    tail_shape = (CONV_TAIL, D_MODEL)
    cast_in, cast_out, cast_shapes = _cast_specs(casts, batch * nt, lambda b, i: b * nt + i)
    in_specs = [row, _const_spec((1, D_MODEL)), _const_spec(win.shape), _const_spec(cw.shape),
                _const_spec(wout.shape), _const_spec(tail_shape)] + _ffn_specs(wg, wu, wd)
    outs = pl.pallas_call(
        _hosting(functools.partial(_conv_ffn_kernel, tm=tm), len(in_specs), 2, len(casts)),
        grid=(batch, nt),
        out_shape=[jax.ShapeDtypeStruct((n, D_MODEL), F32), jax.ShapeDtypeStruct(tail_shape, F32)] + cast_shapes,
        in_specs=in_specs + cast_in,
        out_specs=[row, pl.BlockSpec(tail_shape, lambda b, i: (0, 0))] + cast_out,
        scratch_shapes=[pltpu.VMEM(tail_shape, F32)],
        compiler_params=pltpu.CompilerParams(dimension_semantics=("arbitrary", "arbitrary"),
                                             vmem_limit_bytes=VMEM_LIMIT),
        name="conv_ffn",
    )(h, g, win, cw, wout, zinit, gf, wg, wu, wd, *[v for v, _ in casts])
    return outs[0], outs[1], outs[2:]


def _pool_ffn_kernel(h_ref, g_ref, wp_ref, sc_ref, hinit_ref, gf_ref, wg_ref, wu_ref, wd_ref,
                     out_ref, atail_ref, hist_scr, *, tm, from_start):
    @pl.when(pl.program_id(1) == 0)
    def _():
        hist_scr[...] = hinit_ref[...]

    x = h_ref[...]
    a = _rms(x, g_ref[...])
    ext = jnp.concatenate([hist_scr[...], a], axis=0)
    outs = []
    for gi, win in enumerate(POOL_WINDOWS):
        cols = slice(gi * POOL_GROUP_DIM, (gi + 1) * POOL_GROUP_DIM)
        s = ext[:, cols]
        k = 1
        while k < win:
            s = s + pltpu.roll(s, k, 0)
            k *= 2
        s = s[POOL_HIST:]
        if from_start:
            pos = lax.broadcasted_iota(jnp.int32, (tm, POOL_GROUP_DIM), 0)
            mean = s / jnp.minimum(win, pos + 1).astype(F32)
        else:
            mean = s * (1.0 / win)
        mix = (mean - a[:, cols]).astype(BF16)
        outs.append(_dot(mix, wp_ref[gi]))
    y = x + jnp.concatenate(outs, axis=-1) * sc_ref[...]
    out_ref[...] = _ffn_math(y, gf_ref, wg_ref, wu_ref, wd_ref)
    tail = a[tm - POOL_HIST:]
    hist_scr[...] = tail
    atail_ref[...] = tail


def _pool_ffn_call(h, g, wp, sc, hinit, gf, wg, wu, wd, batch, tm, from_start):
    n = h.shape[0]
    nt = n // batch // tm
    row = pl.BlockSpec((tm, D_MODEL), lambda b, i: (b * nt + i, 0))
    hist_shape = (POOL_HIST, D_MODEL)
    return pl.pallas_call(
        functools.partial(_pool_ffn_kernel, tm=tm, from_start=from_start),
        grid=(batch, nt),
        out_shape=(jax.ShapeDtypeStruct((n, D_MODEL), F32), jax.ShapeDtypeStruct(hist_shape, F32)),
        in_specs=[row, _const_spec((1, D_MODEL)), _const_spec(wp.shape), _const_spec((1, D_MODEL)),
                  _const_spec(hist_shape)] + _ffn_specs(wg, wu, wd),
        out_specs=(row, pl.BlockSpec(hist_shape, lambda b, i: (0, 0))),
        scratch_shapes=[pltpu.VMEM(hist_shape, F32)],
        compiler_params=pltpu.CompilerParams(dimension_semantics=("arbitrary", "arbitrary"),
                                             vmem_limit_bytes=VMEM_LIMIT),
        name="pool_ffn",
    )(h, g, wp, sc, hinit, gf, wg, wu, wd)


ATTN_TM = 512
ATTN_LOOKAHEAD = 1
ROW_TM = 1024
CONV_TM = 1024


def kernel(x, meta_tokens, rel_bias_table, norm_mix, norm_ffn, norm_final, attn_w_qkv, attn_b_qkv, attn_w_o,
           attn_b_o, attn_sinks, conv_w_in, conv_w, conv_w_out, pool_w, pool_scale, ffn_w_gate, ffn_w_up,
           ffn_w_down):
    batch, seq, _ = x.shape
    depth = norm_mix.shape[0]
    n_mixers = 3
    hr = x.reshape(batch * seq, D_MODEL)
    hm = meta_tokens.astype(F32)
    row = lambda v: v.reshape(1, -1)
    meta_codes = _meta_codes()

    ffn_bf16, conv_bf16 = {}, {}

    def ffn_casts(layers):
        return [(w, l) for l in layers for w in (ffn_w_gate, ffn_w_up, ffn_w_down)]

    for i in range(depth):
        kind, j = i % n_mixers, i // n_mixers
        last = i == depth - 1
        gm, gf = row(norm_mix[i]), row(norm_ffn[i])
        if kind != 0:
            wg, wu, wd = ffn_bf16[i]
        if kind == 0:
            b, bo = attn_b_qkv[j], row(attn_b_o[j])
            bk = row(b[Q_DIM:Q_DIM + KV_DIM])
            tbl_meta = _meta_table(meta_codes, rel_bias_table, attn_sinks[j])
            tbl_real = _real_table(rel_bias_table, attn_sinks[j])
            hm, km, vmT, wqvT, bqvT, wk, wo = _attn_meta_call(hm, gm, attn_w_qkv, j, row(b), tbl_meta,
                                                              attn_w_o, bo)
            hosted = [l for l in (i, i + 2) if l < depth and (l == i or l % n_mixers == 2)]
            hr, cast = _attn_call(hr, gm, wqvT, bqvT, wk, bk, km, vmT, tbl_real, wo, bo, batch, ATTN_TM,
                                  ffn_casts(hosted))
            for n, l in enumerate(hosted):
                ffn_bf16[l] = cast[3 * n:3 * n + 3]
            wg, wu, wd = ffn_bf16[i]
            if not last:
                hm, _ = _ffn_call(hm, gf, wg, wu, wd, row(norm_final), N_META, False)
            conv_next = i + 1 < depth and (i + 1) % n_mixers == 1
            casts = ([(conv_w_in, (i + 1) // n_mixers), (conv_w_out, (i + 1) // n_mixers)] + ffn_casts([i + 1])
                     if conv_next else [])
            hr, cast = _ffn_call(hr, gf, wg, wu, wd, row(norm_final), ROW_TM, last, casts)
            if conv_next:
                conv_bf16[i + 1], ffn_bf16[i + 1] = cast[:2], cast[2:]
        elif kind == 1:
            win, wout = conv_bf16[i]
            hm, ztail, _ = _conv_ffn_call(hm, gm, win, conv_w[j], wout, jnp.zeros((CONV_TAIL, D_MODEL), F32),
                                          gf, wg, wu, wd, 1, N_META)
            hr, _, _ = _conv_ffn_call(hr, gm, win, conv_w[j], wout, ztail, gf, wg, wu, wd, batch, CONV_TM)
        else:
            wp, sc = pool_w[j].astype(BF16), row(pool_scale[j])
            hm, atail = _pool_ffn_call(hm, gm, wp, sc, jnp.zeros((POOL_HIST, D_MODEL), F32),
                                       gf, wg, wu, wd, 1, N_META, True)
            hr, _ = _pool_ffn_call(hr, gm, wp, sc, atail, gf, wg, wu, wd, batch, ROW_TM, False)
    return hr.reshape(batch, seq, D_MODEL)
```

```python
import functools
import math

import numpy as np
import jax
import jax.numpy as jnp
from jax import lax
from jax.experimental import pallas as pl
from jax.experimental.pallas import tpu as pltpu

F32 = jnp.float32
BF16 = jnp.bfloat16

D_MODEL = 1024
N_META = 16
RMS_EPS = 1e-6
N_HEADS = 16
N_KV = 4
HEAD_DIM = 64
GROUP = N_HEADS // N_KV
Q_DIM = N_HEADS * HEAD_DIM
KV_DIM = N_KV * HEAD_DIM
WINDOW = 128
N_BUCKETS = 32
MAX_DISTANCE = 128
POOL_WINDOWS = (2, 4, 8, 16)
POOL_GROUP_DIM = D_MODEL // len(POOL_WINDOWS)
POOL_HIST = 16
CONV_TAIL = 8
LANES = 128
FFN_LEAD_ROWS = 256
BF16_SUBLANES = 16
LOG2E = math.log2(math.e)

Q_BLK = 128
KV_PAIR = 4
UNIT_LANES = KV_PAIR * GROUP * Q_BLK
BAND_ROWS = WINDOW + Q_BLK
SINK_ROW = BAND_ROWS + N_META
KEY_ROWS = SINK_ROW
TABLE_ROWS = SINK_ROW + 8
N_NEAR = WINDOW // Q_BLK
ONES_ROWS = 16

META_KEY_COLS = 128
CODE_SINK = N_BUCKETS
CODE_MASKED = N_BUCKETS + 1

VMEM_LIMIT = 56 * 1024 * 1024


def _rel_bucket_np(dist):
    max_exact = N_BUCKETS // 2
    d = np.maximum(dist, 0)
    df = np.maximum(d, 1).astype(np.float64)
    large = max_exact + (np.log(df / max_exact) / math.log(MAX_DISTANCE / max_exact)
                         * (N_BUCKETS - max_exact)).astype(np.int64)
    large = np.minimum(large, N_BUCKETS - 1)
    return np.where(d < max_exact, d, large).astype(np.int32)


def _meta_codes():
    i = np.arange(N_META)[:, None]
    m = np.arange(N_META)[None, :]
    code = np.full((N_META, META_KEY_COLS), CODE_MASKED, np.int32)
    code[:, :N_META] = np.where(i >= m, _rel_bucket_np(i - m), CODE_MASKED)
    code[:, N_META] = CODE_SINK
    return code[None]


def _band_codes():
    return np.stack([_rel_bucket_np(np.arange(WINDOW)), np.zeros(WINDOW, np.int32)])


def _near_meta_codes():
    m = np.arange(N_META)[:, None]
    i = np.arange(Q_BLK)[None, :]
    return np.stack([_rel_bucket_np(N_META + n * Q_BLK + i - m) for n in range(N_NEAR)])


def _const_spec(shape):
    zeros = (0,) * len(shape)
    return pl.BlockSpec(shape, lambda *_: zeros, pipeline_mode=pl.Buffered(1))


def _hosting(kernel_fn, n_in, n_out, n_cast):
    def wrapped(*refs):
        ins, rest = refs[:n_in], refs[n_in:]
        cast_in, rest = rest[:n_cast], rest[n_cast:]
        outs, rest = rest[:n_out], rest[n_out:]
        cast_out, scratch = rest[:n_cast], rest[n_cast:]
        for src, dst in zip(cast_in, cast_out):
            dst[...] = src[...].astype(BF16)
        kernel_fn(*ins, *outs, *scratch)
    return wrapped


def _cast_specs(casts, steps, flat_step):
    in_specs, out_specs, out_shapes = [], [], []
    for w, layer in casts:
        _, rows, cols = w.shape
        n_slices = max(n for n in range(1, steps + 1)
                       if steps % n == 0 and rows % n == 0 and (rows // n) % BF16_SUBLANES == 0)
        per = steps // n_slices
        blk = rows // n_slices
        in_specs.append(pl.BlockSpec((None, blk, cols), lambda *idx, l=layer, p=per: (l, flat_step(*idx) // p, 0)))
        out_specs.append(pl.BlockSpec((blk, cols), lambda *idx, p=per: (flat_step(*idx) // p, 0)))
        out_shapes.append(jax.ShapeDtypeStruct((rows, cols), BF16))
    return in_specs, out_specs, out_shapes


def _smem_spec():
    return pl.BlockSpec(memory_space=pltpu.SMEM)


def _rms(x, g):
    return x * lax.rsqrt(jnp.mean(x * x, axis=-1, keepdims=True) + RMS_EPS) * g


def _dot(a, b):
    return jnp.dot(a, b, preferred_element_type=F32)


def _dot_nt(a, b):
    return lax.dot_general(a, b, (((1,), (1,)), ((), ())), preferred_element_type=F32)


def _dot_tn(a, b):
    return lax.dot_general(a, b, (((0,), (0,)), ((), ())), preferred_element_type=F32)


def _select_by_code(code, tbl_ref, h, otherwise):
    val = otherwise
    for b in range(N_BUCKETS):
        val = jnp.where(code == b, tbl_ref[b, h], val)
    return val


def _meta_table_kernel(code_ref, tbl_ref, sink_ref, out_ref):
    n_var, rows, _ = code_ref.shape
    for v in range(n_var):
        code = code_ref[v]
        for h in range(N_HEADS):
            kh, g = divmod(h, GROUP)
            val = _select_by_code(code, tbl_ref, h, jnp.where(code == CODE_SINK, sink_ref[h], -jnp.inf))
            out_ref[v, kh, g * rows:(g + 1) * rows, :] = val


def _meta_table(codes, rel_table, sinks):
    n_var, rows, cols = codes.shape
    return pl.pallas_call(
        _meta_table_kernel,
        out_shape=jax.ShapeDtypeStruct((n_var, N_KV, GROUP * rows, cols), F32),
        in_specs=[pl.BlockSpec(memory_space=pltpu.VMEM), _smem_spec(), _smem_spec()],
        out_specs=pl.BlockSpec(memory_space=pltpu.VMEM),
        name="meta_bias_table",
    )(jnp.asarray(codes), rel_table, sinks)


def _real_table_kernel(band_code_ref, meta_code_ref, tbl_ref, sink_ref, out_ref):
    neg = -jnp.inf
    band_code = band_code_ref[0:1, :]
    t = lax.broadcasted_iota(jnp.int32, (BAND_ROWS, WINDOW), 0)
    i = lax.broadcasted_iota(jnp.int32, (BAND_ROWS, WINDOW), 1)
    in_window = (i < t) & (t <= i + WINDOW)
    for h in range(N_HEADS):
        pair, lane_blk = divmod(h, KV_PAIR * GROUP)
        lanes = slice(lane_blk * Q_BLK, (lane_blk + 1) * Q_BLK)
        by_dist = _select_by_code(band_code, tbl_ref, h, jnp.zeros((1, WINDOW), F32)) * LOG2E
        rolled = pltpu.roll(jnp.broadcast_to(by_dist, (WINDOW, WINDOW)), 0, 1, stride=1, stride_axis=0)
        rolled = jnp.concatenate([rolled, rolled[:BAND_ROWS - WINDOW]], axis=0)
        far = jnp.full((N_META, Q_BLK), tbl_ref[N_BUCKETS - 1, h], F32)
        sink_rows = jnp.full((TABLE_ROWS - SINK_ROW, Q_BLK), sink_ref[h], F32) * LOG2E
        for v in range(1 + N_NEAR):
            if v == 0:
                valid, meta = in_window, far * LOG2E
            else:
                first_key_row = WINDOW - (v - 1) * Q_BLK
                valid = in_window & (t >= first_key_row)
                meta = _select_by_code(meta_code_ref[v - 1], tbl_ref, h, far) * LOG2E
            out_ref[v, pair, 0:BAND_ROWS, lanes] = jnp.where(valid, rolled, neg)[:, :Q_BLK]
            out_ref[v, pair, BAND_ROWS:SINK_ROW, lanes] = meta
            out_ref[v, pair, SINK_ROW:TABLE_ROWS, lanes] = sink_rows


def _real_table(rel_table, sinks):
    vmem = pl.BlockSpec(memory_space=pltpu.VMEM)
    return pl.pallas_call(
        _real_table_kernel,
        out_shape=jax.ShapeDtypeStruct((1 + N_NEAR, N_KV // KV_PAIR, TABLE_ROWS, UNIT_LANES), F32),
        in_specs=[vmem, vmem, _smem_spec(), _smem_spec()],
        out_specs=vmem,
        name="real_bias_table",
    )(jnp.asarray(_band_codes()), jnp.asarray(_near_meta_codes()), rel_table, sinks)


def _attn_kernel(h_ref, g_ref, wqv_ref, bqv_ref, wk_ref, bk_ref, km_ref, vmT_ref, tbl_ref, wo_ref, bo_ref,
                 out_ref, qT_scr, k_scr, vT_scr, oT_scr, *, tm):
    first_tile = pl.program_id(1) == 0

    @pl.when(first_tile)
    def _():
        k_scr[0:WINDOW, :] = jnp.zeros((WINDOW, KV_DIM), BF16)
        vT_scr[:, 0:WINDOW] = jnp.zeros((KV_DIM, WINDOW), BF16)

    @pl.when(jnp.logical_not(first_tile))
    def _():
        k_scr[0:WINDOW, :] = k_scr[tm:tm + WINDOW, :]
        vT_scr[:, 0:WINDOW] = vT_scr[:, tm:tm + WINDOW]

    x = h_ref[...]
    a = _rms(x, g_ref[...]).astype(BF16)
    bias = jnp.concatenate([bqv_ref[...]] * (tm // LANES), axis=1)
    qvT = _dot_nt(wqv_ref[...], a) + bias
    qT_scr[...] = (qvT[:Q_DIM] * (HEAD_DIM ** -0.5 * LOG2E)).astype(BF16)
    vT_scr[:, WINDOW:] = qvT[Q_DIM:].astype(BF16)
    k_scr[WINDOW:, :] = (_dot(a, wk_ref[...]) + bk_ref[...]).astype(BF16)

    kext = k_scr[...]
    vText = vT_scr[...]
    km = km_ref[...]
    vmT = vmT_ref[...]
    pair_dim = KV_PAIR * HEAD_DIM
    kv_lanes = GROUP * Q_BLK
    zq = jnp.zeros((HEAD_DIM, kv_lanes), BF16)
    ones = jnp.ones((ONES_ROWS, KEY_ROWS), BF16)

    def scores(j, pair):
        r0 = j * Q_BLK
        cols = slice(pair * pair_dim, (pair + 1) * pair_dim)
        blocks = []
        for c in range(KV_PAIR):
            kh = pair * KV_PAIR + c
            qsT = jnp.concatenate(
                [qT_scr[(kh * GROUP + g) * HEAD_DIM:(kh * GROUP + g + 1) * HEAD_DIM, r0:r0 + Q_BLK]
                 for g in range(GROUP)], axis=1)
            blocks.append(jnp.concatenate([zq] * c + [qsT] + [zq] * (KV_PAIR - 1 - c), axis=1))
        qsT = jnp.concatenate(blocks, axis=0)
        keys = jnp.concatenate([kext[r0:r0 + BAND_ROWS, cols], km[:, cols]], axis=0)
        return _dot(keys, qsT)

    def finish(j, pair, sT, sT_next):
        r0 = j * Q_BLK
        cols = slice(pair * pair_dim, (pair + 1) * pair_dim)
        var = jnp.where(first_tile, 1 + j, 0) if j < N_NEAR else 0
        sT = sT + tbl_ref[var, pair, 0:KEY_ROWS, :]
        sink = tbl_ref[var, pair, SINK_ROW:SINK_ROW + 1, :]
        m = jnp.maximum(jnp.max(sT, axis=0, keepdims=True), sink)
        pT = jnp.exp2(sT - m).astype(BF16)
        p_sink = jnp.exp2(sink - m)
        valsT = jnp.concatenate([vText[cols, r0:r0 + BAND_ROWS], vmT[cols, :N_META]], axis=1)
        den = ones
        if sT_next is not None:
            bits = pltpu.bitcast(sT_next[KEY_ROWS - ONES_ROWS:, :LANES], jnp.uint32)
            zero = pltpu.bitcast((bits >> 16) >> 16, F32).astype(BF16)
            den = jnp.concatenate([ones[:, :LANES] + zero, ones[:, LANES:]], axis=1)
        for c in range(KV_PAIR):
            kh = pair * KV_PAIR + c
            lanes = slice(c * kv_lanes, (c + 1) * kv_lanes)
            oT = _dot(jnp.concatenate([valsT[c * HEAD_DIM:(c + 1) * HEAD_DIM], den], axis=0), pT[:, lanes])
            o = oT[:HEAD_DIM] / (oT[HEAD_DIM:HEAD_DIM + 1] + p_sink[:, lanes])
            for g in range(GROUP):
                h0 = (kh * GROUP + g) * HEAD_DIM
                oT_scr[h0:h0 + HEAD_DIM, r0:r0 + Q_BLK] = o[:, g * Q_BLK:(g + 1) * Q_BLK].astype(BF16)

    units = [(j, pair) for j in range(tm // Q_BLK) for pair in range(N_KV // KV_PAIR)]
    pending = [scores(*u) for u in units[:ATTN_LOOKAHEAD]]
    for idx, unit in enumerate(units):
        ahead = idx + ATTN_LOOKAHEAD
        if ahead < len(units):
            pending.append(scores(*units[ahead]))
        sT = pending.pop(0)
        finish(*unit, sT, pending[-1] if pending else None)
    out_ref[...] = x + _dot_tn(oT_scr[...], wo_ref[...]) + bo_ref[...]


def _attn_call(h, g, wqvT, bqvT, wk, bk, km, vmT, tbl, wo, bo, batch, tm, casts=()):
    n = h.shape[0]
    nt = n // batch // tm
    row = pl.BlockSpec((tm, D_MODEL), lambda b, i: (b * nt + i, 0))
    cast_in, cast_out, cast_shapes = _cast_specs(casts, batch * nt, lambda b, i: b * nt + i)
    in_specs = [row, _const_spec((1, D_MODEL)), _const_spec(wqvT.shape), _const_spec(bqvT.shape),
                _const_spec(wk.shape), _const_spec((1, KV_DIM)), _const_spec(km.shape), _const_spec(vmT.shape),
                _const_spec(tbl.shape), _const_spec(wo.shape), _const_spec((1, D_MODEL))]
    outs = pl.pallas_call(
        _hosting(functools.partial(_attn_kernel, tm=tm), len(in_specs), 1, len(casts)),
        grid=(batch, nt),
        out_shape=[jax.ShapeDtypeStruct((n, D_MODEL), F32)] + cast_shapes,
        in_specs=in_specs + cast_in,
        out_specs=[row] + cast_out,
        scratch_shapes=[pltpu.VMEM((Q_DIM, tm), BF16), pltpu.VMEM((WINDOW + tm, KV_DIM), BF16),
                        pltpu.VMEM((KV_DIM, WINDOW + tm), BF16), pltpu.VMEM((Q_DIM, tm), BF16)],
        compiler_params=pltpu.CompilerParams(dimension_semantics=("arbitrary", "arbitrary"),
                                             vmem_limit_bytes=VMEM_LIMIT),
        name="attn",
    )(h, g, wqvT, bqvT, wk, bk, km, vmT, tbl, wo, bo, *[v for v, _ in casts])
    return outs[0], outs[1:]


def _attn_meta_kernel(h_ref, g_ref, w_ref, b_ref, tbl_ref, wo_ref, bo_ref,
                      out_ref, km_ref, vmT_ref, wqvT_ref, bqvT_ref, wk_ref, wo16_ref):
    w = w_ref[...]
    b = b_ref[...]
    wq, wk, wv = w[:, :Q_DIM], w[:, Q_DIM:Q_DIM + KV_DIM], w[:, Q_DIM + KV_DIM:]
    wqvT = jnp.concatenate([wq.T, wv.T], axis=0).astype(BF16)
    b_qv = jnp.concatenate([b[:, :Q_DIM], b[:, Q_DIM + KV_DIM:]], axis=1)
    bqvT = jnp.broadcast_to(b_qv, (LANES, Q_DIM + KV_DIM)).T
    wk16 = wk.astype(BF16)
    wo16 = wo_ref[...].astype(BF16)
    wqvT_ref[...] = wqvT
    bqvT_ref[...] = bqvT
    wk_ref[...] = wk16
    wo16_ref[...] = wo16

    x = h_ref[...]
    a = _rms(x, g_ref[...]).astype(BF16)
    a2 = jnp.concatenate([a, jnp.zeros_like(a)], axis=0)
    lane = lax.broadcasted_iota(jnp.int32, (KV_DIM, 2 * N_META), 1)
    vT = _dot_nt(wqvT[Q_DIM:], a2) + jnp.where(lane < N_META, bqvT[Q_DIM:, :2 * N_META], 0.0)
    vmT_ref[...] = vT.astype(BF16)
    q = ((_dot_nt(a, wqvT[:Q_DIM]) + b[:, :Q_DIM]) * (HEAD_DIM ** -0.5)).astype(BF16)
    k = (_dot(a, wk16) + b[:, Q_DIM:Q_DIM + KV_DIM]).astype(BF16)
    v = (_dot_nt(a, wqvT[Q_DIM:]) + b[:, Q_DIM + KV_DIM:]).astype(BF16)
    km_ref[...] = k
    pad = META_KEY_COLS - N_META
    vcat = jnp.concatenate([v, jnp.zeros((pad, KV_DIM), BF16)], axis=0)
    pieces = []
    for kh in range(N_KV):
        cols = slice(kh * HEAD_DIM, (kh + 1) * HEAD_DIM)
        qs = jnp.concatenate([q[:, (kh * GROUP + g) * HEAD_DIM:(kh * GROUP + g + 1) * HEAD_DIM]
                              for g in range(GROUP)], axis=0)
        kcat = jnp.concatenate([k[:, cols], jnp.zeros((pad, HEAD_DIM), BF16)], axis=0)
        s = _dot_nt(qs, kcat) + tbl_ref[0, kh]
        p = jnp.exp(s - jnp.max(s, axis=-1, keepdims=True))
        l = jnp.sum(p, axis=-1, keepdims=True)
        o = _dot(p.astype(BF16), vcat)[:, cols] / l
        pieces += [o[g * N_META:(g + 1) * N_META] for g in range(GROUP)]
    o_all = jnp.concatenate(pieces, axis=-1).astype(BF16)
    out_ref[...] = x + _dot(o_all, wo16) + bo_ref[...]


def _attn_meta_call(h, g, w_stack, layer, b, tbl, wo_stack, bo):
    whole = lambda shape: pl.BlockSpec(shape, lambda i: (0,) * len(shape))
    of_layer = lambda stack: pl.BlockSpec((None,) + stack.shape[1:], lambda i: (layer, 0, 0))
    out_shapes = ((N_META, D_MODEL, F32), (N_META, KV_DIM, BF16), (KV_DIM, 2 * N_META, BF16),
                  (Q_DIM + KV_DIM, D_MODEL, BF16), (Q_DIM + KV_DIM, LANES, F32), (D_MODEL, KV_DIM, BF16),
                  (Q_DIM, D_MODEL, BF16))
    return pl.pallas_call(
        _attn_meta_kernel,
        grid=(1,),
        out_shape=tuple(jax.ShapeDtypeStruct((r, c), dt) for r, c, dt in out_shapes),
        in_specs=[whole(h.shape), whole(g.shape), of_layer(w_stack), whole(b.shape), whole(tbl.shape),
                  of_layer(wo_stack), whole(bo.shape)],
        out_specs=tuple(whole((r, c)) for r, c, _ in out_shapes),
        compiler_params=pltpu.CompilerParams(dimension_semantics=("arbitrary",), vmem_limit_bytes=VMEM_LIMIT),
        name="attn_meta",
    )(h, g, w_stack, b, tbl, wo_stack, bo)


def _ffn_math(x, g_ref, wg_ref, wu_ref, wd_ref):
    rows = x.shape[0]
    lead = FFN_LEAD_ROWS if rows > 2 * FFN_LEAD_ROWS else rows
    chunks = [x[:lead]] + ([x[lead:]] if lead < rows else [])
    hmids = []
    for xc in chunks:
        a = _rms(xc, g_ref[...]).astype(BF16)
        gate = _dot(a, wg_ref[...])
        up = _dot(a, wu_ref[...])
        hmids.append((gate * (1.0 / (1.0 + jnp.exp(-gate))) * up).astype(BF16))
    outs = [xc + _dot(hm, wd_ref[...]) for xc, hm in zip(chunks, hmids)]
    return outs[0] if len(outs) == 1 else jnp.concatenate(outs, axis=0)


def _ffn_kernel(h_ref, g_ref, wg_ref, wu_ref, wd_ref, gf_ref, out_ref, *, final_norm):
    y = _ffn_math(h_ref[...], g_ref, wg_ref, wu_ref, wd_ref)
    out_ref[...] = _rms(y, gf_ref[...]) if final_norm else y


def _ffn_specs(wg, wu, wd):
    return [_const_spec((1, D_MODEL)), _const_spec(wg.shape), _const_spec(wu.shape), _const_spec(wd.shape)]


def _ffn_call(h, g, wg, wu, wd, gf, tm, final_norm, casts=()):
    n = h.shape[0]
    row = pl.BlockSpec((tm, D_MODEL), lambda i: (i, 0))
    cast_in, cast_out, cast_shapes = _cast_specs(casts, n // tm, lambda i: i)
    in_specs = [row] + _ffn_specs(wg, wu, wd) + [_const_spec((1, D_MODEL))]
    outs = pl.pallas_call(
        _hosting(functools.partial(_ffn_kernel, final_norm=final_norm), len(in_specs), 1, len(casts)),
        grid=(n // tm,),
        out_shape=[jax.ShapeDtypeStruct((n, D_MODEL), F32)] + cast_shapes,
        in_specs=in_specs + cast_in,
        out_specs=[row] + cast_out,
        compiler_params=pltpu.CompilerParams(dimension_semantics=("parallel",), vmem_limit_bytes=VMEM_LIMIT),
        name="ffn",
    )(h, g, wg, wu, wd, gf, *[v for v, _ in casts])
    return outs[0], outs[1:]


def _conv_ffn_kernel(h_ref, g_ref, win_ref, cw_ref, wout_ref, zinit_ref, gf_ref, wg_ref, wu_ref, wd_ref,
                     out_ref, ztail_ref, zprev_scr, *, tm):
    @pl.when(pl.program_id(1) == 0)
    def _():
        zprev_scr[...] = zinit_ref[...]

    x = h_ref[...]
    a = _rms(x, g_ref[...]).astype(BF16)
    bcu = _dot(a, win_ref[...])
    gate_b = bcu[:, :D_MODEL]
    z = bcu[:, D_MODEL:2 * D_MODEL] * bcu[:, 2 * D_MODEL:]
    zp = zprev_scr[...]
    row = lax.broadcasted_iota(jnp.int32, (tm, D_MODEL), 0)
    last, last2 = zp[CONV_TAIL - 1:CONV_TAIL], zp[CONV_TAIL - 2:CONV_TAIL - 1]
    z1 = jnp.where(row == 0, last, pltpu.roll(z, 1, 0))
    z2 = jnp.where(row == 0, last2, jnp.where(row == 1, last, pltpu.roll(z, 2, 0)))
    cw = cw_ref[...]
    conv = cw[0:1] * z2 + cw[1:2] * z1 + cw[2:3] * z
    y = x + _dot((gate_b * conv).astype(BF16), wout_ref[...])
    out_ref[...] = _ffn_math(y, gf_ref, wg_ref, wu_ref, wd_ref)
    tail = z[tm - CONV_TAIL:]
    zprev_scr[...] = tail
    ztail_ref[...] = tail


def _conv_ffn_call(h, g, win, cw, wout, zinit, gf, wg, wu, wd, batch, tm, casts=()):
    n = h.shape[0]
    nt = n // batch // tm
    row = pl.BlockSpec((tm, D_MODEL), lambda b, i: (b * nt + i, 0))
    tail_shape = (CONV_TAIL, D_MODEL)
    cast_in, cast_out, cast_shapes = _cast_specs(casts, batch * nt, lambda b, i: b * nt + i)
    in_specs = [row, _const_spec((1, D_MODEL)), _const_spec(win.shape), _const_spec(cw.shape),
                _const_spec(wout.shape), _const_spec(tail_shape)] + _ffn_specs(wg, wu, wd)
    outs = pl.pallas_call(
        _hosting(functools.partial(_conv_ffn_kernel, tm=tm), len(in_specs), 2, len(casts)),
        grid=(batch, nt),
        out_shape=[jax.ShapeDtypeStruct((n, D_MODEL), F32), jax.ShapeDtypeStruct(tail_shape, F32)] + cast_shapes,
        in_specs=in_specs + cast_in,
        out_specs=[row, pl.BlockSpec(tail_shape, lambda b, i: (0, 0))] + cast_out,
        scratch_shapes=[pltpu.VMEM(tail_shape, F32)],
        compiler_params=pltpu.CompilerParams(dimension_semantics=("arbitrary", "arbitrary"),
                                             vmem_limit_bytes=VMEM_LIMIT),
        name="conv_ffn",
    )(h, g, win, cw, wout, zinit, gf, wg, wu, wd, *[v for v, _ in casts])
    return outs[0], outs[1], outs[2:]


def _pool_ffn_kernel(h_ref, g_ref, wp_ref, sc_ref, hinit_ref, gf_ref, wg_ref, wu_ref, wd_ref,
                     out_ref, atail_ref, hist_scr, *, tm, from_start):
    @pl.when(pl.program_id(1) == 0)
    def _():
        hist_scr[...] = hinit_ref[...]

    x = h_ref[...]
    a = _rms(x, g_ref[...])
    ext = jnp.concatenate([hist_scr[...], a], axis=0)
    outs = []
    for gi, win in enumerate(POOL_WINDOWS):
        cols = slice(gi * POOL_GROUP_DIM, (gi + 1) * POOL_GROUP_DIM)
        s = ext[:, cols]
        k = 1
        while k < win:
            s = s + pltpu.roll(s, k, 0)
            k *= 2
        s = s[POOL_HIST:]
        if from_start:
            pos = lax.broadcasted_iota(jnp.int32, (tm, POOL_GROUP_DIM), 0)
            mean = s / jnp.minimum(win, pos + 1).astype(F32)
        else:
            mean = s * (1.0 / win)
        mix = (mean - a[:, cols]).astype(BF16)
        outs.append(_dot(mix, wp_ref[gi]))
    y = x + jnp.concatenate(outs, axis=-1) * sc_ref[...]
    out_ref[...] = _ffn_math(y, gf_ref, wg_ref, wu_ref, wd_ref)
    tail = a[tm - POOL_HIST:]
    hist_scr[...] = tail
    atail_ref[...] = tail


def _pool_ffn_call(h, g, wp, sc, hinit, gf, wg, wu, wd, batch, tm, from_start):
    n = h.shape[0]
    nt = n // batch // tm
    row = pl.BlockSpec((tm, D_MODEL), lambda b, i: (b * nt + i, 0))
    hist_shape = (POOL_HIST, D_MODEL)
    return pl.pallas_call(
        functools.partial(_pool_ffn_kernel, tm=tm, from_start=from_start),
        grid=(batch, nt),
        out_shape=(jax.ShapeDtypeStruct((n, D_MODEL), F32), jax.ShapeDtypeStruct(hist_shape, F32)),
        in_specs=[row, _const_spec((1, D_MODEL)), _const_spec(wp.shape), _const_spec((1, D_MODEL)),
                  _const_spec(hist_shape)] + _ffn_specs(wg, wu, wd),
        out_specs=(row, pl.BlockSpec(hist_shape, lambda b, i: (0, 0))),
        scratch_shapes=[pltpu.VMEM(hist_shape, F32)],
        compiler_params=pltpu.CompilerParams(dimension_semantics=("arbitrary", "arbitrary"),
                                             vmem_limit_bytes=VMEM_LIMIT),
        name="pool_ffn",
    )(h, g, wp, sc, hinit, gf, wg, wu, wd)


ATTN_TM = 512
ATTN_LOOKAHEAD = 1
ROW_TM = 1024
CONV_TM = 1024


def kernel(x, meta_tokens, rel_bias_table, norm_mix, norm_ffn, norm_final, attn_w_qkv, attn_b_qkv, attn_w_o,
           attn_b_o, attn_sinks, conv_w_in, conv_w, conv_w_out, pool_w, pool_scale, ffn_w_gate, ffn_w_up,
           ffn_w_down):
    batch, seq, _ = x.shape
    depth = norm_mix.shape[0]
    n_mixers = 3
    hr = x.reshape(batch * seq, D_MODEL)
    hm = meta_tokens.astype(F32)
    row = lambda v: v.reshape(1, -1)
    meta_codes = _meta_codes()

    ffn_bf16, conv_bf16 = {}, {}

    def ffn_casts(layers):
        return [(w, l) for l in layers for w in (ffn_w_gate, ffn_w_up, ffn_w_down)]

    for i in range(depth):
        kind, j = i % n_mixers, i // n_mixers
        last = i == depth - 1
        gm, gf = row(norm_mix[i]), row(norm_ffn[i])
        if kind != 0:
            wg, wu, wd = ffn_bf16[i]
        if kind == 0:
            b, bo = attn_b_qkv[j], row(attn_b_o[j])
            bk = row(b[Q_DIM:Q_DIM + KV_DIM])
            tbl_meta = _meta_table(meta_codes, rel_bias_table, attn_sinks[j])
            tbl_real = _real_table(rel_bias_table, attn_sinks[j])
            hm, km, vmT, wqvT, bqvT, wk, wo = _attn_meta_call(hm, gm, attn_w_qkv, j, row(b), tbl_meta,
                                                              attn_w_o, bo)
            hosted = [l for l in (i, i + 2) if l < depth and (l == i or l % n_mixers == 2)]
            hr, cast = _attn_call(hr, gm, wqvT, bqvT, wk, bk, km, vmT, tbl_real, wo, bo, batch, ATTN_TM,
                                  ffn_casts(hosted))
            for n, l in enumerate(hosted):
                ffn_bf16[l] = cast[3 * n:3 * n + 3]
            wg, wu, wd = ffn_bf16[i]
            if not last:
                hm, _ = _ffn_call(hm, gf, wg, wu, wd, row(norm_final), N_META, False)
            conv_next = i + 1 < depth and (i + 1) % n_mixers == 1
            casts = ([(conv_w_in, (i + 1) // n_mixers), (conv_w_out, (i + 1) // n_mixers)] + ffn_casts([i + 1])
                     if conv_next else [])
            hr, cast = _ffn_call(hr, gf, wg, wu, wd, row(norm_final), ROW_TM, last, casts)
            if conv_next:
                conv_bf16[i + 1], ffn_bf16[i + 1] = cast[:2], cast[2:]
        elif kind == 1:
            win, wout = conv_bf16[i]
            hm, ztail, _ = _conv_ffn_call(hm, gm, win, conv_w[j], wout, jnp.zeros((CONV_TAIL, D_MODEL), F32),
                                          gf, wg, wu, wd, 1, N_META)
            hr, _, _ = _conv_ffn_call(hr, gm, win, conv_w[j], wout, ztail, gf, wg, wu, wd, batch, CONV_TM)
        else:
            wp, sc = pool_w[j].astype(BF16), row(pool_scale[j])
            hm, atail = _pool_ffn_call(hm, gm, wp, sc, jnp.zeros((POOL_HIST, D_MODEL), F32),
                                       gf, wg, wu, wd, 1, N_META, True)
            hr, _ = _pool_ffn_call(hr, gm, wp, sc, atail, gf, wg, wu, wd, batch, ROW_TM, False)
    return hr.reshape(batch, seq, D_MODEL)
```

```python
import functools
import math

import numpy as np
import jax
import jax.numpy as jnp
from jax import lax
from jax.experimental import pallas as pl
from jax.experimental.pallas import tpu as pltpu

F32 = jnp.float32
BF16 = jnp.bfloat16

D_MODEL = 1024
N_META = 16
RMS_EPS = 1e-6
N_HEADS = 16
N_KV = 4
HEAD_DIM = 64
GROUP = N_HEADS // N_KV
Q_DIM = N_HEADS * HEAD_DIM
KV_DIM = N_KV * HEAD_DIM
WINDOW = 128
N_BUCKETS = 32
MAX_DISTANCE = 128
POOL_WINDOWS = (2, 4, 8, 16)
POOL_GROUP_DIM = D_MODEL // len(POOL_WINDOWS)
POOL_HIST = 16
CONV_TAIL = 8
LANES = 128
FFN_LEAD_ROWS = 256
BF16_SUBLANES = 16
LOG2E = math.log2(math.e)

Q_BLK = 128
KV_PAIR = 4
UNIT_LANES = KV_PAIR * GROUP * Q_BLK
BAND_ROWS = WINDOW + Q_BLK
SINK_ROW = BAND_ROWS + N_META
KEY_ROWS = SINK_ROW + N_META
N_NEAR = WINDOW // Q_BLK
ONES_ROWS = 16

META_KEY_COLS = 128
CODE_SINK = N_BUCKETS
CODE_MASKED = N_BUCKETS + 1

VMEM_LIMIT = 56 * 1024 * 1024


def _rel_bucket_np(dist):
    max_exact = N_BUCKETS // 2
    d = np.maximum(dist, 0)
    df = np.maximum(d, 1).astype(np.float64)
    large = max_exact + (np.log(df / max_exact) / math.log(MAX_DISTANCE / max_exact)
                         * (N_BUCKETS - max_exact)).astype(np.int64)
    large = np.minimum(large, N_BUCKETS - 1)
    return np.where(d < max_exact, d, large).astype(np.int32)


def _meta_codes():
    i = np.arange(N_META)[:, None]
    m = np.arange(N_META)[None, :]
    code = np.full((N_META, META_KEY_COLS), CODE_MASKED, np.int32)
    code[:, :N_META] = np.where(i >= m, _rel_bucket_np(i - m), CODE_MASKED)
    code[:, N_META] = CODE_SINK
    return code[None]


def _band_codes():
    return np.stack([_rel_bucket_np(np.arange(WINDOW)), np.zeros(WINDOW, np.int32)])


def _near_meta_codes():
    m = np.arange(N_META)[:, None]
    i = np.arange(Q_BLK)[None, :]
    return np.stack([_rel_bucket_np(N_META + n * Q_BLK + i - m) for n in range(N_NEAR)])


def _const_spec(shape):
    zeros = (0,) * len(shape)
    return pl.BlockSpec(shape, lambda *_: zeros, pipeline_mode=pl.Buffered(1))


def _hosting(kernel_fn, n_in, n_out, n_cast):
    def wrapped(*refs):
        ins, rest = refs[:n_in], refs[n_in:]
        cast_in, rest = rest[:n_cast], rest[n_cast:]
        outs, rest = rest[:n_out], rest[n_out:]
        cast_out, scratch = rest[:n_cast], rest[n_cast:]
        kernel_fn(*ins, *outs, *scratch)
        for src, dst in zip(cast_in, cast_out):
            dst[...] = src[...].astype(BF16)
    return wrapped


def _cast_specs(casts, steps, flat_step):
    in_specs, out_specs, out_shapes = [], [], []
    for w, layer in casts:
        _, rows, cols = w.shape
        n_slices = max(n for n in range(1, steps + 1)
                       if steps % n == 0 and rows % n == 0 and (rows // n) % BF16_SUBLANES == 0)
        per = steps // n_slices
        blk = rows // n_slices
        in_specs.append(pl.BlockSpec((None, blk, cols), lambda *idx, l=layer, p=per: (l, flat_step(*idx) // p, 0)))
        out_specs.append(pl.BlockSpec((blk, cols), lambda *idx, p=per: (flat_step(*idx) // p, 0)))
        out_shapes.append(jax.ShapeDtypeStruct((rows, cols), BF16))
    return in_specs, out_specs, out_shapes


def _smem_spec():
    return pl.BlockSpec(memory_space=pltpu.SMEM)


def _rms(x, g):
    return x * lax.rsqrt(jnp.mean(x * x, axis=-1, keepdims=True) + RMS_EPS) * g


def _dot(a, b):
    return jnp.dot(a, b, preferred_element_type=F32)


def _dot_nt(a, b):
    return lax.dot_general(a, b, (((1,), (1,)), ((), ())), preferred_element_type=F32)


def _dot_tn(a, b):
    return lax.dot_general(a, b, (((0,), (0,)), ((), ())), preferred_element_type=F32)


def _select_by_code(code, tbl_ref, h, otherwise):
    val = otherwise
    for b in range(N_BUCKETS):
        val = jnp.where(code == b, tbl_ref[b, h], val)
    return val


def _meta_table_kernel(code_ref, tbl_ref, sink_ref, out_ref):
    n_var, rows, _ = code_ref.shape
    for v in range(n_var):
        code = code_ref[v]
        for h in range(N_HEADS):
            kh, g = divmod(h, GROUP)
            val = _select_by_code(code, tbl_ref, h, jnp.where(code == CODE_SINK, sink_ref[h], -jnp.inf))
            out_ref[v, kh, g * rows:(g + 1) * rows, :] = val


def _meta_table(codes, rel_table, sinks):
    n_var, rows, cols = codes.shape
    return pl.pallas_call(
        _meta_table_kernel,
        out_shape=jax.ShapeDtypeStruct((n_var, N_KV, GROUP * rows, cols), F32),
        in_specs=[pl.BlockSpec(memory_space=pltpu.VMEM), _smem_spec(), _smem_spec()],
        out_specs=pl.BlockSpec(memory_space=pltpu.VMEM),
        name="meta_bias_table",
    )(jnp.asarray(codes), rel_table, sinks)


def _real_table_kernel(band_code_ref, meta_code_ref, tbl_ref, sink_ref, out_ref):
    neg = -jnp.inf
    band_code = band_code_ref[0:1, :]
    t = lax.broadcasted_iota(jnp.int32, (BAND_ROWS, WINDOW), 0)
    i = lax.broadcasted_iota(jnp.int32, (BAND_ROWS, WINDOW), 1)
    in_window = (i < t) & (t <= i + WINDOW)
    row16 = lax.broadcasted_iota(jnp.int32, (N_META, Q_BLK), 0)
    for h in range(N_HEADS):
        pair, lane_blk = divmod(h, KV_PAIR * GROUP)
        lanes = slice(lane_blk * Q_BLK, (lane_blk + 1) * Q_BLK)
        by_dist = _select_by_code(band_code, tbl_ref, h, jnp.zeros((1, WINDOW), F32)) * LOG2E
        rolled = pltpu.roll(jnp.broadcast_to(by_dist, (WINDOW, WINDOW)), 0, 1, stride=1, stride_axis=0)
        rolled = jnp.concatenate([rolled, rolled[:BAND_ROWS - WINDOW]], axis=0)
        far = jnp.full((N_META, Q_BLK), tbl_ref[N_BUCKETS - 1, h], F32)
        sink_pad = jnp.where(row16 == 0, sink_ref[h] * LOG2E, neg)
        for v in range(1 + N_NEAR):
            if v == 0:
                valid, meta = in_window, far * LOG2E
            else:
                first_key_row = WINDOW - (v - 1) * Q_BLK
                valid = in_window & (t >= first_key_row)
                meta = _select_by_code(meta_code_ref[v - 1], tbl_ref, h, far) * LOG2E
            out_ref[v, pair, 0:BAND_ROWS, lanes] = jnp.where(valid, rolled, neg)[:, :Q_BLK]
            out_ref[v, pair, BAND_ROWS:SINK_ROW, lanes] = meta
            out_ref[v, pair, SINK_ROW:KEY_ROWS, lanes] = sink_pad


def _real_table(rel_table, sinks):
    vmem = pl.BlockSpec(memory_space=pltpu.VMEM)
    return pl.pallas_call(
        _real_table_kernel,
        out_shape=jax.ShapeDtypeStruct((1 + N_NEAR, N_KV // KV_PAIR, KEY_ROWS, UNIT_LANES), F32),
        in_specs=[vmem, vmem, _smem_spec(), _smem_spec()],
        out_specs=vmem,
        name="real_bias_table",
    )(jnp.asarray(_band_codes()), jnp.asarray(_near_meta_codes()), rel_table, sinks)


def _attn_kernel(h_ref, g_ref, wqv_ref, bqv_ref, wk_ref, bk_ref, km_ref, vmT_ref, tbl_ref, wo_ref, bo_ref,
                 out_ref, qT_scr, k_scr, vT_scr, oT_scr, *, tm):
    first_tile = pl.program_id(1) == 0

    @pl.when(first_tile)
    def _():
        k_scr[0:WINDOW, :] = jnp.zeros((WINDOW, KV_DIM), BF16)
        vT_scr[:, 0:WINDOW] = jnp.zeros((KV_DIM, WINDOW), BF16)

    @pl.when(jnp.logical_not(first_tile))
    def _():
        k_scr[0:WINDOW, :] = k_scr[tm:tm + WINDOW, :]
        vT_scr[:, 0:WINDOW] = vT_scr[:, tm:tm + WINDOW]

    x = h_ref[...]
    a = _rms(x, g_ref[...]).astype(BF16)
    bias = jnp.concatenate([bqv_ref[...]] * (tm // LANES), axis=1)
    qvT = _dot_nt(wqv_ref[...], a) + bias
    qT_scr[...] = (qvT[:Q_DIM] * (HEAD_DIM ** -0.5 * LOG2E)).astype(BF16)
    vT_scr[:, WINDOW:] = qvT[Q_DIM:].astype(BF16)
    k_scr[WINDOW:, :] = (_dot(a, wk_ref[...]) + bk_ref[...]).astype(BF16)

    kext = k_scr[...]
    vText = vT_scr[...]
    km = km_ref[...]
    vmT = vmT_ref[...]
    pair_dim = KV_PAIR * HEAD_DIM
    kv_lanes = GROUP * Q_BLK
    zk = jnp.zeros((KEY_ROWS - SINK_ROW, pair_dim), BF16)
    zq = jnp.zeros((HEAD_DIM, kv_lanes), BF16)
    ones = jnp.ones((ONES_ROWS, KEY_ROWS), BF16)

    def scores(j, pair):
        r0 = j * Q_BLK
        cols = slice(pair * pair_dim, (pair + 1) * pair_dim)
        blocks = []
        for c in range(KV_PAIR):
            kh = pair * KV_PAIR + c
            qsT = jnp.concatenate(
                [qT_scr[(kh * GROUP + g) * HEAD_DIM:(kh * GROUP + g + 1) * HEAD_DIM, r0:r0 + Q_BLK]
                 for g in range(GROUP)], axis=1)
            blocks.append(jnp.concatenate([zq] * c + [qsT] + [zq] * (KV_PAIR - 1 - c), axis=1))
        qsT = jnp.concatenate(blocks, axis=0)
        keys = jnp.concatenate([kext[r0:r0 + BAND_ROWS, cols], km[:, cols], zk], axis=0)
        return _dot(keys, qsT)

    def finish(j, pair, sT, sT_next):
        r0 = j * Q_BLK
        cols = slice(pair * pair_dim, (pair + 1) * pair_dim)
        var = jnp.where(first_tile, 1 + j, 0) if j < N_NEAR else 0
        sT = sT + tbl_ref[var, pair]
        m = jnp.max(sT, axis=0, keepdims=True)
        pT = jnp.exp2(sT - m).astype(BF16)
        valsT = jnp.concatenate([vText[cols, r0:r0 + BAND_ROWS], vmT[cols, :]], axis=1)
        den = ones
        if sT_next is not None:
            bits = pltpu.bitcast(sT_next[KEY_ROWS - ONES_ROWS:, :LANES], jnp.uint32)
            zero = pltpu.bitcast((bits >> 16) >> 16, F32).astype(BF16)
            den = jnp.concatenate([ones[:, :LANES] + zero, ones[:, LANES:]], axis=1)
        for c in range(KV_PAIR):
            kh = pair * KV_PAIR + c
            lanes = slice(c * kv_lanes, (c + 1) * kv_lanes)
            oT = _dot(jnp.concatenate([valsT[c * HEAD_DIM:(c + 1) * HEAD_DIM], den], axis=0), pT[:, lanes])
            o = oT[:HEAD_DIM] / oT[HEAD_DIM:HEAD_DIM + 1]
            for g in range(GROUP):
                h0 = (kh * GROUP + g) * HEAD_DIM
                oT_scr[h0:h0 + HEAD_DIM, r0:r0 + Q_BLK] = o[:, g * Q_BLK:(g + 1) * Q_BLK].astype(BF16)

    units = [(j, pair) for j in range(tm // Q_BLK) for pair in range(N_KV // KV_PAIR)]
    pending = [scores(*u) for u in units[:ATTN_LOOKAHEAD]]
    for idx, unit in enumerate(units):
        ahead = idx + ATTN_LOOKAHEAD
        if ahead < len(units):
            pending.append(scores(*units[ahead]))
        sT = pending.pop(0)
        finish(*unit, sT, pending[-1] if pending else None)
    out_ref[...] = x + _dot_tn(oT_scr[...], wo_ref[...]) + bo_ref[...]


def _attn_call(h, g, wqvT, bqvT, wk, bk, km, vmT, tbl, wo, bo, batch, tm, casts=()):
    n = h.shape[0]
    nt = n // batch // tm
    row = pl.BlockSpec((tm, D_MODEL), lambda b, i: (b * nt + i, 0))
    cast_in, cast_out, cast_shapes = _cast_specs(casts, batch * nt, lambda b, i: b * nt + i)
    in_specs = [row, _const_spec((1, D_MODEL)), _const_spec(wqvT.shape), _const_spec(bqvT.shape),
                _const_spec(wk.shape), _const_spec((1, KV_DIM)), _const_spec(km.shape), _const_spec(vmT.shape),
                _const_spec(tbl.shape), _const_spec(wo.shape), _const_spec((1, D_MODEL))]
    outs = pl.pallas_call(
        _hosting(functools.partial(_attn_kernel, tm=tm), len(in_specs), 1, len(casts)),
        grid=(batch, nt),
        out_shape=[jax.ShapeDtypeStruct((n, D_MODEL), F32)] + cast_shapes,
        in_specs=in_specs + cast_in,
        out_specs=[row] + cast_out,
        scratch_shapes=[pltpu.VMEM((Q_DIM, tm), BF16), pltpu.VMEM((WINDOW + tm, KV_DIM), BF16),
                        pltpu.VMEM((KV_DIM, WINDOW + tm), BF16), pltpu.VMEM((Q_DIM, tm), BF16)],
        compiler_params=pltpu.CompilerParams(dimension_semantics=("arbitrary", "arbitrary"),
                                             vmem_limit_bytes=VMEM_LIMIT),
        name="attn",
    )(h, g, wqvT, bqvT, wk, bk, km, vmT, tbl, wo, bo, *[v for v, _ in casts])
    return outs[0], outs[1:]


def _attn_meta_kernel(h_ref, g_ref, w_ref, b_ref, tbl_ref, wo_ref, bo_ref,
                      out_ref, km_ref, vmT_ref, wqvT_ref, bqvT_ref, wk_ref, wo16_ref):
    w = w_ref[...]
    b = b_ref[...]
    wq, wk, wv = w[:, :Q_DIM], w[:, Q_DIM:Q_DIM + KV_DIM], w[:, Q_DIM + KV_DIM:]
    wqvT = jnp.concatenate([wq.T, wv.T], axis=0).astype(BF16)
    b_qv = jnp.concatenate([b[:, :Q_DIM], b[:, Q_DIM + KV_DIM:]], axis=1)
    bqvT = jnp.broadcast_to(b_qv, (LANES, Q_DIM + KV_DIM)).T
    wk16 = wk.astype(BF16)
    wo16 = wo_ref[...].astype(BF16)
    wqvT_ref[...] = wqvT
    bqvT_ref[...] = bqvT
    wk_ref[...] = wk16
    wo16_ref[...] = wo16

    x = h_ref[...]
    a = _rms(x, g_ref[...]).astype(BF16)
    a2 = jnp.concatenate([a, jnp.zeros_like(a)], axis=0)
    lane = lax.broadcasted_iota(jnp.int32, (KV_DIM, 2 * N_META), 1)
    vT = _dot_nt(wqvT[Q_DIM:], a2) + jnp.where(lane < N_META, bqvT[Q_DIM:, :2 * N_META], 0.0)
    vmT_ref[...] = vT.astype(BF16)
    q = ((_dot_nt(a, wqvT[:Q_DIM]) + b[:, :Q_DIM]) * (HEAD_DIM ** -0.5)).astype(BF16)
    k = (_dot(a, wk16) + b[:, Q_DIM:Q_DIM + KV_DIM]).astype(BF16)
    v = (_dot_nt(a, wqvT[Q_DIM:]) + b[:, Q_DIM + KV_DIM:]).astype(BF16)
    km_ref[...] = k
    pad = META_KEY_COLS - N_META
    vcat = jnp.concatenate([v, jnp.zeros((pad, KV_DIM), BF16)], axis=0)
    pieces = []
    for kh in range(N_KV):
        cols = slice(kh * HEAD_DIM, (kh + 1) * HEAD_DIM)
        qs = jnp.concatenate([q[:, (kh * GROUP + g) * HEAD_DIM:(kh * GROUP + g + 1) * HEAD_DIM]
                              for g in range(GROUP)], axis=0)
        kcat = jnp.concatenate([k[:, cols], jnp.zeros((pad, HEAD_DIM), BF16)], axis=0)
        s = _dot_nt(qs, kcat) + tbl_ref[0, kh]
        p = jnp.exp(s - jnp.max(s, axis=-1, keepdims=True))
        l = jnp.sum(p, axis=-1, keepdims=True)
        o = _dot(p.astype(BF16), vcat)[:, cols] / l
        pieces += [o[g * N_META:(g + 1) * N_META] for g in range(GROUP)]
    o_all = jnp.concatenate(pieces, axis=-1).astype(BF16)
    out_ref[...] = x + _dot(o_all, wo16) + bo_ref[...]


def _attn_meta_call(h, g, w_stack, layer, b, tbl, wo_stack, bo):
    whole = lambda shape: pl.BlockSpec(shape, lambda i: (0,) * len(shape))
    of_layer = lambda stack: pl.BlockSpec((None,) + stack.shape[1:], lambda i: (layer, 0, 0))
    out_shapes = ((N_META, D_MODEL, F32), (N_META, KV_DIM, BF16), (KV_DIM, 2 * N_META, BF16),
                  (Q_DIM + KV_DIM, D_MODEL, BF16), (Q_DIM + KV_DIM, LANES, F32), (D_MODEL, KV_DIM, BF16),
                  (Q_DIM, D_MODEL, BF16))
    return pl.pallas_call(
        _attn_meta_kernel,
        grid=(1,),
        out_shape=tuple(jax.ShapeDtypeStruct((r, c), dt) for r, c, dt in out_shapes),
        in_specs=[whole(h.shape), whole(g.shape), of_layer(w_stack), whole(b.shape), whole(tbl.shape),
                  of_layer(wo_stack), whole(bo.shape)],
        out_specs=tuple(whole((r, c)) for r, c, _ in out_shapes),
        compiler_params=pltpu.CompilerParams(dimension_semantics=("arbitrary",), vmem_limit_bytes=VMEM_LIMIT),
        name="attn_meta",
    )(h, g, w_stack, b, tbl, wo_stack, bo)


def _ffn_math(x, g_ref, wg_ref, wu_ref, wd_ref):
    rows = x.shape[0]
    lead = FFN_LEAD_ROWS if rows > 2 * FFN_LEAD_ROWS else rows
    chunks = [x[:lead]] + ([x[lead:]] if lead < rows else [])
    hmids = []
    for xc in chunks:
        a = _rms(xc, g_ref[...]).astype(BF16)
        gate = _dot(a, wg_ref[...])
        up = _dot(a, wu_ref[...])
        hmids.append((gate * (1.0 / (1.0 + jnp.exp(-gate))) * up).astype(BF16))
    outs = [xc + _dot(hm, wd_ref[...]) for xc, hm in zip(chunks, hmids)]
    return outs[0] if len(outs) == 1 else jnp.concatenate(outs, axis=0)


def _ffn_kernel(h_ref, g_ref, wg_ref, wu_ref, wd_ref, gf_ref, out_ref, *, final_norm):
    y = _ffn_math(h_ref[...], g_ref, wg_ref, wu_ref, wd_ref)
    out_ref[...] = _rms(y, gf_ref[...]) if final_norm else y


def _ffn_specs(wg, wu, wd):
    return [_const_spec((1, D_MODEL)), _const_spec(wg.shape), _const_spec(wu.shape), _const_spec(wd.shape)]


def _ffn_call(h, g, wg, wu, wd, gf, tm, final_norm, casts=()):
    n = h.shape[0]
    row = pl.BlockSpec((tm, D_MODEL), lambda i: (i, 0))
    cast_in, cast_out, cast_shapes = _cast_specs(casts, n // tm, lambda i: i)
    in_specs = [row] + _ffn_specs(wg, wu, wd) + [_const_spec((1, D_MODEL))]
    outs = pl.pallas_call(
        _hosting(functools.partial(_ffn_kernel, final_norm=final_norm), len(in_specs), 1, len(casts)),
        grid=(n // tm,),
        out_shape=[jax.ShapeDtypeStruct((n, D_MODEL), F32)] + cast_shapes,
        in_specs=in_specs + cast_in,
        out_specs=[row] + cast_out,
        compiler_params=pltpu.CompilerParams(dimension_semantics=("parallel",), vmem_limit_bytes=VMEM_LIMIT),
        name="ffn",
    )(h, g, wg, wu, wd, gf, *[v for v, _ in casts])
    return outs[0], outs[1:]


def _conv_ffn_kernel(h_ref, g_ref, win_ref, cw_ref, wout_ref, zinit_ref, gf_ref, wg_ref, wu_ref, wd_ref,
                     out_ref, ztail_ref, zprev_scr, *, tm):
    @pl.when(pl.program_id(1) == 0)
    def _():
        zprev_scr[...] = zinit_ref[...]

    x = h_ref[...]
    a = _rms(x, g_ref[...]).astype(BF16)
    bcu = _dot(a, win_ref[...])
    gate_b = bcu[:, :D_MODEL]
    z = bcu[:, D_MODEL:2 * D_MODEL] * bcu[:, 2 * D_MODEL:]
    zp = zprev_scr[...]
    row = lax.broadcasted_iota(jnp.int32, (tm, D_MODEL), 0)
    last, last2 = zp[CONV_TAIL - 1:CONV_TAIL], zp[CONV_TAIL - 2:CONV_TAIL - 1]
    z1 = jnp.where(row == 0, last, pltpu.roll(z, 1, 0))
    z2 = jnp.where(row == 0, last2, jnp.where(row == 1, last, pltpu.roll(z, 2, 0)))
    cw = cw_ref[...]
    conv = cw[0:1] * z2 + cw[1:2] * z1 + cw[2:3] * z
    y = x + _dot((gate_b * conv).astype(BF16), wout_ref[...])
    out_ref[...] = _ffn_math(y, gf_ref, wg_ref, wu_ref, wd_ref)
    tail = z[tm - CONV_TAIL:]
    zprev_scr[...] = tail
    ztail_ref[...] = tail


def _conv_ffn_call(h, g, win, cw, wout, zinit, gf, wg, wu, wd, batch, tm, casts=()):
    n = h.shape[0]
    nt = n // batch // tm
    row = pl.BlockSpec((tm, D_MODEL), lambda b, i: (b * nt + i, 0))
    tail_shape = (CONV_TAIL, D_MODEL)
    cast_in, cast_out, cast_shapes = _cast_specs(casts, batch * nt, lambda b, i: b * nt + i)
    in_specs = [row, _const_spec((1, D_MODEL)), _const_spec(win.shape), _const_spec(cw.shape),
                _const_spec(wout.shape), _const_spec(tail_shape)] + _ffn_specs(wg, wu, wd)
    outs = pl.pallas_call(
        _hosting(functools.partial(_conv_ffn_kernel, tm=tm), len(in_specs), 2, len(casts)),
        grid=(batch, nt),
        out_shape=[jax.ShapeDtypeStruct((n, D_MODEL), F32), jax.ShapeDtypeStruct(tail_shape, F32)] + cast_shapes,
        in_specs=in_specs + cast_in,
        out_specs=[row, pl.BlockSpec(tail_shape, lambda b, i: (0, 0))] + cast_out,
        scratch_shapes=[pltpu.VMEM(tail_shape, F32)],
        compiler_params=pltpu.CompilerParams(dimension_semantics=("arbitrary", "arbitrary"),
                                             vmem_limit_bytes=VMEM_LIMIT),
        name="conv_ffn",
    )(h, g, win, cw, wout, zinit, gf, wg, wu, wd, *[v for v, _ in casts])
    return outs[0], outs[1], outs[2:]


def _pool_ffn_kernel(h_ref, g_ref, wp_ref, sc_ref, hinit_ref, gf_ref, wg_ref, wu_ref, wd_ref,
                     out_ref, atail_ref, hist_scr, *, tm, from_start):
    @pl.when(pl.program_id(1) == 0)
    def _():
        hist_scr[...] = hinit_ref[...]

    x = h_ref[...]
    a = _rms(x, g_ref[...])
    ext = jnp.concatenate([hist_scr[...], a], axis=0)
    outs = []
    for gi, win in enumerate(POOL_WINDOWS):
        cols = slice(gi * POOL_GROUP_DIM, (gi + 1) * POOL_GROUP_DIM)
        s = ext[:, cols]
        k = 1
        while k < win:
            s = s + pltpu.roll(s, k, 0)
            k *= 2
        s = s[POOL_HIST:]
        if from_start:
            pos = lax.broadcasted_iota(jnp.int32, (tm, POOL_GROUP_DIM), 0)
            mean = s / jnp.minimum(win, pos + 1).astype(F32)
        else:
            mean = s * (1.0 / win)
        mix = (mean - a[:, cols]).astype(BF16)
        outs.append(_dot(mix, wp_ref[gi]))
    y = x + jnp.concatenate(outs, axis=-1) * sc_ref[...]
    out_ref[...] = _ffn_math(y, gf_ref, wg_ref, wu_ref, wd_ref)
    tail = a[tm - POOL_HIST:]
    hist_scr[...] = tail
    atail_ref[...] = tail


def _pool_ffn_call(h, g, wp, sc, hinit, gf, wg, wu, wd, batch, tm, from_start):
    n = h.shape[0]
    nt = n // batch // tm
    row = pl.BlockSpec((tm, D_MODEL), lambda b, i: (b * nt + i, 0))
    hist_shape = (POOL_HIST, D_MODEL)
    return pl.pallas_call(
        functools.partial(_pool_ffn_kernel, tm=tm, from_start=from_start),
        grid=(batch, nt),
        out_shape=(jax.ShapeDtypeStruct((n, D_MODEL), F32), jax.ShapeDtypeStruct(hist_shape, F32)),
        in_specs=[row, _const_spec((1, D_MODEL)), _const_spec(wp.shape), _const_spec((1, D_MODEL)),
                  _const_spec(hist_shape)] + _ffn_specs(wg, wu, wd),
        out_specs=(row, pl.BlockSpec(hist_shape, lambda b, i: (0, 0))),
        scratch_shapes=[pltpu.VMEM(hist_shape, F32)],
        compiler_params=pltpu.CompilerParams(dimension_semantics=("arbitrary", "arbitrary"),
                                             vmem_limit_bytes=VMEM_LIMIT),
        name="pool_ffn",
    )(h, g, wp, sc, hinit, gf, wg, wu, wd)


ATTN_TM = 512
ATTN_LOOKAHEAD = 1
ROW_TM = 1024
CONV_TM = 1024


def kernel(x, meta_tokens, rel_bias_table, norm_mix, norm_ffn, norm_final, attn_w_qkv, attn_b_qkv, attn_w_o,
           attn_b_o, attn_sinks, conv_w_in, conv_w, conv_w_out, pool_w, pool_scale, ffn_w_gate, ffn_w_up,
           ffn_w_down):
    batch, seq, _ = x.shape
    depth = norm_mix.shape[0]
    n_mixers = 3
    hr = x.reshape(batch * seq, D_MODEL)
    hm = meta_tokens.astype(F32)
    row = lambda v: v.reshape(1, -1)
    meta_codes = _meta_codes()

    ffn_bf16, conv_bf16 = {}, {}

    def ffn_casts(layers):
        return [(w, l) for l in layers for w in (ffn_w_gate, ffn_w_up, ffn_w_down)]

    for i in range(depth):
        kind, j = i % n_mixers, i // n_mixers
        last = i == depth - 1
        gm, gf = row(norm_mix[i]), row(norm_ffn[i])
        if kind != 0:
            wg, wu, wd = ffn_bf16[i]
        if kind == 0:
            b, bo = attn_b_qkv[j], row(attn_b_o[j])
            bk = row(b[Q_DIM:Q_DIM + KV_DIM])
            tbl_meta = _meta_table(meta_codes, rel_bias_table, attn_sinks[j])
            tbl_real = _real_table(rel_bias_table, attn_sinks[j])
            hm, km, vmT, wqvT, bqvT, wk, wo = _attn_meta_call(hm, gm, attn_w_qkv, j, row(b), tbl_meta,
                                                              attn_w_o, bo)
            hosted = [l for l in (i, i + 2) if l < depth and (l == i or l % n_mixers == 2)]
            hr, cast = _attn_call(hr, gm, wqvT, bqvT, wk, bk, km, vmT, tbl_real, wo, bo, batch, ATTN_TM,
                                  ffn_casts(hosted))
            for n, l in enumerate(hosted):
                ffn_bf16[l] = cast[3 * n:3 * n + 3]
            wg, wu, wd = ffn_bf16[i]
            if not last:
                hm, _ = _ffn_call(hm, gf, wg, wu, wd, row(norm_final), N_META, False)
            conv_next = i + 1 < depth and (i + 1) % n_mixers == 1
            casts = ([(conv_w_in, (i + 1) // n_mixers), (conv_w_out, (i + 1) // n_mixers)] + ffn_casts([i + 1])
                     if conv_next else [])
            hr, cast = _ffn_call(hr, gf, wg, wu, wd, row(norm_final), ROW_TM, last, casts)
            if conv_next:
                conv_bf16[i + 1], ffn_bf16[i + 1] = cast[:2], cast[2:]
        elif kind == 1:
            win, wout = conv_bf16[i]
            hm, ztail, _ = _conv_ffn_call(hm, gm, win, conv_w[j], wout, jnp.zeros((CONV_TAIL, D_MODEL), F32),
                                          gf, wg, wu, wd, 1, N_META)
            hr, _, _ = _conv_ffn_call(hr, gm, win, conv_w[j], wout, ztail, gf, wg, wu, wd, batch, CONV_TM)
        else:
            wp, sc = pool_w[j].astype(BF16), row(pool_scale[j])
            hm, atail = _pool_ffn_call(hm, gm, wp, sc, jnp.zeros((POOL_HIST, D_MODEL), F32),
                                       gf, wg, wu, wd, 1, N_META, True)
            hr, _ = _pool_ffn_call(hr, gm, wp, sc, atail, gf, wg, wu, wd, batch, ROW_TM, False)
    return hr.reshape(batch, seq, D_MODEL)
```

```python
import functools
import math

import numpy as np
import jax
import jax.numpy as jnp
from jax import lax
from jax.experimental import pallas as pl
from jax.experimental.pallas import tpu as pltpu

F32 = jnp.float32
BF16 = jnp.bfloat16

D_MODEL = 1024
N_META = 16
RMS_EPS = 1e-6
N_HEADS = 16
N_KV = 4
HEAD_DIM = 64
GROUP = N_HEADS // N_KV
Q_DIM = N_HEADS * HEAD_DIM
KV_DIM = N_KV * HEAD_DIM
WINDOW = 128
N_BUCKETS = 32
MAX_DISTANCE = 128
POOL_WINDOWS = (2, 4, 8, 16)
POOL_GROUP_DIM = D_MODEL // len(POOL_WINDOWS)
POOL_HIST = 16
CONV_TAIL = 8
LANES = 128
FFN_LEAD_ROWS = 256
BF16_SUBLANES = 16
LOG2E = math.log2(math.e)

Q_BLK = 128
KV_PAIR = 4
UNIT_LANES = KV_PAIR * GROUP * Q_BLK
BAND_ROWS = WINDOW + Q_BLK
SINK_ROW = BAND_ROWS + N_META
KEY_ROWS = SINK_ROW + N_META
N_NEAR = WINDOW // Q_BLK
ONES_ROWS = 16

META_KEY_COLS = 128
CODE_SINK = N_BUCKETS
CODE_MASKED = N_BUCKETS + 1

VMEM_LIMIT = 56 * 1024 * 1024


def _rel_bucket_np(dist):
    max_exact = N_BUCKETS // 2
    d = np.maximum(dist, 0)
    df = np.maximum(d, 1).astype(np.float64)
    large = max_exact + (np.log(df / max_exact) / math.log(MAX_DISTANCE / max_exact)
                         * (N_BUCKETS - max_exact)).astype(np.int64)
    large = np.minimum(large, N_BUCKETS - 1)
    return np.where(d < max_exact, d, large).astype(np.int32)


def _meta_codes():
    i = np.arange(N_META)[:, None]
    m = np.arange(N_META)[None, :]
    code = np.full((N_META, META_KEY_COLS), CODE_MASKED, np.int32)
    code[:, :N_META] = np.where(i >= m, _rel_bucket_np(i - m), CODE_MASKED)
    code[:, N_META] = CODE_SINK
    return code[None]


def _band_codes():
    return np.stack([_rel_bucket_np(np.arange(WINDOW)), np.zeros(WINDOW, np.int32)])


def _near_meta_codes():
    m = np.arange(N_META)[:, None]
    i = np.arange(Q_BLK)[None, :]
    return np.stack([_rel_bucket_np(N_META + n * Q_BLK + i - m) for n in range(N_NEAR)])


def _const_spec(shape):
    zeros = (0,) * len(shape)
    return pl.BlockSpec(shape, lambda *_: zeros, pipeline_mode=pl.Buffered(1))


def _hosting(kernel_fn, n_in, n_out, n_cast):
    def wrapped(*refs):
        ins, rest = refs[:n_in], refs[n_in:]
        cast_in, rest = rest[:n_cast], rest[n_cast:]
        outs, rest = rest[:n_out], rest[n_out:]
        cast_out, scratch = rest[:n_cast], rest[n_cast:]
        kernel_fn(*ins, *outs, *scratch)
        for src, dst in zip(cast_in, cast_out):
            dst[...] = src[...].astype(BF16)
    return wrapped


def _cast_specs(casts, steps, flat_step):
    in_specs, out_specs, out_shapes = [], [], []
    for w, layer in casts:
        _, rows, cols = w.shape
        n_slices = max(n for n in range(1, steps + 1)
                       if steps % n == 0 and rows % n == 0 and (rows // n) % BF16_SUBLANES == 0)
        per = steps // n_slices
        blk = rows // n_slices
        in_specs.append(pl.BlockSpec((None, blk, cols), lambda *idx, l=layer, p=per: (l, flat_step(*idx) // p, 0)))
        out_specs.append(pl.BlockSpec((blk, cols), lambda *idx, p=per: (flat_step(*idx) // p, 0)))
        out_shapes.append(jax.ShapeDtypeStruct((rows, cols), BF16))
    return in_specs, out_specs, out_shapes


def _smem_spec():
    return pl.BlockSpec(memory_space=pltpu.SMEM)


def _rms(x, g):
    return x * lax.rsqrt(jnp.mean(x * x, axis=-1, keepdims=True) + RMS_EPS) * g


def _dot(a, b):
    return jnp.dot(a, b, preferred_element_type=F32)


def _dot_nt(a, b):
    return lax.dot_general(a, b, (((1,), (1,)), ((), ())), preferred_element_type=F32)


def _dot_tn(a, b):
    return lax.dot_general(a, b, (((0,), (0,)), ((), ())), preferred_element_type=F32)


def _select_by_code(code, tbl_ref, h, otherwise):
    val = otherwise
    for b in range(N_BUCKETS):
        val = jnp.where(code == b, tbl_ref[b, h], val)
    return val


def _meta_table_kernel(code_ref, tbl_ref, sink_ref, out_ref):
    n_var, rows, _ = code_ref.shape
    for v in range(n_var):
        code = code_ref[v]
        for h in range(N_HEADS):
            kh, g = divmod(h, GROUP)
            val = _select_by_code(code, tbl_ref, h, jnp.where(code == CODE_SINK, sink_ref[h], -jnp.inf))
            out_ref[v, kh, g * rows:(g + 1) * rows, :] = val


def _meta_table(codes, rel_table, sinks):
    n_var, rows, cols = codes.shape
    return pl.pallas_call(
        _meta_table_kernel,
        out_shape=jax.ShapeDtypeStruct((n_var, N_KV, GROUP * rows, cols), F32),
        in_specs=[pl.BlockSpec(memory_space=pltpu.VMEM), _smem_spec(), _smem_spec()],
        out_specs=pl.BlockSpec(memory_space=pltpu.VMEM),
        name="meta_bias_table",
    )(jnp.asarray(codes), rel_table, sinks)


def _real_table_kernel(band_code_ref, meta_code_ref, tbl_ref, sink_ref, out_ref):
    neg = -jnp.inf
    band_code = band_code_ref[0:1, :]
    t = lax.broadcasted_iota(jnp.int32, (BAND_ROWS, WINDOW), 0)
    i = lax.broadcasted_iota(jnp.int32, (BAND_ROWS, WINDOW), 1)
    in_window = (i < t) & (t <= i + WINDOW)
    row16 = lax.broadcasted_iota(jnp.int32, (N_META, Q_BLK), 0)
    for h in range(N_HEADS):
        pair, lane_blk = divmod(h, KV_PAIR * GROUP)
        lanes = slice(lane_blk * Q_BLK, (lane_blk + 1) * Q_BLK)
        by_dist = _select_by_code(band_code, tbl_ref, h, jnp.zeros((1, WINDOW), F32)) * LOG2E
        rolled = pltpu.roll(jnp.broadcast_to(by_dist, (WINDOW, WINDOW)), 0, 1, stride=1, stride_axis=0)
        rolled = jnp.concatenate([rolled, rolled[:BAND_ROWS - WINDOW]], axis=0)
        far = jnp.full((N_META, Q_BLK), tbl_ref[N_BUCKETS - 1, h], F32)
        sink_pad = jnp.where(row16 == 0, sink_ref[h] * LOG2E, neg)
        for v in range(1 + N_NEAR):
            if v == 0:
                valid, meta = in_window, far * LOG2E
            else:
                first_key_row = WINDOW - (v - 1) * Q_BLK
                valid = in_window & (t >= first_key_row)
                meta = _select_by_code(meta_code_ref[v - 1], tbl_ref, h, far) * LOG2E
            out_ref[v, pair, 0:BAND_ROWS, lanes] = jnp.where(valid, rolled, neg)[:, :Q_BLK]
            out_ref[v, pair, BAND_ROWS:SINK_ROW, lanes] = meta
            out_ref[v, pair, SINK_ROW:KEY_ROWS, lanes] = sink_pad


def _real_table(rel_table, sinks):
    vmem = pl.BlockSpec(memory_space=pltpu.VMEM)
    return pl.pallas_call(
        _real_table_kernel,
        out_shape=jax.ShapeDtypeStruct((1 + N_NEAR, N_KV // KV_PAIR, KEY_ROWS, UNIT_LANES), F32),
        in_specs=[vmem, vmem, _smem_spec(), _smem_spec()],
        out_specs=vmem,
        name="real_bias_table",
    )(jnp.asarray(_band_codes()), jnp.asarray(_near_meta_codes()), rel_table, sinks)


def _attn_kernel(h_ref, g_ref, wqv_ref, bqv_ref, wk_ref, bk_ref, km_ref, vmT_ref, tbl_ref, wo_ref, bo_ref,
                 out_ref, qT_scr, k_scr, vT_scr, oT_scr, *, tm):
    first_tile = pl.program_id(1) == 0

    @pl.when(first_tile)
    def _():
        k_scr[0:WINDOW, :] = jnp.zeros((WINDOW, KV_DIM), BF16)
        vT_scr[:, 0:WINDOW] = jnp.zeros((KV_DIM, WINDOW), BF16)

    @pl.when(jnp.logical_not(first_tile))
    def _():
        k_scr[0:WINDOW, :] = k_scr[tm:tm + WINDOW, :]
        vT_scr[:, 0:WINDOW] = vT_scr[:, tm:tm + WINDOW]

    x = h_ref[...]
    a = _rms(x, g_ref[...]).astype(BF16)
    k_scr[WINDOW:, :] = (_dot(a, wk_ref[...]) + bk_ref[...]).astype(BF16)
    bias = jnp.concatenate([bqv_ref[...]] * (tm // LANES), axis=1)
    qvT = _dot_nt(wqv_ref[...], a) + bias
    qT_scr[...] = (qvT[:Q_DIM] * (HEAD_DIM ** -0.5 * LOG2E)).astype(BF16)
    vT_scr[:, WINDOW:] = qvT[Q_DIM:].astype(BF16)

    kext = k_scr[...]
    vText = vT_scr[...]
    km = km_ref[...]
    vmT = vmT_ref[...]
    pair_dim = KV_PAIR * HEAD_DIM
    kv_lanes = GROUP * Q_BLK
    zk = jnp.zeros((KEY_ROWS - SINK_ROW, pair_dim), BF16)
    zq = jnp.zeros((HEAD_DIM, kv_lanes), BF16)
    ones = jnp.ones((ONES_ROWS, KEY_ROWS), BF16)

    def scores(j, pair):
        r0 = j * Q_BLK
        cols = slice(pair * pair_dim, (pair + 1) * pair_dim)
        blocks = []
        for c in range(KV_PAIR):
            kh = pair * KV_PAIR + c
            qsT = jnp.concatenate(
                [qT_scr[(kh * GROUP + g) * HEAD_DIM:(kh * GROUP + g + 1) * HEAD_DIM, r0:r0 + Q_BLK]
                 for g in range(GROUP)], axis=1)
            blocks.append(jnp.concatenate([zq] * c + [qsT] + [zq] * (KV_PAIR - 1 - c), axis=1))
        qsT = jnp.concatenate(blocks, axis=0)
        keys = jnp.concatenate([kext[r0:r0 + BAND_ROWS, cols], km[:, cols], zk], axis=0)
        return _dot(keys, qsT)

    def finish(j, pair, sT, sT_next):
        r0 = j * Q_BLK
        cols = slice(pair * pair_dim, (pair + 1) * pair_dim)
        var = jnp.where(first_tile, 1 + j, 0) if j < N_NEAR else 0
        sT = sT + tbl_ref[var, pair]
        m = jnp.max(sT, axis=0, keepdims=True)
        pT = jnp.exp2(sT - m).astype(BF16)
        valsT = jnp.concatenate([vText[cols, r0:r0 + BAND_ROWS], vmT[cols, :]], axis=1)
        den = ones
        if sT_next is not None:
            bits = pltpu.bitcast(sT_next[KEY_ROWS - ONES_ROWS:, :LANES], jnp.uint32)
            zero = pltpu.bitcast((bits >> 16) >> 16, F32).astype(BF16)
            den = jnp.concatenate([ones[:, :LANES] + zero, ones[:, LANES:]], axis=1)
        for c in range(KV_PAIR):
            kh = pair * KV_PAIR + c
            lanes = slice(c * kv_lanes, (c + 1) * kv_lanes)
            oT = _dot(jnp.concatenate([valsT[c * HEAD_DIM:(c + 1) * HEAD_DIM], den], axis=0), pT[:, lanes])
            o = oT[:HEAD_DIM] / oT[HEAD_DIM:HEAD_DIM + 1]
            for g in range(GROUP):
                h0 = (kh * GROUP + g) * HEAD_DIM
                oT_scr[h0:h0 + HEAD_DIM, r0:r0 + Q_BLK] = o[:, g * Q_BLK:(g + 1) * Q_BLK].astype(BF16)

    units = [(j, pair) for j in range(tm // Q_BLK) for pair in range(N_KV // KV_PAIR)]
    pending = [scores(*u) for u in units[:ATTN_LOOKAHEAD]]
    for idx, unit in enumerate(units):
        ahead = idx + ATTN_LOOKAHEAD
        if ahead < len(units):
            pending.append(scores(*units[ahead]))
        sT = pending.pop(0)
        finish(*unit, sT, pending[-1] if pending else None)
    out_ref[...] = x + _dot_tn(oT_scr[...], wo_ref[...]) + bo_ref[...]


def _attn_call(h, g, wqvT, bqvT, wk, bk, km, vmT, tbl, wo, bo, batch, tm, casts=()):
    n = h.shape[0]
    nt = n // batch // tm
    row = pl.BlockSpec((tm, D_MODEL), lambda b, i: (b * nt + i, 0))
    cast_in, cast_out, cast_shapes = _cast_specs(casts, batch * nt, lambda b, i: b * nt + i)
    in_specs = [row, _const_spec((1, D_MODEL)), _const_spec(wqvT.shape), _const_spec(bqvT.shape),
                _const_spec(wk.shape), _const_spec((1, KV_DIM)), _const_spec(km.shape), _const_spec(vmT.shape),
                _const_spec(tbl.shape), _const_spec(wo.shape), _const_spec((1, D_MODEL))]
    outs = pl.pallas_call(
        _hosting(functools.partial(_attn_kernel, tm=tm), len(in_specs), 1, len(casts)),
        grid=(batch, nt),
        out_shape=[jax.ShapeDtypeStruct((n, D_MODEL), F32)] + cast_shapes,
        in_specs=in_specs + cast_in,
        out_specs=[row] + cast_out,
        scratch_shapes=[pltpu.VMEM((Q_DIM, tm), BF16), pltpu.VMEM((WINDOW + tm, KV_DIM), BF16),
                        pltpu.VMEM((KV_DIM, WINDOW + tm), BF16), pltpu.VMEM((Q_DIM, tm), BF16)],
        compiler_params=pltpu.CompilerParams(dimension_semantics=("arbitrary", "arbitrary"),
                                             vmem_limit_bytes=VMEM_LIMIT),
        name="attn",
    )(h, g, wqvT, bqvT, wk, bk, km, vmT, tbl, wo, bo, *[v for v, _ in casts])
    return outs[0], outs[1:]


def _attn_meta_kernel(h_ref, g_ref, w_ref, b_ref, tbl_ref, wo_ref, bo_ref,
                      out_ref, km_ref, vmT_ref, wqvT_ref, bqvT_ref, wk_ref, wo16_ref):
    w = w_ref[...]
    b = b_ref[...]
    wq, wk, wv = w[:, :Q_DIM], w[:, Q_DIM:Q_DIM + KV_DIM], w[:, Q_DIM + KV_DIM:]
    wqvT = jnp.concatenate([wq.T, wv.T], axis=0).astype(BF16)
    b_qv = jnp.concatenate([b[:, :Q_DIM], b[:, Q_DIM + KV_DIM:]], axis=1)
    bqvT = jnp.broadcast_to(b_qv, (LANES, Q_DIM + KV_DIM)).T
    wk16 = wk.astype(BF16)
    wo16 = wo_ref[...].astype(BF16)
    wqvT_ref[...] = wqvT
    bqvT_ref[...] = bqvT
    wk_ref[...] = wk16
    wo16_ref[...] = wo16

    x = h_ref[...]
    a = _rms(x, g_ref[...]).astype(BF16)
    a2 = jnp.concatenate([a, jnp.zeros_like(a)], axis=0)
    lane = lax.broadcasted_iota(jnp.int32, (KV_DIM, 2 * N_META), 1)
    vT = _dot_nt(wqvT[Q_DIM:], a2) + jnp.where(lane < N_META, bqvT[Q_DIM:, :2 * N_META], 0.0)
    vmT_ref[...] = vT.astype(BF16)
    q = ((_dot_nt(a, wqvT[:Q_DIM]) + b[:, :Q_DIM]) * (HEAD_DIM ** -0.5)).astype(BF16)
    k = (_dot(a, wk16) + b[:, Q_DIM:Q_DIM + KV_DIM]).astype(BF16)
    v = (_dot_nt(a, wqvT[Q_DIM:]) + b[:, Q_DIM + KV_DIM:]).astype(BF16)
    km_ref[...] = k
    pad = META_KEY_COLS - N_META
    vcat = jnp.concatenate([v, jnp.zeros((pad, KV_DIM), BF16)], axis=0)
    pieces = []
    for kh in range(N_KV):
        cols = slice(kh * HEAD_DIM, (kh + 1) * HEAD_DIM)
        qs = jnp.concatenate([q[:, (kh * GROUP + g) * HEAD_DIM:(kh * GROUP + g + 1) * HEAD_DIM]
                              for g in range(GROUP)], axis=0)
        kcat = jnp.concatenate([k[:, cols], jnp.zeros((pad, HEAD_DIM), BF16)], axis=0)
        s = _dot_nt(qs, kcat) + tbl_ref[0, kh]
        p = jnp.exp(s - jnp.max(s, axis=-1, keepdims=True))
        l = jnp.sum(p, axis=-1, keepdims=True)
        o = _dot(p.astype(BF16), vcat)[:, cols] / l
        pieces += [o[g * N_META:(g + 1) * N_META] for g in range(GROUP)]
    o_all = jnp.concatenate(pieces, axis=-1).astype(BF16)
    out_ref[...] = x + _dot(o_all, wo16) + bo_ref[...]


def _attn_meta_call(h, g, w_stack, layer, b, tbl, wo_stack, bo):
    whole = lambda shape: pl.BlockSpec(shape, lambda i: (0,) * len(shape))
    of_layer = lambda stack: pl.BlockSpec((None,) + stack.shape[1:], lambda i: (layer, 0, 0))
    out_shapes = ((N_META, D_MODEL, F32), (N_META, KV_DIM, BF16), (KV_DIM, 2 * N_META, BF16),
                  (Q_DIM + KV_DIM, D_MODEL, BF16), (Q_DIM + KV_DIM, LANES, F32), (D_MODEL, KV_DIM, BF16),
                  (Q_DIM, D_MODEL, BF16))
    return pl.pallas_call(
        _attn_meta_kernel,
        grid=(1,),
        out_shape=tuple(jax.ShapeDtypeStruct((r, c), dt) for r, c, dt in out_shapes),
        in_specs=[whole(h.shape), whole(g.shape), of_layer(w_stack), whole(b.shape), whole(tbl.shape),
                  of_layer(wo_stack), whole(bo.shape)],
        out_specs=tuple(whole((r, c)) for r, c, _ in out_shapes),
        compiler_params=pltpu.CompilerParams(dimension_semantics=("arbitrary",), vmem_limit_bytes=VMEM_LIMIT),
        name="attn_meta",
    )(h, g, w_stack, b, tbl, wo_stack, bo)


def _ffn_math(x, g_ref, wg_ref, wu_ref, wd_ref):
    rows = x.shape[0]
    lead = FFN_LEAD_ROWS if rows > 2 * FFN_LEAD_ROWS else rows
    chunks = [x[:lead]] + ([x[lead:]] if lead < rows else [])
    hmids = []
    for xc in chunks:
        a = _rms(xc, g_ref[...]).astype(BF16)
        gate = _dot(a, wg_ref[...])
        up = _dot(a, wu_ref[...])
        hmids.append((gate * (1.0 / (1.0 + jnp.exp(-gate))) * up).astype(BF16))
    outs = [xc + _dot(hm, wd_ref[...]) for xc, hm in zip(chunks, hmids)]
    return outs[0] if len(outs) == 1 else jnp.concatenate(outs, axis=0)


def _ffn_kernel(h_ref, g_ref, wg_ref, wu_ref, wd_ref, gf_ref, out_ref, *, final_norm):
    y = _ffn_math(h_ref[...], g_ref, wg_ref, wu_ref, wd_ref)
    out_ref[...] = _rms(y, gf_ref[...]) if final_norm else y


def _ffn_specs(wg, wu, wd):
    return [_const_spec((1, D_MODEL)), _const_spec(wg.shape), _const_spec(wu.shape), _const_spec(wd.shape)]


def _ffn_call(h, g, wg, wu, wd, gf, tm, final_norm, casts=()):
    n = h.shape[0]
    row = pl.BlockSpec((tm, D_MODEL), lambda i: (i, 0))
    cast_in, cast_out, cast_shapes = _cast_specs(casts, n // tm, lambda i: i)
    in_specs = [row] + _ffn_specs(wg, wu, wd) + [_const_spec((1, D_MODEL))]
    outs = pl.pallas_call(
        _hosting(functools.partial(_ffn_kernel, final_norm=final_norm), len(in_specs), 1, len(casts)),
        grid=(n // tm,),
        out_shape=[jax.ShapeDtypeStruct((n, D_MODEL), F32)] + cast_shapes,
        in_specs=in_specs + cast_in,
        out_specs=[row] + cast_out,
        compiler_params=pltpu.CompilerParams(dimension_semantics=("parallel",), vmem_limit_bytes=VMEM_LIMIT),
        name="ffn",
    )(h, g, wg, wu, wd, gf, *[v for v, _ in casts])
    return outs[0], outs[1:]


def _conv_ffn_kernel(h_ref, g_ref, win_ref, cw_ref, wout_ref, zinit_ref, gf_ref, wg_ref, wu_ref, wd_ref,
                     out_ref, ztail_ref, zprev_scr, *, tm):
    @pl.when(pl.program_id(1) == 0)
    def _():
        zprev_scr[...] = zinit_ref[...]

    x = h_ref[...]
    a = _rms(x, g_ref[...]).astype(BF16)
    bcu = _dot(a, win_ref[...])
    gate_b = bcu[:, :D_MODEL]
    z = bcu[:, D_MODEL:2 * D_MODEL] * bcu[:, 2 * D_MODEL:]
    zp = zprev_scr[...]
    row = lax.broadcasted_iota(jnp.int32, (tm, D_MODEL), 0)
    last, last2 = zp[CONV_TAIL - 1:CONV_TAIL], zp[CONV_TAIL - 2:CONV_TAIL - 1]
    z1 = jnp.where(row == 0, last, pltpu.roll(z, 1, 0))
    z2 = jnp.where(row == 0, last2, jnp.where(row == 1, last, pltpu.roll(z, 2, 0)))
    cw = cw_ref[...]
    conv = cw[0:1] * z2 + cw[1:2] * z1 + cw[2:3] * z
    y = x + _dot((gate_b * conv).astype(BF16), wout_ref[...])
    out_ref[...] = _ffn_math(y, gf_ref, wg_ref, wu_ref, wd_ref)
    tail = z[tm - CONV_TAIL:]
    zprev_scr[...] = tail
    ztail_ref[...] = tail


def _conv_ffn_call(h, g, win, cw, wout, zinit, gf, wg, wu, wd, batch, tm, casts=()):
    n = h.shape[0]
    nt = n // batch // tm
    row = pl.BlockSpec((tm, D_MODEL), lambda b, i: (b * nt + i, 0))
    tail_shape = (CONV_TAIL, D_MODEL)
    cast_in, cast_out, cast_shapes = _cast_specs(casts, batch * nt, lambda b, i: b * nt + i)
    in_specs = [row, _const_spec((1, D_MODEL)), _const_spec(win.shape), _const_spec(cw.shape),
                _const_spec(wout.shape), _const_spec(tail_shape)] + _ffn_specs(wg, wu, wd)
    outs = pl.pallas_call(
        _hosting(functools.partial(_conv_ffn_kernel, tm=tm), len(in_specs), 2, len(casts)),
        grid=(batch, nt),
        out_shape=[jax.ShapeDtypeStruct((n, D_MODEL), F32), jax.ShapeDtypeStruct(tail_shape, F32)] + cast_shapes,
        in_specs=in_specs + cast_in,
        out_specs=[row, pl.BlockSpec(tail_shape, lambda b, i: (0, 0))] + cast_out,
        scratch_shapes=[pltpu.VMEM(tail_shape, F32)],
        compiler_params=pltpu.CompilerParams(dimension_semantics=("arbitrary", "arbitrary"),
                                             vmem_limit_bytes=VMEM_LIMIT),
        name="conv_ffn",
    )(h, g, win, cw, wout, zinit, gf, wg, wu, wd, *[v for v, _ in casts])
    return outs[0], outs[1], outs[2:]


def _pool_ffn_kernel(h_ref, g_ref, wp_ref, sc_ref, hinit_ref, gf_ref, wg_ref, wu_ref, wd_ref,
                     out_ref, atail_ref, hist_scr, *, tm, from_start):
    @pl.when(pl.program_id(1) == 0)
    def _():
        hist_scr[...] = hinit_ref[...]

    x = h_ref[...]
    a = _rms(x, g_ref[...])
    ext = jnp.concatenate([hist_scr[...], a], axis=0)
    outs = []
    for gi, win in enumerate(POOL_WINDOWS):
        cols = slice(gi * POOL_GROUP_DIM, (gi + 1) * POOL_GROUP_DIM)
        s = ext[:, cols]
        k = 1
        while k < win:
            s = s + pltpu.roll(s, k, 0)
            k *= 2
        s = s[POOL_HIST:]
        if from_start:
            pos = lax.broadcasted_iota(jnp.int32, (tm, POOL_GROUP_DIM), 0)
            mean = s / jnp.minimum(win, pos + 1).astype(F32)
        else:
            mean = s * (1.0 / win)
        mix = (mean - a[:, cols]).astype(BF16)
        outs.append(_dot(mix, wp_ref[gi]))
    y = x + jnp.concatenate(outs, axis=-1) * sc_ref[...]
    out_ref[...] = _ffn_math(y, gf_ref, wg_ref, wu_ref, wd_ref)
    tail = a[tm - POOL_HIST:]
    hist_scr[...] = tail
    atail_ref[...] = tail


def _pool_ffn_call(h, g, wp, sc, hinit, gf, wg, wu, wd, batch, tm, from_start):
    n = h.shape[0]
    nt = n // batch // tm
    row = pl.BlockSpec((tm, D_MODEL), lambda b, i: (b * nt + i, 0))
    hist_shape = (POOL_HIST, D_MODEL)
    return pl.pallas_call(
        functools.partial(_pool_ffn_kernel, tm=tm, from_start=from_start),
        grid=(batch, nt),
        out_shape=(jax.ShapeDtypeStruct((n, D_MODEL), F32), jax.ShapeDtypeStruct(hist_shape, F32)),
        in_specs=[row, _const_spec((1, D_MODEL)), _const_spec(wp.shape), _const_spec((1, D_MODEL)),
                  _const_spec(hist_shape)] + _ffn_specs(wg, wu, wd),
        out_specs=(row, pl.BlockSpec(hist_shape, lambda b, i: (0, 0))),
        scratch_shapes=[pltpu.VMEM(hist_shape, F32)],
        compiler_params=pltpu.CompilerParams(dimension_semantics=("arbitrary", "arbitrary"),
                                             vmem_limit_bytes=VMEM_LIMIT),
        name="pool_ffn",
    )(h, g, wp, sc, hinit, gf, wg, wu, wd)


ATTN_TM = 512
ATTN_LOOKAHEAD = 1
ROW_TM = 1024
CONV_TM = 1024


def kernel(x, meta_tokens, rel_bias_table, norm_mix, norm_ffn, norm_final, attn_w_qkv, attn_b_qkv, attn_w_o,
           attn_b_o, attn_sinks, conv_w_in, conv_w, conv_w_out, pool_w, pool_scale, ffn_w_gate, ffn_w_up,
           ffn_w_down):
    batch, seq, _ = x.shape
    depth = norm_mix.shape[0]
    n_mixers = 3
    hr = x.reshape(batch * seq, D_MODEL)
    hm = meta_tokens.astype(F32)
    row = lambda v: v.reshape(1, -1)
    meta_codes = _meta_codes()

    ffn_bf16, conv_bf16 = {}, {}

    def ffn_casts(layers):
        return [(w, l) for l in layers for w in (ffn_w_gate, ffn_w_up, ffn_w_down)]

    for i in range(depth):
        kind, j = i % n_mixers, i // n_mixers
        last = i == depth - 1
        gm, gf = row(norm_mix[i]), row(norm_ffn[i])
        if kind != 0:
            wg, wu, wd = ffn_bf16[i]
        if kind == 0:
            b, bo = attn_b_qkv[j], row(attn_b_o[j])
            bk = row(b[Q_DIM:Q_DIM + KV_DIM])
            tbl_meta = _meta_table(meta_codes, rel_bias_table, attn_sinks[j])
            tbl_real = _real_table(rel_bias_table, attn_sinks[j])
            hm, km, vmT, wqvT, bqvT, wk, wo = _attn_meta_call(hm, gm, attn_w_qkv, j, row(b), tbl_meta,
                                                              attn_w_o, bo)
            hosted = [l for l in (i, i + 2) if l < depth and (l == i or l % n_mixers == 2)]
            hr, cast = _attn_call(hr, gm, wqvT, bqvT, wk, bk, km, vmT, tbl_real, wo, bo, batch, ATTN_TM,
                                  ffn_casts(hosted))
            for n, l in enumerate(hosted):
                ffn_bf16[l] = cast[3 * n:3 * n + 3]
            wg, wu, wd = ffn_bf16[i]
            if not last:
                hm, _ = _ffn_call(hm, gf, wg, wu, wd, row(norm_final), N_META, False)
            conv_next = i + 1 < depth and (i + 1) % n_mixers == 1
            casts = ([(conv_w_in, (i + 1) // n_mixers), (conv_w_out, (i + 1) // n_mixers)] + ffn_casts([i + 1])
                     if conv_next else [])
            hr, cast = _ffn_call(hr, gf, wg, wu, wd, row(norm_final), ROW_TM, last, casts)
            if conv_next:
                conv_bf16[i + 1], ffn_bf16[i + 1] = cast[:2], cast[2:]
        elif kind == 1:
            win, wout = conv_bf16[i]
            hm, ztail, _ = _conv_ffn_call(hm, gm, win, conv_w[j], wout, jnp.zeros((CONV_TAIL, D_MODEL), F32),
                                          gf, wg, wu, wd, 1, N_META)
            hr, _, _ = _conv_ffn_call(hr, gm, win, conv_w[j], wout, ztail, gf, wg, wu, wd, batch, CONV_TM)
        else:
            wp, sc = pool_w[j].astype(BF16), row(pool_scale[j])
            hm, atail = _pool_ffn_call(hm, gm, wp, sc, jnp.zeros((POOL_HIST, D_MODEL), F32),
                                       gf, wg, wu, wd, 1, N_META, True)
            hr, _ = _pool_ffn_call(hr, gm, wp, sc, atail, gf, wg, wu, wd, batch, ROW_TM, False)
    return hr.reshape(batch, seq, D_MODEL)
```
